```python
import math
import jax
import jax.numpy as jnp
from jax import lax
import numpy as np

D_MODEL = 1024
BATCH = 4
SEQ = 4096
DEPTH = 4
DEC_BATCH = 128
DEC_SEQ = 1
PAST_LEN = 8192
PAGE_SIZE = 128

N_MIXERS = 4
HEAD_DIM = 64
N_HEADS = D_MODEL // HEAD_DIM
Q_BLOCK = 128
D_FF = ((8 * D_MODEL + 3 * 256 - 1) // (3 * 256)) * 256
P_DIM = 256
ALPHA = (2 * DEPTH) ** 0.25
BETA = (8 * DEPTH) ** -0.25
LN_EPS = 1e-5
NEG_INF = -1e30

N_BUCKETS = 32
MAX_DISTANCE = 2048

NSA_KV_HEADS = 1
NSA_GROUP = N_HEADS // NSA_KV_HEADS
CMP_BLOCK = 32
CMP_STRIDE = 16
CMP_HIDDEN = 2 * HEAD_DIM
SLC_BLOCK = 64
N_SELECT = 16
NSA_WINDOW = 512
FORCE_BONUS = 1e4
NSA_IN = D_MODEL + 6 * NSA_KV_HEADS * HEAD_DIM + 3 * N_HEADS

MLA_HEADS = N_HEADS
Q_LORA = 256
KV_LORA = 256
QK_NOPE = 64
QK_ROPE = 32
V_HEAD = 64
ROPE_THETA = 10000.0
MLA_IN = Q_LORA + KV_LORA + QK_ROPE

DIL_PATTERNS = ((128, 1), (512, 4), (2048, 16))
N_DIL = len(DIL_PATTERNS)
DIL_KV_HEADS = 4
DIL_GROUP = N_HEADS // DIL_KV_HEADS
DIL_GROUP_IN = D_MODEL + 2 * DIL_KV_HEADS * HEAD_DIM
DIL_IN = N_DIL * DIL_GROUP_IN

SB_KV_HEADS = 4
SB_GROUP = N_HEADS // SB_KV_HEADS
SB_IN = D_MODEL + 2 * SB_KV_HEADS * HEAD_DIM

kernel_name = 'hybrid_nsa_mla_dilated_stickbreak_decoder_step'


def layer_norm(x, g, b):
    xf = x.astype(jnp.float32)
    mu = jnp.mean(xf, axis=-1, keepdims=True)
    var = jnp.mean(jnp.square(xf - mu), axis=-1, keepdims=True)
    return ((xf - mu) * lax.rsqrt(var + LN_EPS) * g.astype(jnp.float32) + b.astype(jnp.float32)).astype(x.dtype)


def rms_norm(x, g):
    xf = x.astype(jnp.float32)
    return (xf * lax.rsqrt(jnp.mean(jnp.square(xf), axis=-1, keepdims=True) + LN_EPS) * g.astype(jnp.float32)).astype(x.dtype)


def swiglu(x, wg, wu, wd):
    return (jax.nn.silu(x @ wg) * (x @ wu)) @ wd


def rope(x, pos):
    half = x.shape[-1] // 2
    inv = ROPE_THETA ** (-jnp.arange(half, dtype=jnp.float32) / half)
    ang = pos.astype(jnp.float32)[:, None] * inv[None, :]
    ang = ang.reshape((ang.shape[0],) + (1,) * (x.ndim - 3) + (half,))
    cos, sin = jnp.cos(ang), jnp.sin(ang)
    xf = x.astype(jnp.float32)
    x1, x2 = xf[..., :half], xf[..., half:]
    return jnp.concatenate([x1 * cos - x2 * sin, x1 * sin + x2 * cos], axis=-1).astype(x.dtype)


def t5_bucket(dist):
    max_exact = N_BUCKETS // 2
    n = jnp.maximum(dist, 0)
    log_ratio = jnp.log(jnp.maximum(n, max_exact).astype(jnp.float32) / max_exact) / math.log(MAX_DISTANCE / max_exact)
    large = jnp.minimum(max_exact + (log_ratio * (N_BUCKETS - max_exact)).astype(jnp.int32), N_BUCKETS - 1)
    return jnp.where(n < max_exact, n, large)


def t5_bias_gqa(table, dist, n_kv):
    b = jnp.moveaxis(table.astype(jnp.float32)[t5_bucket(dist)], -1, 0)
    return b.reshape((n_kv, -1) + b.shape[1:])


def masked_softmax(s, valid):
    s = jnp.where(valid, s, NEG_INF)
    m = jnp.max(s, axis=-1, keepdims=True)
    e = jnp.where(valid, jnp.exp(s - m), 0.0)
    den = jnp.maximum(jnp.sum(e, axis=-1, keepdims=True), 1e-30)
    return e / den, (m + jnp.log(den))[..., 0]


def dense_gqa(q, k, v, valid, bias):
    s = jnp.einsum('bqhgd,bkhd->bhgqk', q, k).astype(jnp.float32) * (q.shape[-1] ** -0.5)
    if bias is not None:
        s = s + bias
    p, lse = masked_softmax(s, valid)
    return jnp.einsum('bhgqk,bkhd->bqhgd', p.astype(v.dtype), v), lse


def sweep_queries(block_fn, n_q):
    qb = Q_BLOCK if n_q % Q_BLOCK == 0 else n_q
    nb = n_q // qb
    if nb == 1:
        return block_fn(0, qb)
    ys = lax.map(lambda i: block_fn(i * qb, qb), jnp.arange(nb))
    return jnp.moveaxis(ys, 0, 1).reshape((ys.shape[1], n_q) + ys.shape[3:])


def gather_pages(pool, page_table):
    rows = pool[page_table]
    return rows.reshape((rows.shape[0], rows.shape[1] * rows.shape[2]) + rows.shape[3:])


def residual_tail(x, mix, p, ln1g, ln1b, ln2g, ln2b, wg, wu, wd, pwg, pwp):
    x = layer_norm(ALPHA * x + mix, ln1g, ln1b)
    x = layer_norm(ALPHA * x + swiglu(x, wg, wu, wd), ln2g, ln2b)
    return x + jax.nn.sigmoid(x @ pwg) * (p @ pwp)


def nsa_split(x, w_in):
    b, t, _ = x.shape
    kvd = NSA_KV_HEADS * HEAD_DIM
    h = x @ w_in
    q = h[..., :D_MODEL].reshape(b, t, NSA_KV_HEADS, NSA_GROUP, HEAD_DIM)
    kv = h[..., D_MODEL:D_MODEL + 6 * kvd].reshape(b, t, 6, NSA_KV_HEADS, HEAD_DIM)
    gate = h[..., D_MODEL + 6 * kvd:].reshape(b, t, 3, NSA_KV_HEADS, NSA_GROUP)
    return q, kv, gate


def nsa_compress(kv, pe, w1, w2):
    b, l, hk, d = kv.shape
    r = CMP_BLOCK // CMP_STRIDE
    lp = -(-l // CMP_STRIDE) * CMP_STRIDE
    halves = jnp.pad(kv, ((0, 0), (0, lp - l), (0, 0), (0, 0))).reshape(b, lp // CMP_STRIDE, CMP_STRIDE, hk, d)
    n_c = lp // CMP_STRIDE - r + 1
    w1r = w1.reshape(r, CMP_STRIDE, d, CMP_HIDDEN)
    h = jnp.einsum('ld,lde->e', pe, w1.reshape(CMP_BLOCK, d, CMP_HIDDEN))
    for i in range(r):
        h = h + jnp.einsum('bnshd,sde->bnhe', halves[:, i:i + n_c], w1r[i])
    return jax.nn.gelu(h) @ w2


def nsa_block(qb_, qpos, gb, kc, vc, cend, ksb, vsb, kwb, vwb, kwpos, table):
    b, tq, hk, g, d = qb_.shape
    scale = d ** -0.5
    valid_c = cend[None, :] <= qpos[:, None] + 1
    s_c = jnp.einsum('bqhgd,bnhd->bhgqn', qb_, kc).astype(jnp.float32) * scale
    s_c = s_c + t5_bias_gqa(table, qpos[:, None] - (cend[None, :] - 1), hk)
    p_c, _ = masked_softmax(s_c, valid_c)
    o_c = jnp.einsum('bhgqn,bnhd->bqhgd', p_c.astype(vc.dtype), vc)
    n_s = ksb.shape[2]
    sstart = jnp.arange(n_s) * SLC_BLOCK
    intersects = ((cend[:, None] - CMP_BLOCK < sstart[None, :] + SLC_BLOCK) & (cend[:, None] > sstart[None, :])).astype(jnp.float32)
    imp = jnp.einsum('bhgqn,nj->bhqj', p_c, intersects)
    qblk = qpos // SLC_BLOCK
    jb = jnp.arange(n_s)[None, :]
    forced = (jb == 0) | (jb == qblk[:, None]) | (jb == qblk[:, None] - 1)
    score = jnp.where(jb <= qblk[:, None], imp + jnp.where(forced, FORCE_BONUS, 0.0), NEG_INF)
    _, idx = lax.top_k(score, min(N_SELECT, n_s))
    ksel = idx.shape[-1]
    bi = jnp.arange(b)[:, None, None, None]
    hi = jnp.arange(hk)[None, :, None, None]
    kg = ksb[bi, hi, idx]
    vg = vsb[bi, hi, idx]
    kpos = idx[..., None] * SLC_BLOCK + jnp.arange(SLC_BLOCK)
    dist = qpos[None, None, :, None, None] - kpos
    bias = table.astype(jnp.float32).reshape(N_BUCKETS, hk, g)[t5_bucket(dist), hi[..., None]]
    s_s = jnp.einsum('bqhgd,bhqkld->bhgqkl', qb_, kg).astype(jnp.float32) * scale + jnp.moveaxis(bias, -1, 2)
    p_s, _ = masked_softmax(s_s.reshape(b, hk, g, tq, ksel * SLC_BLOCK), (dist >= 0).reshape(b, hk, 1, tq, ksel * SLC_BLOCK))
    o_s = jnp.einsum('bhgqkl,bhqkld->bqhgd', p_s.reshape(s_s.shape).astype(vg.dtype), vg)
    dist_w = qpos[:, None] - kwpos[None, :]
    valid_w = (dist_w >= 0) & (dist_w <= NSA_WINDOW) & (kwpos[None, :] >= 0)
    o_w, _ = dense_gqa(qb_, kwb, vwb, valid_w, t5_bias_gqa(table, dist_w, hk))
    gates = jax.nn.sigmoid(gb.astype(jnp.float32)).astype(o_c.dtype)[..., None]
    o = gates[:, :, 0] * o_c + gates[:, :, 1] * o_s + gates[:, :, 2] * o_w
    return o.reshape(b, tq, hk * g * d)


def nsa_attend(q, gate, kv_full, win_kv, q_pos0, win_pos0, pe, w1, w2, table):
    b, l = kv_full.shape[:2]
    kc = nsa_compress(kv_full[:, :, 0], pe[0], w1[0], w2[0])
    vc = nsa_compress(kv_full[:, :, 1], pe[1], w1[1], w2[1])
    cend = jnp.arange(kc.shape[1]) * CMP_STRIDE + CMP_BLOCK
    lp = -(-l // SLC_BLOCK) * SLC_BLOCK
    slc = jnp.pad(kv_full[:, :, 2:4], ((0, 0), (0, lp - l), (0, 0), (0, 0), (0, 0)))
    slc = jnp.transpose(slc.reshape(b, lp // SLC_BLOCK, SLC_BLOCK, 2, NSA_KV_HEADS, HEAD_DIM), (3, 0, 4, 1, 2, 5))
    tq = q.shape[1]
    lw = win_kv.shape[1] - tq

    def block(start, qb):
        qpos = q_pos0 + start + jnp.arange(qb)
        wblk = lax.dynamic_slice_in_dim(win_kv, start, lw + qb, 1)
        wpos = win_pos0 + start + jnp.arange(lw + qb)
        return nsa_block(lax.dynamic_slice_in_dim(q, start, qb, 1), qpos, lax.dynamic_slice_in_dim(gate, start, qb, 1),
                         kc, vc, cend, slc[0], slc[1], wblk[:, :, 0], wblk[:, :, 1], wpos, table)
    return sweep_queries(block, tq)


def nsa_layer(xp, xs, cache_kv, cache_win, page_table, w_in, pe, w1, w2, w_out, table):
    t = xp.shape[1]
    q, kv, gate = nsa_split(xp, w_in)
    win = jnp.pad(kv[:, :, 4:], ((0, 0), (NSA_WINDOW, 0), (0, 0), (0, 0), (0, 0)))
    yp = nsa_attend(q, gate, kv[:, :, :4], win, 0, -NSA_WINDOW, pe, w1, w2, table) @ w_out
    kv_p, win_p = kv[:, :, :4], kv[:, t - min(NSA_WINDOW, t):, 4:]
    q, kv, gate = nsa_split(xs, w_in)
    wb = cache_win.shape[1]
    full = jnp.concatenate([gather_pages(cache_kv, page_table), kv[:, :, :4]], axis=1)
    win = jnp.concatenate([cache_win, kv[:, :, 4:]], axis=1)
    ys = nsa_attend(q, gate, full, win, PAST_LEN, PAST_LEN - wb, pe, w1, w2, table) @ w_out
    return yp, ys, kv_p, kv[:, :, :4], win_p, win[:, win.shape[1] - wb:]


def mla_project(x, pos, w_dq, q_norm, kv_norm, w_uq):
    b, t, _ = x.shape
    h = x @ w_dq
    cq = rms_norm(h[..., :Q_LORA], q_norm)
    ckv = rms_norm(h[..., Q_LORA:Q_LORA + KV_LORA], kv_norm)
    kr = rope(h[..., Q_LORA + KV_LORA:], pos)
    q = (cq @ w_uq).reshape(b, t, MLA_HEADS, QK_NOPE + QK_ROPE)
    return q[..., :QK_NOPE], rope(q[..., QK_NOPE:], pos), jnp.concatenate([ckv, kr], axis=-1)


def mla_layer(xp, xs, cache, page_table, w_dq, q_norm, kv_norm, w_uq, w_uk, w_uv, w_out):
    b, t, _ = xp.shape
    pos = jnp.arange(t)
    q_nope, q_rope, lat_p = mla_project(xp, pos, w_dq, q_norm, kv_norm, w_uq)
    ckv, kr = lat_p[..., :KV_LORA], lat_p[..., KV_LORA:]
    k = jnp.concatenate([jnp.einsum('btc,chd->bthd', ckv, w_uk),
                         jnp.broadcast_to(kr[:, :, None, :], (b, t, MLA_HEADS, QK_ROPE))], axis=-1)
    v = jnp.einsum('btc,chd->bthd', ckv, w_uv)
    q = jnp.concatenate([q_nope, q_rope], axis=-1)[:, :, :, None, :]

    def block_p(start, qb):
        qpos = start + jnp.arange(qb)
        o, _ = dense_gqa(lax.dynamic_slice_in_dim(q, start, qb, 1), k, v, pos[None, :] <= qpos[:, None], None)
        return o.reshape(b, qb, MLA_HEADS * V_HEAD)
    yp = sweep_queries(block_p, t) @ w_out
    bs, ts, _ = xs.shape
    q_nope, q_rope, lat_s = mla_project(xs, PAST_LEN + jnp.arange(ts), w_dq, q_norm, kv_norm, w_uq)
    lat_all = jnp.concatenate([gather_pages(cache, page_table), lat_s], axis=1)
    ckv, kr = lat_all[..., :KV_LORA], lat_all[..., KV_LORA:]
    kpos = jnp.arange(lat_all.shape[1])
    q_lat = jnp.einsum('bqhd,chd->bqhc', q_nope, w_uk)
    scale = (QK_NOPE + QK_ROPE) ** -0.5

    def block_s(start, qb):
        qpos = PAST_LEN + start + jnp.arange(qb)
        ql = lax.dynamic_slice_in_dim(q_lat, start, qb, 1)
        qr = lax.dynamic_slice_in_dim(q_rope, start, qb, 1)
        s = (jnp.einsum('bqhc,bkc->bhqk', ql, ckv) + jnp.einsum('bqhr,bkr->bhqk', qr, kr)).astype(jnp.float32) * scale
        p, _ = masked_softmax(s, kpos[None, :] <= qpos[:, None])
        o_lat = jnp.einsum('bhqk,bkc->bqhc', p.astype(ckv.dtype), ckv)
        return jnp.einsum('bqhc,chd->bqhd', o_lat, w_uv).reshape(bs, qb, MLA_HEADS * V_HEAD)
    ys = sweep_queries(block_s, ts) @ w_out
    return yp, ys, lat_p, lat_s


def dil_split(x, w_in):
    b, t, _ = x.shape
    h = (x @ w_in).reshape(b, t, N_DIL, DIL_GROUP_IN)
    q = h[..., :D_MODEL].reshape(b, t, N_DIL, DIL_KV_HEADS, DIL_GROUP, HEAD_DIM)
    kv = h[..., D_MODEL:].reshape(b, t, N_DIL, 2, DIL_KV_HEADS, HEAD_DIM)
    return q, kv


def dil_attend(q, srcs, offsets, table):
    b, tq = q.shape[:2]
    qs = [q[:, :, g] for g in range(N_DIL)]
    scale = HEAD_DIM ** -0.5

    def block(start, qb):
        outs, lses = [], []
        for g, (w, dil) in enumerate(DIL_PATTERNS):
            qg = lax.dynamic_slice_in_dim(qs[g], start, qb, 1)
            dist = jnp.arange(w // dil + 1) * dil
            kidx = offsets[g] + start + jnp.arange(qb)[:, None] - dist[None, :]
            kvg = srcs[g][:, jnp.maximum(kidx, 0)]
            s = jnp.einsum('bqhgd,bqkhd->bhgqk', qg, kvg[:, :, :, 0]).astype(jnp.float32) * scale
            s = s + t5_bias_gqa(table, dist[None, :], DIL_KV_HEADS)
            p, lse = masked_softmax(s, kidx >= 0)
            outs.append(jnp.einsum('bhgqk,bqkhd->bqhgd', p.astype(kvg.dtype), kvg[:, :, :, 1]))
            lses.append(lse)
        wts = jnp.transpose(jax.nn.softmax(jnp.stack(lses), axis=0), (0, 1, 4, 2, 3))[..., None]
        o = jnp.sum(wts.astype(outs[0].dtype) * jnp.stack(outs), axis=0)
        return o.reshape(b, qb, N_HEADS * HEAD_DIM)
    return sweep_queries(block, tq)


def dil_layer(xp, xs, states, w_in, w_out, table):
    t = xp.shape[1]
    q, kv = dil_split(xp, w_in)
    yp = dil_attend(q, [kv[:, :, g] for g in range(N_DIL)], [0] * N_DIL, table) @ w_out
    st_p = [kv[:, t - min(w, t):, g] for g, (w, _) in enumerate(DIL_PATTERNS)]
    q, kv = dil_split(xs, w_in)
    srcs = [jnp.concatenate([states[g], kv[:, :, g]], axis=1) for g in range(N_DIL)]
    ys = dil_attend(q, srcs, [s.shape[1] for s in states], table) @ w_out
    st_s = [srcs[g][:, srcs[g].shape[1] - states[g].shape[1]:] for g in range(N_DIL)]
    return yp, ys, st_p, st_s


def sb_split(x, w_in):
    b, t, _ = x.shape
    h = x @ w_in
    q = h[..., :D_MODEL].reshape(b, t, SB_KV_HEADS, SB_GROUP, HEAD_DIM)
    kv = h[..., D_MODEL:].reshape(b, t, 2, SB_KV_HEADS, HEAD_DIM)
    return q, kv


def sb_attend(q, kv_all, q_pos0):
    b, tq = q.shape[:2]
    k, v = kv_all[:, :, 0], kv_all[:, :, 1]
    kpos = jnp.arange(kv_all.shape[1])
    scale = HEAD_DIM ** -0.5

    def block(start, qb):
        qpos = q_pos0 + start + jnp.arange(qb)
        z = jnp.einsum('bqhgd,bkhd->bhgqk', lax.dynamic_slice_in_dim(q, start, qb, 1), k).astype(jnp.float32) * scale
        valid = kpos[None, :] < qpos[:, None]
        log_1m = jnp.where(valid, jax.nn.log_sigmoid(-z), 0.0)
        tail = lax.cumsum(log_1m, axis=z.ndim - 1, reverse=True)
        tail = jnp.pad(tail[..., 1:], [(0, 0)] * (z.ndim - 1) + [(0, 1)])
        a = jnp.where(valid, jnp.exp(jax.nn.log_sigmoid(z) + tail), 0.0)
        o = jnp.einsum('bhgqk,bkhd->bqhgd', a.astype(v.dtype), v)
        return o.reshape(b, qb, N_HEADS * HEAD_DIM)
    return sweep_queries(block, tq)


def sb_layer(xp, xs, cache, page_table, w_in, w_out):
    q, kv_p = sb_split(xp, w_in)
    yp = sb_attend(q, kv_p, 0) @ w_out
    q, kv_s = sb_split(xs, w_in)
    kv_all = jnp.concatenate([gather_pages(cache, page_table), kv_s], axis=1)
    ys = sb_attend(q, kv_all, PAST_LEN) @ w_out
    return yp, ys, kv_p, kv_s


def setup_inputs(seed: int = 0) -> dict:
    key = jax.random.key(seed)
    keys = iter(jax.random.split(key, 64))
    f32 = jnp.float32

    def nrm(shape, scale=1.0):
        return jax.random.normal(next(keys), shape, f32) * scale

    n_of = [len(range(m, DEPTH, N_MIXERS)) for m in range(N_MIXERS)]
    n_pages = PAST_LEN // PAGE_SIZE
    n_pool = (5 * DEC_BATCH * n_pages) // 4
    page_table = jax.random.permutation(next(keys), n_pool)[:DEC_BATCH * n_pages].reshape(DEC_BATCH, n_pages).astype(jnp.int32)
    d = D_MODEL
    inputs = {
        'x_prompt': nrm((BATCH, SEQ, d)),
        'x_sample': nrm((DEC_BATCH, DEC_SEQ, d)),
        'cache_nsa_kv': nrm((n_of[0], n_pool, PAGE_SIZE, 4, NSA_KV_HEADS, HEAD_DIM)),
        'cache_nsa_win': nrm((n_of[0], DEC_BATCH, min(NSA_WINDOW, PAST_LEN), 2, NSA_KV_HEADS, HEAD_DIM)),
        'cache_mla': nrm((n_of[1], n_pool, PAGE_SIZE, KV_LORA + QK_ROPE)),
        'state_dil_w128': nrm((n_of[2], DEC_BATCH, min(DIL_PATTERNS[0][0], PAST_LEN), 2, DIL_KV_HEADS, HEAD_DIM)),
        'state_dil_w512': nrm((n_of[2], DEC_BATCH, min(DIL_PATTERNS[1][0], PAST_LEN), 2, DIL_KV_HEADS, HEAD_DIM)),
        'state_dil_w2048': nrm((n_of[2], DEC_BATCH, min(DIL_PATTERNS[2][0], PAST_LEN), 2, DIL_KV_HEADS, HEAD_DIM)),
        'cache_sb_kv': nrm((n_of[3], n_pool, PAGE_SIZE, 2, SB_KV_HEADS, HEAD_DIM)),
        'page_table': page_table,
        'p_prompt': nrm((DEPTH, BATCH, SEQ, P_DIM)),
        'p_sample': nrm((DEPTH, DEC_BATCH, DEC_SEQ, P_DIM)),
        'rel_bias': nrm((N_BUCKETS, N_HEADS), 0.5),
        'ln1_g': 1.0 + nrm((DEPTH, d), 0.02),
        'ln1_b': nrm((DEPTH, d), 0.02),
        'ln2_g': 1.0 + nrm((DEPTH, d), 0.02),
        'ln2_b': nrm((DEPTH, d), 0.02),
        'ffn_wg': nrm((DEPTH, d, D_FF), d ** -0.5),
        'ffn_wu': nrm((DEPTH, d, D_FF), d ** -0.5),
        'ffn_wd': nrm((DEPTH, D_FF, d), D_FF ** -0.5 * BETA),
        'ple_wg': nrm((DEPTH, d, d), d ** -0.5),
        'ple_wp': nrm((DEPTH, P_DIM, d), P_DIM ** -0.5),
        'nsa_w_in': nrm((n_of[0], d, NSA_IN), d ** -0.5),
        'nsa_cmp_pe': nrm((n_of[0], 2, CMP_BLOCK, HEAD_DIM), 0.1),
        'nsa_cmp_w1': nrm((n_of[0], 2, CMP_BLOCK * HEAD_DIM, CMP_HIDDEN), (CMP_BLOCK * HEAD_DIM) ** -0.5),
        'nsa_cmp_w2': nrm((n_of[0], 2, CMP_HIDDEN, HEAD_DIM), CMP_HIDDEN ** -0.5),
        'nsa_w_out': nrm((n_of[0], N_HEADS * HEAD_DIM, d), (N_HEADS * HEAD_DIM) ** -0.5 * BETA),
        'mla_w_dq': nrm((n_of[1], d, MLA_IN), d ** -0.5),
        'mla_q_norm': 1.0 + nrm((n_of[1], Q_LORA), 0.02),
        'mla_kv_norm': 1.0 + nrm((n_of[1], KV_LORA), 0.02),
        'mla_w_uq': nrm((n_of[1], Q_LORA, MLA_HEADS * (QK_NOPE + QK_ROPE)), Q_LORA ** -0.5),
        'mla_w_uk': nrm((n_of[1], KV_LORA, MLA_HEADS, QK_NOPE), KV_LORA ** -0.5),
        'mla_w_uv': nrm((n_of[1], KV_LORA, MLA_HEADS, V_HEAD), KV_LORA ** -0.5),
        'mla_w_out': nrm((n_of[1], MLA_HEADS * V_HEAD, d), (MLA_HEADS * V_HEAD) ** -0.5 * BETA),
        'dil_w_in': nrm((n_of[2], d, DIL_IN), d ** -0.5),
        'dil_w_out': nrm((n_of[2], N_HEADS * HEAD_DIM, d), (N_HEADS * HEAD_DIM) ** -0.5 * BETA),
        'sb_w_in': nrm((n_of[3], d, SB_IN), d ** -0.5),
        'sb_w_out': nrm((n_of[3], N_HEADS * HEAD_DIM, d), (N_HEADS * HEAD_DIM) ** -0.5 * BETA),
    }
    return inputs


def reference(x_prompt, x_sample, cache_nsa_kv, cache_nsa_win, cache_mla, state_dil_w128, state_dil_w512,
              state_dil_w2048, cache_sb_kv, page_table, p_prompt, p_sample, rel_bias, ln1_g, ln1_b, ln2_g, ln2_b,
              ffn_wg, ffn_wu, ffn_wd, ple_wg, ple_wp, nsa_w_in, nsa_cmp_pe, nsa_cmp_w1, nsa_cmp_w2, nsa_w_out,
              mla_w_dq, mla_q_norm, mla_kv_norm, mla_w_uq, mla_w_uk, mla_w_uv, mla_w_out, dil_w_in, dil_w_out,
              sb_w_in, sb_w_out):
    dil_states = (state_dil_w128, state_dil_w512, state_dil_w2048)
    nsa_kv_p, nsa_kv_s, nsa_win_p, nsa_win_s = [], [], [], []
    mla_p, mla_s = [], []
    dil_p = [[] for _ in range(N_DIL)]
    dil_s = [[] for _ in range(N_DIL)]
    sb_p, sb_s = [], []
    xp, xs = x_prompt, x_sample
    for i in range(DEPTH):
        kind, j = i % N_MIXERS, i // N_MIXERS
        if kind == 0:
            mp, ms, a, b_, c, e = nsa_layer(xp, xs, cache_nsa_kv[j], cache_nsa_win[j], page_table, nsa_w_in[j],
                                            nsa_cmp_pe[j], nsa_cmp_w1[j], nsa_cmp_w2[j], nsa_w_out[j], rel_bias)
            nsa_kv_p.append(a)
            nsa_kv_s.append(b_)
            nsa_win_p.append(c)
            nsa_win_s.append(e)
        elif kind == 1:
            mp, ms, a, b_ = mla_layer(xp, xs, cache_mla[j], page_table, mla_w_dq[j], mla_q_norm[j], mla_kv_norm[j],
                                      mla_w_uq[j], mla_w_uk[j], mla_w_uv[j], mla_w_out[j])
            mla_p.append(a)
            mla_s.append(b_)
        elif kind == 2:
            mp, ms, st_p, st_s = dil_layer(xp, xs, [s[j] for s in dil_states], dil_w_in[j], dil_w_out[j], rel_bias)
            for g in range(N_DIL):
                dil_p[g].append(st_p[g])
                dil_s[g].append(st_s[g])
        else:
            mp, ms, a, b_ = sb_layer(xp, xs, cache_sb_kv[j], page_table, sb_w_in[j], sb_w_out[j])
            sb_p.append(a)
            sb_s.append(b_)
        xp = residual_tail(xp, mp, p_prompt[i], ln1_g[i], ln1_b[i], ln2_g[i], ln2_b[i],
                           ffn_wg[i], ffn_wu[i], ffn_wd[i], ple_wg[i], ple_wp[i])
        xs = residual_tail(xs, ms, p_sample[i], ln1_g[i], ln1_b[i], ln2_g[i], ln2_b[i],
                           ffn_wg[i], ffn_wu[i], ffn_wd[i], ple_wg[i], ple_wp[i])
    return (xp, xs,
            jnp.stack(nsa_kv_p), jnp.stack(nsa_kv_s), jnp.stack(nsa_win_p), jnp.stack(nsa_win_s),
            jnp.stack(mla_p), jnp.stack(mla_s),
            jnp.stack(dil_p[0]), jnp.stack(dil_s[0]), jnp.stack(dil_p[1]), jnp.stack(dil_s[1]),
            jnp.stack(dil_p[2]), jnp.stack(dil_s[2]),
            jnp.stack(sb_p), jnp.stack(sb_s))
```

```python
import functools
import math

import numpy as np
import jax
import jax.numpy as jnp
from jax import lax
from jax.experimental import pallas as pl
from jax.experimental.pallas import tpu as pltpu

F32 = jnp.float32
BF16 = jnp.bfloat16

HEAD_DIM = 64
N_HEADS = 16
PAGE = 128
Q_TILE = 128
LN_EPS = 1e-5
NEG_INF = -1e30
DEPTH = 4
ALPHA = (2 * DEPTH) ** 0.25
N_BUCKETS = 32
MAX_DISTANCE = 2048
CMP_BLOCK = 32
CMP_STRIDE = 16
SLC_BLOCK = 64
N_SELECT = 16
NSA_WINDOW = 512
FORCE_BONUS = 1e4
DIL_PATTERNS = ((128, 1), (512, 4), (2048, 16))
ROPE_THETA = 10000.0
QK_ROPE = 32
KV_LORA = 256
Q_LORA = 256
SB_DEAD = -104.0
VMEM_LIMIT_BYTES = 56 * 1024 * 1024


def _cparams(*sem):
    return pltpu.CompilerParams(dimension_semantics=sem, vmem_limit_bytes=VMEM_LIMIT_BYTES)


def _dot(a, b):
    return jnp.dot(a, b, preferred_element_type=F32)


def _dot_nt(a, b):
    return lax.dot_general(a, b, (((1,), (1,)), ((), ())), preferred_element_type=F32)


def _const_spec(shape):
    nd = len(shape)
    return pl.BlockSpec(shape, lambda *_: (0,) * nd)


def _bucket_of_distance(n_dist):
    n = np.arange(n_dist, dtype=np.int64)
    max_exact = N_BUCKETS // 2
    ratio = np.maximum(n, max_exact).astype(np.float32) / np.float32(max_exact)
    log_ratio = np.log(ratio).astype(np.float32) / np.float32(math.log(MAX_DISTANCE / max_exact))
    large = np.minimum(max_exact + (log_ratio * np.float32(N_BUCKETS - max_exact)).astype(np.int32), N_BUCKETS - 1)
    return np.where(n < max_exact, n, large).astype(np.int32)


def _bias_by_distance(rel_bias, dists):
    d = np.asarray(dists)
    bucket = _bucket_of_distance(int(d.max()) + 1)[np.maximum(d, 0)]
    vals = jnp.moveaxis(rel_bias.astype(F32)[bucket], -1, 0)
    return jnp.where(jnp.asarray(d >= 0), vals, 0.0)


def _proj_body(x_ref, *refs, n_out):
    xb = x_ref[...].astype(BF16)
    for w_ref, o_ref in zip(refs[:n_out], refs[n_out:]):
        o_ref[...] = _dot(xb, w_ref[...]).astype(o_ref.dtype)


def _proj(x, ws, tm):
    m, k = x.shape
    n_out = len(ws)
    return pl.pallas_call(
        functools.partial(_proj_body, n_out=n_out),
        grid=(m // tm,),
        in_specs=[pl.BlockSpec((tm, k), lambda i: (i, 0))] + [_const_spec(w.shape) for w in ws],
        out_specs=[pl.BlockSpec((tm, w.shape[1]), lambda i: (i, 0)) for w in ws],
        out_shape=[jax.ShapeDtypeStruct((m, w.shape[1]), F32) for w in ws],
        compiler_params=_cparams("parallel"),
        name="proj",
    )(x, *ws)


def _row_tile(m):
    return 256 if m % 256 == 0 else m


def _layer_norm(x, g, b):
    mu = jnp.mean(x, axis=-1, keepdims=True)
    xc = x - mu
    var = jnp.mean(xc * xc, axis=-1, keepdims=True)
    return xc * lax.rsqrt(var + LN_EPS) * g + b


FF_CHUNK = 256


def _tail_body(*refs, n_mix):
    mix_refs = refs[:n_mix]
    (x_ref, p_ref, wo_ref, g1_ref, b1_ref, g2_ref, b2_ref, wg_ref, wu_ref, wd_ref,
     pwg_ref, pwp_ref, o_ref) = refs[n_mix:]
    if n_mix == 1:
        attn = mix_refs[0][...]
    else:
        n_g = n_mix // 2
        lses = [r[...] for r in mix_refs[n_g:]]
        m = functools.reduce(jnp.maximum, lses)
        es = [jnp.exp(l - m) for l in lses]
        den = functools.reduce(lambda a, b: a + b, es)
        ws = [e / den for e in es]
        cols = []
        for h in range(N_HEADS):
            sl = slice(h * HEAD_DIM, (h + 1) * HEAD_DIM)
            acc = None
            for g in range(n_g):
                term = ws[g][:, h:h + 1] * mix_refs[g][:, sl]
                acc = term if acc is None else acc + term
            cols.append(acc)
        attn = jnp.concatenate(cols, axis=1)
    x = x_ref[...]
    mix = _dot(attn.astype(BF16), wo_ref[...])
    h1 = _layer_norm(ALPHA * x + mix, g1_ref[...], b1_ref[...])
    h1b = h1.astype(BF16)
    d_ff = wg_ref.shape[1]
    acc = jnp.zeros(x.shape, F32)
    for c in range(d_ff // FF_CHUNK):
        sl = slice(c * FF_CHUNK, (c + 1) * FF_CHUNK)
        g = _dot(h1b, wg_ref[:, sl])
        u = _dot(h1b, wu_ref[:, sl])
        acc = acc + _dot((g * jax.nn.sigmoid(g) * u).astype(BF16), wd_ref[sl, :])
    h2 = _layer_norm(ALPHA * h1 + acc, g2_ref[...], b2_ref[...])
    gate = jax.nn.sigmoid(_dot(h2.astype(BF16), pwg_ref[...]))
    o_ref[...] = h2 + gate * _dot(p_ref[...].astype(BF16), pwp_ref[...])


def _tail(mix_list, x, p, wo, g1, b1, g2, b2, wg, wu, wd, pwg, pwp):
    m, d = x.shape
    tm = _row_tile(m)
    consts = [wo, g1, b1, g2, b2, wg, wu, wd, pwg, pwp]
    row = lambda a: pl.BlockSpec((tm, a.shape[1]), lambda i: (i, 0))
    single = lambda a: pl.BlockSpec(a.shape, lambda i: (0, 0), pipeline_mode=pl.Buffered(1))
    return pl.pallas_call(
        functools.partial(_tail_body, n_mix=len(mix_list)),
        grid=(m // tm,),
        in_specs=[row(a) for a in mix_list] + [row(x), row(p)] + [single(a) for a in consts],
        out_specs=pl.BlockSpec((tm, d), lambda i: (i, 0)),
        out_shape=jax.ShapeDtypeStruct((m, d), F32),
        compiler_params=_cparams("parallel"),
        name="tail",
    )(*mix_list, x, p, *consts)


def _page_copies(pool_ref, buf_ref, sem_ref, pt_ref, b, slot, n_pages, start):
    def body(p, carry):
        if len(buf_ref.shape) == 4:
            dst = buf_ref.at[slot, p]
        else:
            rows = pool_ref.shape[1]
            dst = buf_ref.at[slot, pl.ds(pl.multiple_of(p * rows, rows), rows)]
        cp = pltpu.make_async_copy(pool_ref.at[pt_ref[b, p]], dst, sem_ref.at[slot])
        if start:
            cp.start()
        else:
            cp.wait()
        return carry
    lax.fori_loop(0, n_pages, body, 0)


def _paged_prefetch(pool_ref, buf_ref, sem_ref, pt_ref, n_pages):
    b = pl.program_id(0)
    nb = pl.num_programs(0)
    slot = b % 2

    @pl.when(b == 0)
    def _():
        _page_copies(pool_ref, buf_ref, sem_ref, pt_ref, 0, 0, n_pages, True)

    @pl.when(b + 1 < nb)
    def _():
        _page_copies(pool_ref, buf_ref, sem_ref, pt_ref, b + 1, 1 - slot, n_pages, True)

    _page_copies(pool_ref, buf_ref, sem_ref, pt_ref, b, slot, n_pages, False)
    return slot


def _sb_terms(z, valid):
    t = jnp.log1p(jnp.exp(-jnp.abs(z)))
    l1m = -jnp.maximum(z, 0.0) - t
    if valid is not None:
        l1m = jnp.where(valid, l1m, 0.0)
    ls = jnp.minimum(z, 0.0) - t
    return l1m, ls


def _strict_upper_sum_matrix(n):
    j = lax.broadcasted_iota(jnp.int32, (2 * n, n), 0) & (n - 1)
    s = lax.broadcasted_iota(jnp.int32, (2 * n, n), 1)
    return (j > s).astype(BF16)


def _tail_sums(l1m, uu):
    hi = l1m.astype(BF16)
    lo = (l1m - hi.astype(F32)).astype(BF16)
    return _dot(jnp.concatenate([hi, lo], axis=1), uu)


def _sb_prompt_body(q_ref, kv_ref, o_ref, r_ref, acc_ref, *, n_kv, grp):
    qi = pl.program_id(1)
    tq = q_ref.shape[1]
    rows = grp * tq
    uu = _strict_upper_sum_matrix(Q_TILE)
    qpos = qi * tq + (lax.broadcasted_iota(jnp.int32, (rows, Q_TILE), 0) & (tq - 1))
    lane = lax.broadcasted_iota(jnp.int32, (rows, Q_TILE), 1)
    for g in range(n_kv):
        qg = jnp.concatenate(
            [q_ref[0, :, (g * grp + u) * HEAD_DIM:(g * grp + u + 1) * HEAD_DIM] for u in range(grp)], axis=0)
        qg = (qg * (HEAD_DIM ** -0.5)).astype(BF16)
        r_ref[...] = jnp.zeros(r_ref.shape, F32)
        acc_ref[...] = jnp.zeros(acc_ref.shape, F32)

        def cond(c):
            return jnp.logical_and(c[0] >= 0, c[1] > SB_DEAD)

        def body(c):
            kj = c[0]
            off = pl.multiple_of(kj * Q_TILE, Q_TILE)
            k = kv_ref[0, pl.ds(off, Q_TILE), g * HEAD_DIM:(g + 1) * HEAD_DIM].astype(BF16)
            v = kv_ref[0, pl.ds(off, Q_TILE), (n_kv + g) * HEAD_DIM:(n_kv + g + 1) * HEAD_DIM].astype(BF16)
            z = _dot_nt(qg, k)
            valid = (kj * Q_TILE + lane) < qpos
            l1m, ls = _sb_terms(z, valid)
            r = r_ref[...]
            a = jnp.where(valid, jnp.exp(ls + _tail_sums(l1m, uu) + r), 0.0)
            acc_ref[...] += _dot(a.astype(BF16), v)
            rn = r + jnp.sum(l1m, axis=1, keepdims=True)
            r_ref[...] = rn
            return kj - 1, jnp.max(rn)

        lax.while_loop(cond, body, (qi, jnp.float32(0.0)))
        for u in range(grp):
            h = g * grp + u
            o_ref[0, :, h * HEAD_DIM:(h + 1) * HEAD_DIM] = acc_ref[u * tq:(u + 1) * tq, :]


def _sb_prompt(q, kv, n_kv):
    b, t, d = q.shape
    grp = N_HEADS // n_kv
    return pl.pallas_call(
        functools.partial(_sb_prompt_body, n_kv=n_kv, grp=grp),
        grid=(b, t // Q_TILE),
        in_specs=[pl.BlockSpec((1, Q_TILE, d), lambda i, j: (i, j, 0)),
                  pl.BlockSpec((1, t, kv.shape[2]), lambda i, j: (i, 0, 0))],
        out_specs=pl.BlockSpec((1, Q_TILE, d), lambda i, j: (i, j, 0)),
        out_shape=jax.ShapeDtypeStruct((b, t, d), F32),
        scratch_shapes=[pltpu.VMEM((grp * Q_TILE, 1), F32), pltpu.VMEM((grp * Q_TILE, HEAD_DIM), F32)],
        compiler_params=_cparams("parallel", "arbitrary"),
        name="sb_prompt",
    )(q, kv)


def _head_spread(n_kv):
    d = lax.broadcasted_iota(jnp.int32, (HEAD_DIM, n_kv * HEAD_DIM), 0)
    c = lax.broadcasted_iota(jnp.int32, (HEAD_DIM, n_kv * HEAD_DIM), 1)
    return ((c & (HEAD_DIM - 1)) == d).astype(BF16)


def _block_diag_q(q, n_kv, scale):
    grp = N_HEADS // n_kv
    width = n_kv * HEAD_DIM
    spread = _dot((q * scale).astype(BF16), _head_spread(n_kv))
    row = lax.broadcasted_iota(jnp.int32, (N_HEADS, width), 0)
    col = lax.broadcasted_iota(jnp.int32, (N_HEADS, width), 1)
    own = (col >> 6) == (row >> int(math.log2(grp)))
    return jnp.where(own, spread, 0.0).astype(BF16)


def _block_diag_pick(o_full, n_kv):
    grp = N_HEADS // n_kv
    row = lax.broadcasted_iota(jnp.int32, (N_HEADS, HEAD_DIM), 0)
    out = jnp.zeros((N_HEADS, HEAD_DIM), F32)
    for g in range(n_kv):
        out = out + jnp.where((row >> int(math.log2(grp))) == g, o_full[:, g * HEAD_DIM:(g + 1) * HEAD_DIM], 0.0)
    return out


def _sb_sample_body(pt_ref, q_ref, pool_ref, o_ref, buf_ref, sem_ref, *, n_kv, n_pages):
    slot = _paged_prefetch(pool_ref, buf_ref, sem_ref, pt_ref, n_pages)
    width = n_kv * HEAD_DIM
    qbd = _block_diag_q(q_ref[0], n_kv, HEAD_DIM ** -0.5)
    uu = _strict_upper_sum_matrix(PAGE)

    def cond(c):
        return jnp.logical_and(c[0] >= 0, c[1] > SB_DEAD)

    def body(c):
        p, _, r, acc = c
        k = buf_ref[slot, p, :, 0:width].astype(BF16)
        v = buf_ref[slot, p, :, width:2 * width].astype(BF16)
        l1m, ls = _sb_terms(_dot_nt(qbd, k), None)
        a = jnp.exp(ls + _tail_sums(l1m, uu) + r)
        acc = acc + _dot(a.astype(BF16), v)
        rn = r + jnp.sum(l1m, axis=1, keepdims=True)
        return p - 1, jnp.max(rn), rn, acc

    init = (jnp.int32(n_pages - 1), jnp.float32(0.0), jnp.zeros((N_HEADS, 1), F32), jnp.zeros((N_HEADS, width), F32))
    acc = lax.while_loop(cond, body, init)[3]
    o_ref[0] = _block_diag_pick(acc, n_kv)


def _sb_sample(q, pool, page_table, n_kv):
    nb = q.shape[0]
    n_pages = page_table.shape[1]
    feat = pool.shape[2]
    grid_spec = pltpu.PrefetchScalarGridSpec(
        num_scalar_prefetch=1,
        grid=(nb,),
        in_specs=[pl.BlockSpec((1, N_HEADS, HEAD_DIM), lambda i, pt: (i, 0, 0)),
                  pl.BlockSpec(memory_space=pl.ANY)],
        out_specs=pl.BlockSpec((1, N_HEADS, HEAD_DIM), lambda i, pt: (i, 0, 0)),
        scratch_shapes=[pltpu.VMEM((2, n_pages, PAGE, feat), F32), pltpu.SemaphoreType.DMA((2,))],
    )
    return pl.pallas_call(
        functools.partial(_sb_sample_body, n_kv=n_kv, n_pages=n_pages),
        grid_spec=grid_spec,
        out_shape=jax.ShapeDtypeStruct((nb, N_HEADS, HEAD_DIM), F32),
        compiler_params=_cparams("arbitrary"),
        name="sb_sample",
    )(page_table, q, pool)


def _sb_layer(xp, xs, cache, page_table, w_in):
    b, t, d = xp.shape
    n_kv = (w_in.shape[1] - d) // (2 * HEAD_DIM)
    ws = [w_in[:, :d].astype(BF16), w_in[:, d:].astype(BF16)]
    qp, kvp = _proj(xp.reshape(b * t, d), ws, _row_tile(b * t))
    attn_p = _sb_prompt(qp.reshape(b, t, d), kvp.reshape(b, t, -1), n_kv)
    nb = xs.shape[0]
    qs, kvs = _proj(xs.reshape(nb, d), ws, _row_tile(nb))
    pool = cache.reshape(cache.shape[0], PAGE, 2 * n_kv * HEAD_DIM)
    attn_s = _sb_sample(qs.reshape(nb, N_HEADS, HEAD_DIM), pool, page_table, n_kv)
    return (attn_p.reshape(b * t, d), attn_s.reshape(nb, d),
            kvp.reshape(b, t, 2, n_kv, HEAD_DIM), kvs.reshape(nb, 1, 2, n_kv, HEAD_DIM))


def _toeplitz(base_row, rows, shift):
    return pltpu.roll(jnp.broadcast_to(base_row, (rows, base_row.shape[1])), shift, 1, stride=1, stride_axis=0)


def _dil_prompt_body(q_ref, kvc_ref, kvp_ref, base_ref, o_ref, lse_ref, *, n_kv, grp):
    mi = pl.program_id(2)
    tq = Q_TILE
    i_idx = lax.broadcasted_iota(jnp.int32, (tq, 2 * tq), 0)
    j_idx = lax.broadcasted_iota(jnp.int32, (tq, 2 * tq), 1)
    steps = i_idx - j_idx + tq
    valid = (steps >= 0) & (steps <= tq) & ((mi > 0) | (j_idx >= tq))
    lane = lax.broadcasted_iota(jnp.int32, (tq, 128), 1)
    lse_tile = jnp.zeros((tq, 128), F32)
    width = n_kv * HEAD_DIM
    for g in range(n_kv):
        ksl = slice(g * HEAD_DIM, (g + 1) * HEAD_DIM)
        vsl = slice(width + g * HEAD_DIM, width + (g + 1) * HEAD_DIM)
        k = jnp.concatenate([kvp_ref[0, :, ksl], kvc_ref[0, :, ksl]], axis=0).astype(BF16)
        v = jnp.concatenate([kvp_ref[0, :, vsl], kvc_ref[0, :, vsl]], axis=0).astype(BF16)
        for u in range(grp):
            h = g * grp + u
            hsl = slice(h * HEAD_DIM, (h + 1) * HEAD_DIM)
            qh = (q_ref[0, :, hsl] * (HEAD_DIM ** -0.5)).astype(BF16)
            s = _dot_nt(qh, k) + _toeplitz(base_ref[h:h + 1, :], tq, 0)
            s = jnp.where(valid, s, NEG_INF)
            m = jnp.max(s, axis=1, keepdims=True)
            e = jnp.where(valid, jnp.exp(s - m), 0.0)
            den = jnp.maximum(jnp.sum(e, axis=1, keepdims=True), 1e-30)
            o_ref[0, :, hsl] = _dot((e / den).astype(BF16), v)
            lse_tile = jnp.where(lane == h, m + jnp.log(den), lse_tile)
    lse_ref[0] = lse_tile


def _dil_prompt(q, kv, base, dil, n_kv):
    b, tm, _ = q.shape
    d = N_HEADS * HEAD_DIM
    kvw = 2 * n_kv * HEAD_DIM
    nm = tm // Q_TILE
    return pl.pallas_call(
        functools.partial(_dil_prompt_body, n_kv=n_kv, grp=N_HEADS // n_kv),
        grid=(b, dil, nm),
        in_specs=[pl.BlockSpec((1, Q_TILE, d), lambda i, r, m: (i, m, r)),
                  pl.BlockSpec((1, Q_TILE, kvw), lambda i, r, m: (i, m, r)),
                  pl.BlockSpec((1, Q_TILE, kvw), lambda i, r, m: (i, jnp.maximum(m - 1, 0), r)),
                  _const_spec(base.shape)],
        out_specs=[pl.BlockSpec((1, Q_TILE, d), lambda i, r, m: (i, m, r)),
                   pl.BlockSpec((1, Q_TILE, 128), lambda i, r, m: (i, m, r))],
        out_shape=[jax.ShapeDtypeStruct((b, tm, dil * d), F32), jax.ShapeDtypeStruct((b, tm, dil * 128), F32)],
        compiler_params=_cparams("parallel", "parallel", "arbitrary"),
        name="dil_prompt",
    )(q, kv, kv, base)


def _dil_sample_body(q_ref, kvn_ref, base_ref, st0_ref, st1_ref, st2_ref, o_ref, so0_ref, so1_ref, so2_ref, *, n_kv):
    width = n_kv * HEAD_DIM
    chunks = 2 * width // 128
    outs, lses = [], []
    for g, (st, so, (w, dil)) in enumerate(zip((st0_ref, st1_ref, st2_ref), (so0_ref, so1_ref, so2_ref), DIL_PATTERNS)):
        n_old = w // dil
        old = [st[0, pl.ds(c, n_old, stride=chunks * dil), :] for c in range(chunks)]
        new = [kvn_ref[0, g * chunks + c:g * chunks + c + 1, :] for c in range(chunks)]
        ks = jnp.concatenate(old[:chunks // 2], axis=1).astype(BF16)
        vs = jnp.concatenate(old[chunks // 2:], axis=1).astype(BF16)
        kn = jnp.concatenate(new[:chunks // 2], axis=1).astype(BF16).astype(F32)
        vn = jnp.concatenate(new[chunks // 2:], axis=1).astype(BF16).astype(F32)
        qbd = _block_diag_q(q_ref[0, g], n_kv, HEAD_DIM ** -0.5)
        s_old = _dot_nt(qbd, ks) + base_ref[g, :, 0:n_old]
        s_new = jnp.sum(qbd.astype(F32) * kn, axis=1, keepdims=True) + base_ref[g, :, n_old:n_old + 1]
        m = jnp.maximum(jnp.max(s_old, axis=1, keepdims=True), s_new)
        e_old = jnp.exp(s_old - m)
        e_new = jnp.exp(s_new - m)
        den = jnp.sum(e_old, axis=1, keepdims=True) + e_new
        o_full = _dot((e_old / den).astype(BF16), vs) + (e_new / den).astype(BF16).astype(F32) * vn
        outs.append(_block_diag_pick(o_full, n_kv))
        lses.append(m + jnp.log(den))
        so[0, 0:(w - 1) * chunks, :] = st[0, chunks:w * chunks, :]
        so[0, (w - 1) * chunks:w * chunks, :] = kvn_ref[0, g * chunks:(g + 1) * chunks, :]
    m = functools.reduce(jnp.maximum, lses)
    es = [jnp.exp(l - m) for l in lses]
    den = functools.reduce(lambda a, b: a + b, es)
    o = None
    for e, og in zip(es, outs):
        o = (e / den) * og if o is None else o + (e / den) * og
    o_ref[0] = o


def _dil_sample(q, kvn, base, states, n_kv):
    nb = q.shape[0]
    st_specs = [pl.BlockSpec((1, s.shape[1], 128), lambda i: (i, 0, 0)) for s in states]
    outs = pl.pallas_call(
        functools.partial(_dil_sample_body, n_kv=n_kv),
        grid=(nb,),
        in_specs=[pl.BlockSpec((1,) + q.shape[1:], lambda i: (i, 0, 0, 0)),
                  pl.BlockSpec((1,) + kvn.shape[1:], lambda i: (i, 0, 0)),
                  _const_spec(base.shape)] + st_specs,
        out_specs=[pl.BlockSpec((1, N_HEADS, HEAD_DIM), lambda i: (i, 0, 0))] + st_specs,
        out_shape=[jax.ShapeDtypeStruct((nb, N_HEADS, HEAD_DIM), F32)]
        + [jax.ShapeDtypeStruct(s.shape, F32) for s in states],
        compiler_params=_cparams("parallel"),
        name="dil_sample",
    )(q, kvn, base, *states)
    return outs[0], outs[1:]


def _dil_layer(xp, xs, states, w_in, rel_bias):
    b, t, d = xp.shape
    n_g = len(DIL_PATTERNS)
    w3 = w_in.reshape(d, n_g, -1)
    kvw = w3.shape[2] - d
    n_kv = kvw // (2 * HEAD_DIM)
    ws = []
    for g in range(n_g):
        ws += [w3[:, g, :d].astype(BF16), w3[:, g, d:].astype(BF16)]
    outs_p = _proj(xp.reshape(b * t, d), ws, _row_tile(b * t))
    lane = np.arange(2 * Q_TILE)
    mix, st_p = [], []
    for g, (w, dil) in enumerate(DIL_PATTERNS):
        assert w // dil == Q_TILE and t % (dil * Q_TILE) == 0
        base = _bias_by_distance(rel_bias, np.where(lane <= Q_TILE, (Q_TILE - lane) * dil, -1))
        o_g, lse_g = _dil_prompt(outs_p[2 * g].reshape(b, t // dil, dil * d),
                                 outs_p[2 * g + 1].reshape(b, t // dil, dil * kvw), base, dil, n_kv)
        mix.append((o_g.reshape(b * t, d), lse_g.reshape(b * t, 128)))
        st_p.append(outs_p[2 * g + 1].reshape(b, t, 2, n_kv, HEAD_DIM)[:, t - min(w, t):])
    mix_p = [m[0] for m in mix] + [m[1] for m in mix]
    nb = xs.shape[0]
    outs_s = _proj(xs.reshape(nb, d), ws, _row_tile(nb))
    q_s = jnp.stack([outs_s[2 * g].reshape(nb, N_HEADS, HEAD_DIM) for g in range(n_g)], axis=1)
    kvn = jnp.stack([outs_s[2 * g + 1] for g in range(n_g)], axis=1).reshape(nb, -1, 128)
    base_s = jnp.stack([
        _bias_by_distance(rel_bias, np.where(lane < Q_TILE, w - lane * dil, np.where(lane == Q_TILE, 0, -1)))
        for (w, dil) in DIL_PATTERNS])
    attn_s, st_s = _dil_sample(q_s, kvn, base_s, [s.reshape(nb, -1, 128) for s in states], n_kv)
    st_s = [s.reshape(nb, -1, 2, n_kv, HEAD_DIM) for s in st_s]
    return mix_p, attn_s.reshape(nb, d), st_p, st_s


HEAD_PAD = 128
ROPE_HALF = QK_ROPE // 2
MLA_SCALE = (HEAD_DIM + QK_ROPE) ** -0.5


def _rms(x, g):
    return x * lax.rsqrt(jnp.mean(x * x, axis=-1, keepdims=True) + LN_EPS) * g


def _mla_project_body(x_ref, c_ref, s_ref, wdq_ref, qn_ref, kvn_ref, wq_ref, wqs_ref, wk_ref, wv_ref,
                      q_ref, k_ref, v_ref, lat_ref):
    h = _dot(x_ref[...].astype(BF16), wdq_ref[...])
    cq = _rms(h[:, 0:Q_LORA], qn_ref[...]).astype(BF16)
    ckv = _rms(h[:, Q_LORA:Q_LORA + KV_LORA], kvn_ref[...])
    ckvb = ckv.astype(BF16)
    cos, sin = c_ref[...], s_ref[...]
    base = Q_LORA + KV_LORA
    kr = h[:, base:base + HEAD_PAD] * cos + h[:, base + HEAD_PAD:base + 2 * HEAD_PAD] * sin
    cos_all = jnp.concatenate([cos] * N_HEADS, axis=1)
    sin_all = jnp.concatenate([sin] * N_HEADS, axis=1)
    q_ref[...] = (_dot(cq, wq_ref[...]) * cos_all + _dot(cq, wqs_ref[...]) * sin_all).astype(BF16)
    k_ref[...] = (_dot(ckvb, wk_ref[...]) + jnp.concatenate([kr] * N_HEADS, axis=1)).astype(BF16)
    v_ref[...] = _dot(ckvb, wv_ref[...]).astype(BF16)
    lat_ref[:, 0:KV_LORA] = ckv
    lat_ref[:, KV_LORA:KV_LORA + QK_ROPE] = kr[:, HEAD_DIM:HEAD_DIM + QK_ROPE]


def _mla_project(x, cos, sin, consts, pos_blocks):
    m, d = x.shape
    tm = _row_tile(m)
    wide = N_HEADS * HEAD_PAD
    rope_spec = pl.BlockSpec((tm, HEAD_PAD), lambda i: (i % pos_blocks, 0))
    return pl.pallas_call(
        _mla_project_body,
        grid=(m // tm,),
        in_specs=[pl.BlockSpec((tm, d), lambda i: (i, 0)), rope_spec, rope_spec] + [_const_spec(c.shape) for c in consts],
        out_specs=[pl.BlockSpec((tm, wide), lambda i: (i, 0)), pl.BlockSpec((tm, wide), lambda i: (i, 0)),
                   pl.BlockSpec((tm, d), lambda i: (i, 0)), pl.BlockSpec((tm, KV_LORA + QK_ROPE), lambda i: (i, 0))],
        out_shape=[jax.ShapeDtypeStruct((m, wide), BF16), jax.ShapeDtypeStruct((m, wide), BF16),
                   jax.ShapeDtypeStruct((m, d), BF16), jax.ShapeDtypeStruct((m, KV_LORA + QK_ROPE), F32)],
        compiler_params=_cparams("parallel"),
        name="mla_project",
    )(x, cos, sin, *consts)


MLA_TK = 256


def _mla_prompt_body(q_ref, k_ref, v_ref, o_ref):
    qi = pl.program_id(2)
    tq = q_ref.shape[1]
    qpos = qi * tq + lax.broadcasted_iota(jnp.int32, (tq, MLA_TK), 0)
    lane = lax.broadcasted_iota(jnp.int32, (tq, MLA_TK), 1)
    n_tiles = (qi * tq + tq + MLA_TK - 1) // MLA_TK
    for hh in range(2):
        qh = q_ref[0, :, hh * HEAD_PAD:(hh + 1) * HEAD_PAD]

        def body(kj, c):
            m, l, acc = c
            off = pl.multiple_of(kj * MLA_TK, MLA_TK)
            k = k_ref[0, pl.ds(off, MLA_TK), hh * HEAD_PAD:(hh + 1) * HEAD_PAD]
            v = v_ref[0, pl.ds(off, MLA_TK), hh * HEAD_DIM:(hh + 1) * HEAD_DIM]
            valid = (kj * MLA_TK + lane) <= qpos
            s = jnp.where(valid, _dot_nt(qh, k) * MLA_SCALE, NEG_INF)
            mn = jnp.maximum(m, jnp.max(s, axis=1, keepdims=True))
            p = jnp.where(valid, jnp.exp(s - mn), 0.0)
            alpha = jnp.exp(m - mn)
            return mn, alpha * l + jnp.sum(p, axis=1, keepdims=True), alpha * acc + _dot(p.astype(BF16), v)

        init = (jnp.full((tq, 1), NEG_INF, F32), jnp.zeros((tq, 1), F32), jnp.zeros((tq, HEAD_DIM), F32))
        m, l, acc = lax.fori_loop(0, n_tiles, body, init)
        o_ref[0, :, hh * HEAD_DIM:(hh + 1) * HEAD_DIM] = acc / jnp.maximum(l, 1e-30)


def _mla_prompt(q, k, v):
    b, t, _ = q.shape
    tq = Q_TILE
    return pl.pallas_call(
        _mla_prompt_body,
        grid=(b, N_HEADS // 2, t // tq),
        in_specs=[pl.BlockSpec((1, tq, 2 * HEAD_PAD), lambda i, h, j: (i, j, h)),
                  pl.BlockSpec((1, t, 2 * HEAD_PAD), lambda i, h, j: (i, 0, h)),
                  pl.BlockSpec((1, t, 2 * HEAD_DIM), lambda i, h, j: (i, 0, h))],
        out_specs=pl.BlockSpec((1, tq, 2 * HEAD_DIM), lambda i, h, j: (i, j, h)),
        out_shape=jax.ShapeDtypeStruct((b, t, N_HEADS * HEAD_DIM), F32),
        compiler_params=_cparams("parallel", "parallel", "arbitrary"),
        name="mla_prompt",
    )(q, k, v)


def _mla_absorb_body(q_ref, wuk_ref, o_ref):
    for h in range(N_HEADS):
        qn = q_ref[:, h * HEAD_PAD:h * HEAD_PAD + HEAD_DIM]
        o_ref[:, h * KV_LORA:(h + 1) * KV_LORA] = _dot_nt(qn, wuk_ref[:, h * HEAD_DIM:(h + 1) * HEAD_DIM])


def _mla_unabsorb_body(o_ref, wuv_ref, y_ref):
    for h in range(N_HEADS):
        y_ref[:, h * HEAD_DIM:(h + 1) * HEAD_DIM] = _dot(
            o_ref[:, h * KV_LORA:(h + 1) * KV_LORA].astype(BF16), wuv_ref[:, h * HEAD_DIM:(h + 1) * HEAD_DIM])


def _whole_call(body, out_shape, name, *args):
    return pl.pallas_call(
        body, grid=(1,),
        in_specs=[_const_spec(a.shape) for a in args],
        out_specs=_const_spec(out_shape.shape),
        out_shape=out_shape, compiler_params=_cparams("arbitrary"), name=name)(*args)


MLA_CHUNK_PAGES = 4


def _mla_sample_body(pt_ref, ql_ref, q_ref, new_ref, pool_ref, o_ref, buf_ref, sem_ref, *, n_pages):
    slot = _paged_prefetch(pool_ref, buf_ref, sem_ref, pt_ref, n_pages)
    ql = ql_ref[0].astype(BF16)
    qr = q_ref[0, :, HEAD_DIM:HEAD_DIM + QK_ROPE]
    new_c = new_ref[0, :, 0:KV_LORA].astype(BF16)
    new_r = new_ref[0, :, KV_LORA:KV_LORA + QK_ROPE].astype(BF16)
    s_new = (jnp.sum(ql.astype(F32) * new_c.astype(F32), axis=1, keepdims=True)
             + jnp.sum(qr.astype(F32) * new_r.astype(F32), axis=1, keepdims=True)) * MLA_SCALE
    rows = MLA_CHUNK_PAGES * PAGE

    def body(c, carry):
        m, l, acc = carry
        blk = buf_ref[slot, pl.ds(c * MLA_CHUNK_PAGES, MLA_CHUNK_PAGES)]
        ckv = blk[:, :, 0:KV_LORA].reshape(rows, KV_LORA).astype(BF16)
        kr = blk[:, :, KV_LORA:KV_LORA + QK_ROPE].reshape(rows, QK_ROPE).astype(BF16)
        s = (_dot_nt(ql, ckv) + _dot_nt(qr, kr)) * MLA_SCALE
        mn = jnp.maximum(m, jnp.max(s, axis=1, keepdims=True))
        p = jnp.exp(s - mn)
        alpha = jnp.exp(m - mn)
        return mn, alpha * l + jnp.sum(p, axis=1, keepdims=True), alpha * acc + _dot(p.astype(BF16), ckv)

    init = (s_new, jnp.ones((N_HEADS, 1), F32), jnp.broadcast_to(new_c.astype(F32), (N_HEADS, KV_LORA)))
    m, l, acc = lax.fori_loop(0, n_pages // MLA_CHUNK_PAGES, body, init)
    o_ref[0] = acc / l


def _mla_sample(ql, q, lat_new, pool, page_table):
    nb = ql.shape[0]
    n_pages = page_table.shape[1]
    assert n_pages % MLA_CHUNK_PAGES == 0
    feat = pool.shape[2]
    grid_spec = pltpu.PrefetchScalarGridSpec(
        num_scalar_prefetch=1,
        grid=(nb,),
        in_specs=[pl.BlockSpec((1, N_HEADS, KV_LORA), lambda i, pt: (i, 0, 0)),
                  pl.BlockSpec((1, N_HEADS, HEAD_PAD), lambda i, pt: (i, 0, 0)),
                  pl.BlockSpec((1, 1, feat), lambda i, pt: (i, 0, 0)),
                  pl.BlockSpec(memory_space=pl.ANY)],
        out_specs=pl.BlockSpec((1, N_HEADS, KV_LORA), lambda i, pt: (i, 0, 0)),
        scratch_shapes=[pltpu.VMEM((2, n_pages, PAGE, feat), F32), pltpu.SemaphoreType.DMA((2,))],
    )
    return pl.pallas_call(
        functools.partial(_mla_sample_body, n_pages=n_pages),
        grid_spec=grid_spec,
        out_shape=jax.ShapeDtypeStruct((nb, N_HEADS, KV_LORA), F32),
        compiler_params=_cparams("arbitrary"),
        name="mla_sample",
    )(page_table, ql, q, lat_new, pool)


def _pad_heads(w, parts):
    out = jnp.zeros((w.shape[0], N_HEADS, HEAD_PAD), w.dtype)
    for src, size, dst in parts:
        out = out.at[:, :, dst:dst + size].set(w[:, :, src:src + size])
    return out.reshape(w.shape[0], N_HEADS * HEAD_PAD)


def _mla_layer(xp, xs, cache, page_table, w_dq, q_norm, kv_norm, w_uq, w_uk, w_uv, past_len):
    b, t, d = xp.shape
    nb = xs.shape[0]
    r0, r1 = HEAD_DIM, HEAD_DIM + ROPE_HALF
    keep = [(0, HEAD_DIM, 0), (HEAD_DIM, ROPE_HALF, r0), (r1, ROPE_HALF, r1)]
    swap = [(r1, ROPE_HALF, r0), (HEAD_DIM, ROPE_HALF, r1)]
    base = Q_LORA + KV_LORA
    x1w, x2w = w_dq[:, base:base + ROPE_HALF], w_dq[:, base + ROPE_HALF:base + QK_ROPE]
    z_lo, z_hi = jnp.zeros((d, HEAD_DIM), F32), jnp.zeros((d, HEAD_PAD - HEAD_DIM - QK_ROPE), F32)
    kr_keep = jnp.concatenate([z_lo, x1w, x2w, z_hi], axis=1)
    kr_swap = jnp.concatenate([z_lo, x2w, x1w, z_hi], axis=1)
    wdq = jnp.concatenate([w_dq[:, :base], kr_keep, kr_swap], axis=1).astype(BF16)
    wq3 = w_uq.reshape(Q_LORA, N_HEADS, HEAD_DIM + QK_ROPE)
    wq = _pad_heads(wq3, keep).astype(BF16)
    wqs = _pad_heads(wq3, swap).astype(BF16)
    wk = _pad_heads(w_uk, [(0, HEAD_DIM, 0)]).astype(BF16)
    wv = w_uv.reshape(KV_LORA, N_HEADS * HEAD_DIM).astype(BF16)
    consts = [wdq, q_norm.reshape(1, -1), kv_norm.reshape(1, -1), wq, wqs, wk, wv]

    def rope_tables(pos):
        inv = ROPE_THETA ** (-jnp.arange(ROPE_HALF, dtype=F32) / ROPE_HALF)
        ang = pos.astype(F32)[:, None] * inv[None, :]
        cos, sin = jnp.cos(ang), jnp.sin(ang)
        n = pos.shape[0]
        c = jnp.concatenate([jnp.ones((n, HEAD_DIM), F32), cos, cos, jnp.zeros((n, HEAD_PAD - r1 - ROPE_HALF), F32)], axis=1)
        s = jnp.concatenate([jnp.zeros((n, HEAD_DIM), F32), -sin, sin, jnp.zeros((n, HEAD_PAD - r1 - ROPE_HALF), F32)], axis=1)
        return c, s

    cos_p, sin_p = rope_tables(jnp.arange(t))
    q, k, v, lat_p = _mla_project(xp.reshape(b * t, d), cos_p, sin_p, consts, t // _row_tile(b * t))
    wide = N_HEADS * HEAD_PAD
    attn_p = _mla_prompt(q.reshape(b, t, wide), k.reshape(b, t, wide), v.reshape(b, t, d))
    cos_s, sin_s = rope_tables(jnp.full((nb,), past_len, jnp.int32))
    qs, _, _, lat_s = _mla_project(xs.reshape(nb, d), cos_s, sin_s, consts, 1)
    wuk2 = w_uk.reshape(KV_LORA, N_HEADS * HEAD_DIM).astype(BF16)
    ql = _whole_call(_mla_absorb_body, jax.ShapeDtypeStruct((nb, N_HEADS * KV_LORA), F32), "mla_absorb", qs, wuk2)
    o_lat = _mla_sample(ql.reshape(nb, N_HEADS, KV_LORA), qs.reshape(nb, N_HEADS, HEAD_PAD),
                        lat_s.reshape(nb, 1, -1), cache, page_table)
    attn_s = _whole_call(_mla_unabsorb_body, jax.ShapeDtypeStruct((nb, d), F32), "mla_unabsorb",
                         o_lat.reshape(nb, N_HEADS * KV_LORA), wv)
    return attn_p.reshape(b * t, d), attn_s, lat_p.reshape(b, t, -1), lat_s.reshape(nb, 1, -1)


CMP_HIDDEN = 2 * HEAD_DIM
KV_PAIR = 2 * HEAD_DIM


def _nsa_pe_body(pe_ref, w1_ref, o_ref):
    for c in range(2):
        o_ref[:, c * CMP_HIDDEN:(c + 1) * CMP_HIDDEN] = _dot(pe_ref[c], w1_ref[c])


def _compress(load_rows, n_h, wblk_ref, peh_ref, w2k_ref, w2v_ref):
    hid = jnp.zeros((n_h, 4 * CMP_HIDDEN), F32)
    for s in range(CMP_STRIDE):
        hid = hid + _dot(load_rows(s).astype(BF16), wblk_ref[s])
    up = lambda x: pltpu.roll(x, n_h - 1, 0)
    peh = peh_ref[0:1, :]
    hk = peh[:, 0:CMP_HIDDEN] + hid[:, 0:CMP_HIDDEN] + up(hid[:, CMP_HIDDEN:2 * CMP_HIDDEN])
    hv = peh[:, CMP_HIDDEN:] + hid[:, 2 * CMP_HIDDEN:3 * CMP_HIDDEN] + up(hid[:, 3 * CMP_HIDDEN:])
    return _dot(jax.nn.gelu(hk).astype(BF16), w2k_ref[...]) + _dot(jax.nn.gelu(hv).astype(BF16), w2v_ref[...])


def _nsa_compress_body(cmp_ref, wblk_ref, peh_ref, w2k_ref, w2v_ref, o_ref):
    n_h = o_ref.shape[1]
    o_ref[0] = _compress(lambda s: cmp_ref[0, pl.ds(s, n_h, stride=CMP_STRIDE), :], n_h,
                         wblk_ref, peh_ref, w2k_ref, w2v_ref)


def _nsa_compress_prompt(cmp, consts):
    b, t, _ = cmp.shape
    n_h = t // CMP_STRIDE
    return pl.pallas_call(
        _nsa_compress_body,
        grid=(b,),
        in_specs=[pl.BlockSpec((1, t, KV_PAIR), lambda i: (i, 0, 0))] + [_const_spec(c.shape) for c in consts],
        out_specs=pl.BlockSpec((1, n_h, KV_PAIR), lambda i: (i, 0, 0)),
        out_shape=jax.ShapeDtypeStruct((b, n_h, KV_PAIR), F32),
        compiler_params=_cparams("parallel"),
        name="nsa_compress",
    )(cmp, *consts)


def _intersect_matrix(n_c, n_s):
    n = lax.broadcasted_iota(jnp.int32, (n_c, n_s), 0) * CMP_STRIDE
    j = lax.broadcasted_iota(jnp.int32, (n_c, n_s), 1) * SLC_BLOCK
    return ((n < j + SLC_BLOCK) & (n + CMP_BLOCK > j)).astype(BF16)


def _split_dot(x, w):
    hi = x.astype(BF16)
    lo = (x - hi.astype(F32)).astype(BF16)
    return _dot(hi, w) + _dot(lo, w)


def _softmax_update(s, valid, v, m_ref, l_ref, acc_ref, rows):
    m = m_ref[rows, :]
    mn = jnp.maximum(m, jnp.max(s, axis=1, keepdims=True))
    p = jnp.where(valid, jnp.exp(s - mn), 0.0)
    alpha = jnp.exp(m - mn)
    m_ref[rows, :] = mn
    l_ref[rows, :] = alpha * l_ref[rows, :] + jnp.sum(p, axis=1, keepdims=True)
    acc_ref[rows, :] = alpha * acc_ref[rows, :] + _dot(p.astype(BF16), v)


def _nsa_prompt_body(q_ref, gate_ref, slc_ref, win_ref, kvc_ref, biasc_ref, rev_ref, o_ref,
                     qst_ref, oc_ref, base_ref, ms_ref, ls_ref, accs_ref, mw_ref, lw_ref, accw_ref, *, n_s):
    qi = pl.program_id(1)
    nq = pl.num_programs(1)
    tq = Q_TILE
    n_c = kvc_ref.shape[1]
    q0 = qi * tq
    for h in range(N_HEADS):
        qst_ref[h * tq:(h + 1) * tq, :] = (q_ref[0, :, h * HEAD_DIM:(h + 1) * HEAD_DIM] * (HEAD_DIM ** -0.5)).astype(BF16)

    kc = kvc_ref[0, :, 0:HEAD_DIM].astype(BF16)
    vc = kvc_ref[0, :, HEAD_DIM:KV_PAIR].astype(BF16)
    qpos_c = q0 + lax.broadcasted_iota(jnp.int32, (tq, n_c), 0)
    cend = lax.broadcasted_iota(jnp.int32, (tq, n_c), 1) * CMP_STRIDE + CMP_BLOCK
    valid_c = cend <= qpos_c + 1

    def cmp_head(h, psum):
        rows = pl.ds(pl.multiple_of(h * tq, tq), tq)
        s = jnp.where(valid_c, _dot_nt(qst_ref[rows, :], kc) + biasc_ref[h], NEG_INF)
        m = jnp.max(s, axis=1, keepdims=True)
        e = jnp.where(valid_c, jnp.exp(s - m), 0.0)
        p = e / jnp.maximum(jnp.sum(e, axis=1, keepdims=True), 1e-30)
        oc_ref[rows, :] = _dot(p.astype(BF16), vc)
        return psum + p

    psum = lax.fori_loop(0, N_HEADS, cmp_head, jnp.zeros((tq, n_c), F32))
    imp = _split_dot(psum, _intersect_matrix(n_c, n_s))

    qblk = (q0 + lax.broadcasted_iota(jnp.int32, (tq, n_s), 0)) >> 6
    jb = lax.broadcasted_iota(jnp.int32, (tq, n_s), 1)
    forced = (jb == 0) | (jb == qblk) | (jb == qblk - 1)
    score = jnp.where(jb <= qblk, imp + jnp.where(forced, FORCE_BONUS, 0.0), NEG_INF)
    rank = jnp.zeros((tq, n_s), F32)
    for j in range(n_s):
        col = score[:, j:j + 1]
        rank = rank + ((col > score) | ((col == score) & (jb > j))).astype(F32)
    sel = (rank < N_SELECT).astype(BF16)

    for ref in (ms_ref, mw_ref):
        ref[...] = jnp.full(ref.shape, NEG_INF, F32)
    for ref in (ls_ref, lw_ref, accs_ref, accw_ref):
        ref[...] = jnp.zeros(ref.shape, F32)

    i_idx = lax.broadcasted_iota(jnp.int32, (tq, tq), 0)
    j_idx = lax.broadcasted_iota(jnp.int32, (tq, tq), 1)
    e_row = lax.broadcasted_iota(jnp.int32, (n_s, tq), 0)
    e_lane = lax.broadcasted_iota(jnp.int32, (n_s, tq), 1)
    win_tiles = NSA_WINDOW // tq

    def key_tile(kj, carry):
        dist = (qi - kj) * tq + i_idx - j_idx
        expand = (((kj * tq + e_lane) >> 6) == e_row).astype(BF16)
        valid_s = (_dot(sel, expand) > 0.5) & (dist >= 0)
        valid_w = (dist >= 0) & (dist <= NSA_WINDOW)
        any_s = jnp.max(valid_s.astype(F32)) > 0.5
        in_win = kj >= qi - win_tiles
        off = pl.multiple_of(kj * tq, tq)
        ks = slc_ref[0, pl.ds(off, tq), 0:HEAD_DIM].astype(BF16)
        vs = slc_ref[0, pl.ds(off, tq), HEAD_DIM:KV_PAIR].astype(BF16)
        kw = win_ref[0, pl.ds(off, tq), 0:HEAD_DIM].astype(BF16)
        vw = win_ref[0, pl.ds(off, tq), HEAD_DIM:KV_PAIR].astype(BF16)
        blk = nq - 1 - (qi - kj)
        base_ref[:, 0:tq] = rev_ref[blk]
        base_ref[:, tq:2 * tq] = rev_ref[blk + 1]

        @pl.when(any_s | in_win)
        def _():
            def head(h, c):
                rows = pl.ds(pl.multiple_of(h * tq, tq), tq)
                qh = qst_ref[rows, :]
                bias = _toeplitz(base_ref[pl.ds(h, 1), :], tq, tq + 1)[:, 0:tq]

                @pl.when(any_s)
                def _():
                    s = jnp.where(valid_s, _dot_nt(qh, ks) + bias, NEG_INF)
                    _softmax_update(s, valid_s, vs, ms_ref, ls_ref, accs_ref, rows)

                @pl.when(in_win)
                def _():
                    s = jnp.where(valid_w, _dot_nt(qh, kw) + bias, NEG_INF)
                    _softmax_update(s, valid_w, vw, mw_ref, lw_ref, accw_ref, rows)
                return c
            lax.fori_loop(0, N_HEADS, head, 0)
        return carry

    lax.fori_loop(0, qi + 1, key_tile, 0)

    gates = jax.nn.sigmoid(gate_ref[0])
    for h in range(N_HEADS):
        rows = slice(h * tq, (h + 1) * tq)
        o_s = accs_ref[rows, :] / jnp.maximum(ls_ref[rows, :], 1e-30)
        o_w = accw_ref[rows, :] / jnp.maximum(lw_ref[rows, :], 1e-30)
        o_ref[0, :, h * HEAD_DIM:(h + 1) * HEAD_DIM] = (
            gates[:, h:h + 1] * oc_ref[rows, :] + gates[:, N_HEADS + h:N_HEADS + h + 1] * o_s
            + gates[:, 2 * N_HEADS + h:2 * N_HEADS + h + 1] * o_w)


def _nsa_prompt(q, gate, slc, win, kvc, bias_c, rev):
    b, t, d = q.shape
    tq = Q_TILE
    n_c = kvc.shape[1]
    rows = N_HEADS * tq
    col = lambda: pltpu.VMEM((rows, 1), F32)
    acc = lambda: pltpu.VMEM((rows, HEAD_DIM), F32)
    return pl.pallas_call(
        functools.partial(_nsa_prompt_body, n_s=t // SLC_BLOCK),
        grid=(b, t // tq),
        in_specs=[pl.BlockSpec((1, tq, d), lambda i, j: (i, j, 0)),
                  pl.BlockSpec((1, tq, gate.shape[2]), lambda i, j: (i, j, 0)),
                  pl.BlockSpec((1, t, KV_PAIR), lambda i, j: (i, 0, 0)),
                  pl.BlockSpec((1, t, KV_PAIR), lambda i, j: (i, 0, 0)),
                  pl.BlockSpec((1, n_c, KV_PAIR), lambda i, j: (i, 0, 0)),
                  pl.BlockSpec((N_HEADS, tq, n_c), lambda i, j: (0, j, 0)),
                  _const_spec(rev.shape)],
        out_specs=pl.BlockSpec((1, tq, d), lambda i, j: (i, j, 0)),
        out_shape=jax.ShapeDtypeStruct((b, t, d), F32),
        scratch_shapes=[pltpu.VMEM((rows, HEAD_DIM), BF16), acc(), pltpu.VMEM((N_HEADS, 2 * tq), F32),
                        col(), col(), acc(), col(), col(), acc()],
        compiler_params=_cparams("parallel", "arbitrary"),
        name="nsa_prompt",
    )(q, gate, slc, win, kvc, bias_c, rev)


NSA_CHUNK = 512


def _nsa_sample_body(pt_ref, q_ref, gate_ref, new_ref, cwin_ref, bc_ref, bs_ref, bw_ref, b0_ref,
                     wblk_ref, peh_ref, w2k_ref, w2v_ref, pool_ref, o_ref, buf_ref, sem_ref, *, n_pages, n_sp):
    slot = _paged_prefetch(pool_ref, buf_ref, sem_ref, pt_ref, n_pages)
    past = n_pages * PAGE
    n_h = past // CMP_STRIDE
    q = (q_ref[0] * (HEAD_DIM ** -0.5)).astype(BF16)
    qf = q.astype(F32)

    kvc = _compress(lambda s: buf_ref[slot, pl.ds(2 * s, n_h, stride=2 * CMP_STRIDE), :], n_h,
                    wblk_ref, peh_ref, w2k_ref, w2v_ref)
    kc = kvc[:, 0:HEAD_DIM].astype(BF16)
    vc = kvc[:, HEAD_DIM:KV_PAIR].astype(BF16)
    cend = lax.broadcasted_iota(jnp.int32, (N_HEADS, n_h), 1) * CMP_STRIDE + CMP_BLOCK
    valid_c = cend <= past + 1
    s = jnp.where(valid_c, _dot_nt(q, kc) + bc_ref[...], NEG_INF)
    m = jnp.max(s, axis=1, keepdims=True)
    e = jnp.where(valid_c, jnp.exp(s - m), 0.0)
    p_c = e / jnp.maximum(jnp.sum(e, axis=1, keepdims=True), 1e-30)
    o_c = _dot(p_c.astype(BF16), vc)

    psum = jnp.broadcast_to(jnp.sum(p_c, axis=0, keepdims=True), (8, n_h))
    imp = _split_dot(psum, _intersect_matrix(n_h, n_sp))[0:1, :]
    qblk = past // SLC_BLOCK
    jb = lax.broadcasted_iota(jnp.int32, (1, n_sp), 1)
    forced = (jb == 0) | (jb == qblk) | (jb == qblk - 1)
    score = jnp.where(jb <= qblk, imp + jnp.where(forced, FORCE_BONUS, 0.0), NEG_INF)
    r_idx = lax.broadcasted_iota(jnp.int32, (n_sp, n_sp), 0)
    c_idx = lax.broadcasted_iota(jnp.int32, (n_sp, n_sp), 1)
    score_b = jnp.broadcast_to(score, (n_sp, n_sp))
    score_col = jnp.sum(jnp.where(r_idx == c_idx, score_b, 0.0), axis=1, keepdims=True)
    beats = (score_col > score_b) | ((score_col == score_b) & (r_idx < c_idx))
    rank = jnp.sum(beats.astype(F32), axis=0, keepdims=True)
    sel = jnp.broadcast_to((rank < N_SELECT).astype(BF16), (8, n_sp))
    sel_new = rank[:, qblk:qblk + 1] < N_SELECT

    def fold_new(kv_new, valid):
        k_new = kv_new[:, 0:HEAD_DIM].astype(BF16).astype(F32)
        v_new = kv_new[:, HEAD_DIM:KV_PAIR].astype(BF16).astype(F32)
        s_new = jnp.sum(qf * k_new, axis=1, keepdims=True) + b0_ref[:, 0:1]
        ok = jnp.broadcast_to(valid, (N_HEADS, 1))
        return (jnp.where(ok, s_new, NEG_INF), jnp.where(ok, 1.0, 0.0),
                jnp.where(ok, jnp.broadcast_to(v_new, (N_HEADS, HEAD_DIM)), 0.0))

    def update(carry, s, valid, v):
        m, l, acc = carry
        mn = jnp.maximum(m, jnp.max(s, axis=1, keepdims=True))
        p = jnp.where(valid, jnp.exp(s - mn), 0.0)
        alpha = jnp.exp(m - mn)
        return mn, alpha * l + jnp.sum(p, axis=1, keepdims=True), alpha * acc + _dot(p.astype(BF16), v)

    carry = fold_new(new_ref[0, 0:1, :], sel_new)
    e_row = lax.broadcasted_iota(jnp.int32, (n_sp, NSA_CHUNK), 0)
    e_lane = lax.broadcasted_iota(jnp.int32, (n_sp, NSA_CHUNK), 1)
    for c in range(past // NSA_CHUNK):
        kv = buf_ref[slot, pl.ds(1 + 2 * c * NSA_CHUNK, NSA_CHUNK, stride=2), :]
        expand = (((c * NSA_CHUNK + e_lane) >> 6) == e_row).astype(BF16)
        valid = jnp.broadcast_to(_dot(sel, expand)[0:1, :] > 0.5, (N_HEADS, NSA_CHUNK))
        s = jnp.where(valid, _dot_nt(q, kv[:, 0:HEAD_DIM].astype(BF16)) + bs_ref[:, c * NSA_CHUNK:(c + 1) * NSA_CHUNK], NEG_INF)
        carry = update(carry, s, valid, kv[:, HEAD_DIM:KV_PAIR].astype(BF16))
    o_s = carry[2] / jnp.maximum(carry[1], 1e-30)

    carry = fold_new(new_ref[0, 1:2, :], jnp.full((1, 1), True))
    kw = cwin_ref[0]
    s = _dot_nt(q, kw[:, 0:HEAD_DIM].astype(BF16)) + bw_ref[...]
    carry = update(carry, s, jnp.full(s.shape, True), kw[:, HEAD_DIM:KV_PAIR].astype(BF16))
    o_w = carry[2] / jnp.maximum(carry[1], 1e-30)

    gates = jax.nn.sigmoid(gate_ref[0])
    o_ref[0] = gates[:, 0:1] * o_c + gates[:, 1:2] * o_s + gates[:, 2:3] * o_w


def _nsa_sample(q, gate_t, new, cwin, biases, consts, pool, page_table):
    nb = q.shape[0]
    n_pages = page_table.shape[1]
    past = n_pages * PAGE
    assert past % NSA_CHUNK == 0 and cwin.shape[1] <= NSA_WINDOW
    n_s = past // SLC_BLOCK + 1
    n_sp = -(-n_s // 128) * 128
    per_b = lambda a: pl.BlockSpec((1,) + a.shape[1:], lambda i, pt: (i,) + (0,) * (a.ndim - 1))
    grid_spec = pltpu.PrefetchScalarGridSpec(
        num_scalar_prefetch=1,
        grid=(nb,),
        in_specs=[per_b(q), per_b(gate_t), per_b(new), per_b(cwin)]
        + [_const_spec(a.shape) for a in biases] + [_const_spec(a.shape) for a in consts]
        + [pl.BlockSpec(memory_space=pl.ANY)],
        out_specs=pl.BlockSpec((1, N_HEADS, HEAD_DIM), lambda i, pt: (i, 0, 0)),
        scratch_shapes=[pltpu.VMEM((2, n_pages * pool.shape[1], KV_PAIR), F32), pltpu.SemaphoreType.DMA((2,))],
    )
    return pl.pallas_call(
        functools.partial(_nsa_sample_body, n_pages=n_pages, n_sp=n_sp),
        grid_spec=grid_spec,
        out_shape=jax.ShapeDtypeStruct((nb, N_HEADS, HEAD_DIM), F32),
        compiler_params=_cparams("arbitrary"),
        name="nsa_sample",
    )(page_table, q, gate_t, new, cwin, *biases, *consts, pool)


def _nsa_layer(xp, xs, cache_kv, cache_win, page_table, w_in, pe, w1, w2, rel_bias):
    b, t, d = xp.shape
    nb = xs.shape[0]
    past = page_table.shape[1] * PAGE
    ws = [w_in[:, :d]] + [w_in[:, d + i * KV_PAIR:d + (i + 1) * KV_PAIR] for i in range(3)] + [w_in[:, d + 3 * KV_PAIR:]]
    ws = [w.astype(BF16) for w in ws]

    w1r = w1.reshape(2, 2, CMP_STRIDE, HEAD_DIM, CMP_HIDDEN)
    zero = jnp.zeros((CMP_STRIDE, HEAD_DIM, CMP_HIDDEN), F32)
    top = jnp.concatenate([w1r[0, 0], w1r[0, 1], zero, zero], axis=2)
    bot = jnp.concatenate([zero, zero, w1r[1, 0], w1r[1, 1]], axis=2)
    wblk = jnp.concatenate([top, bot], axis=1).astype(BF16)
    pe8 = jnp.broadcast_to(pe.reshape(2, 1, -1), (2, 8, CMP_BLOCK * HEAD_DIM)).astype(BF16)
    peh = _whole_call(_nsa_pe_body, jax.ShapeDtypeStruct((8, 2 * CMP_HIDDEN), F32), "nsa_pe", pe8, w1.astype(BF16))
    zpad = jnp.zeros((CMP_HIDDEN, HEAD_DIM), F32)
    w2k = jnp.concatenate([w2[0], zpad], axis=1).astype(BF16)
    w2v = jnp.concatenate([zpad, w2[1]], axis=1).astype(BF16)
    consts = [wblk, peh, w2k, w2v]

    qp, cmp_p, slc_p, win_p, gate_p = _proj(xp.reshape(b * t, d), ws, _row_tile(b * t))
    r3 = lambda a: a.reshape(b, t, -1)
    kvc = _nsa_compress_prompt(r3(cmp_p), consts)
    n_c = t // CMP_STRIDE
    nq = t // Q_TILE
    tpos = np.arange(t)[:, None]
    bias_c = _bias_by_distance(rel_bias, tpos - (np.arange(n_c)[None, :] * CMP_STRIDE + CMP_BLOCK - 1))
    u = np.arange((nq + 1) * Q_TILE)
    rev = _bias_by_distance(rel_bias, Q_TILE - 1 + Q_TILE * (nq - 1) - u)
    rev = jnp.transpose(rev.reshape(N_HEADS, nq + 1, Q_TILE), (1, 0, 2))
    attn_p = _nsa_prompt(r3(qp), r3(gate_p), r3(slc_p), r3(win_p), kvc, bias_c, rev)
    kv_p = jnp.concatenate([cmp_p, slc_p], axis=1).reshape(b, t, 4, 1, HEAD_DIM)
    win_out_p = r3(win_p)[:, t - min(NSA_WINDOW, t):].reshape(b, -1, 2, 1, HEAD_DIM)

    qs, cmp_s, slc_s, win_s, gate_s = _proj(xs.reshape(nb, d), ws, _row_tile(nb))
    wb = cache_win.shape[1]
    n_h = past // CMP_STRIDE
    bc = _bias_by_distance(rel_bias, past - (np.arange(n_h) * CMP_STRIDE + CMP_BLOCK - 1))
    bs = _bias_by_distance(rel_bias, past - np.arange(past))
    bw = _bias_by_distance(rel_bias, wb - np.arange(wb))
    b0 = _bias_by_distance(rel_bias, np.zeros((128,), np.int64))
    gate_t = jnp.transpose(gate_s.reshape(nb, 3, N_HEADS), (0, 2, 1))
    new = jnp.stack([slc_s, win_s], axis=1)
    cwin = cache_win.reshape(nb, wb, KV_PAIR)
    pool = cache_kv.reshape(cache_kv.shape[0], 2 * PAGE, KV_PAIR)
    attn_s = _nsa_sample(qs.reshape(nb, N_HEADS, HEAD_DIM), gate_t, new, cwin, [bc, bs, bw, b0], consts, pool, page_table)
    kv_s = jnp.concatenate([cmp_s, slc_s], axis=1).reshape(nb, 1, 4, 1, HEAD_DIM)
    win_out_s = jnp.concatenate([cwin, win_s[:, None, :]], axis=1)[:, 1:].reshape(nb, wb, 2, 1, HEAD_DIM)
    return attn_p.reshape(b * t, d), attn_s.reshape(nb, d), kv_p, kv_s, win_out_p, win_out_s


def kernel(x_prompt, x_sample, cache_nsa_kv, cache_nsa_win, cache_mla, state_dil_w128, state_dil_w512,
           state_dil_w2048, cache_sb_kv, page_table, p_prompt, p_sample, rel_bias, ln1_g, ln1_b, ln2_g, ln2_b,
           ffn_wg, ffn_wu, ffn_wd, ple_wg, ple_wp, nsa_w_in, nsa_cmp_pe, nsa_cmp_w1, nsa_cmp_w2, nsa_w_out,
           mla_w_dq, mla_q_norm, mla_kv_norm, mla_w_uq, mla_w_uk, mla_w_uv, mla_w_out, dil_w_in, dil_w_out,
           sb_w_in, sb_w_out):
    b, t, d = x_prompt.shape
    nb = x_sample.shape[0]
    past_len = page_table.shape[1] * PAGE
    depth = p_prompt.shape[0]
    n_mixers = 4
    dil_states = (state_dil_w128, state_dil_w512, state_dil_w2048)
    xp = x_prompt.reshape(b * t, d)
    xs = x_sample.reshape(nb, d)
    outs = {k: [] for k in ("nsa_kv_p", "nsa_kv_s", "nsa_win_p", "nsa_win_s", "mla_p", "mla_s", "sb_p", "sb_s")}
    dil_p = [[] for _ in DIL_PATTERNS]
    dil_s = [[] for _ in DIL_PATTERNS]
    for i in range(depth):
        kind, j = i % n_mixers, i // n_mixers
        xp3, xs3 = xp.reshape(b, t, d), xs.reshape(nb, 1, d)
        if kind == 0:
            mp, ms, a, b_, c, e = _nsa_layer(xp3, xs3, cache_nsa_kv[j], cache_nsa_win[j], page_table, nsa_w_in[j],
                                            nsa_cmp_pe[j], nsa_cmp_w1[j], nsa_cmp_w2[j], rel_bias)
            mp, ms, w_out = [mp], [ms], nsa_w_out[j]
            outs["nsa_kv_p"].append(a)
            outs["nsa_kv_s"].append(b_)
            outs["nsa_win_p"].append(c)
            outs["nsa_win_s"].append(e)
        elif kind == 1:
            mp, ms, a, b_ = _mla_layer(xp3, xs3, cache_mla[j], page_table, mla_w_dq[j], mla_q_norm[j], mla_kv_norm[j],
                                       mla_w_uq[j], mla_w_uk[j], mla_w_uv[j], past_len)
            mp, ms, w_out = [mp], [ms], mla_w_out[j]
            outs["mla_p"].append(a)
            outs["mla_s"].append(b_)
        elif kind == 2:
            mp, ms, st_p, st_s = _dil_layer(xp3, xs3, [s[j] for s in dil_states], dil_w_in[j], rel_bias)
            ms, w_out = [ms], dil_w_out[j]
            for g in range(len(DIL_PATTERNS)):
                dil_p[g].append(st_p[g])
                dil_s[g].append(st_s[g])
        else:
            mp, ms, a, b_ = _sb_layer(xp3, xs3, cache_sb_kv[j], page_table, sb_w_in[j])
            mp, ms, w_out = [mp], [ms], sb_w_out[j]
            outs["sb_p"].append(a)
            outs["sb_s"].append(b_)
        row = lambda v: v.reshape(1, -1)
        consts = (w_out.astype(BF16), row(ln1_g[i]), row(ln1_b[i]), row(ln2_g[i]), row(ln2_b[i]),
                  ffn_wg[i].astype(BF16), ffn_wu[i].astype(BF16), ffn_wd[i].astype(BF16),
                  ple_wg[i].astype(BF16), ple_wp[i].astype(BF16))
        xp = _tail(mp, xp, p_prompt[i].reshape(b * t, -1), *consts)
        xs = _tail(ms, xs, p_sample[i].reshape(nb, -1), *consts)
    st = jnp.stack
    return (xp.reshape(b, t, d), xs.reshape(nb, 1, d),
            st(outs["nsa_kv_p"]), st(outs["nsa_kv_s"]), st(outs["nsa_win_p"]), st(outs["nsa_win_s"]),
            st(outs["mla_p"]), st(outs["mla_s"]),
            st(dil_p[0]), st(dil_s[0]), st(dil_p[1]), st(dil_s[1]), st(dil_p[2]), st(dil_s[2]),
            st(outs["sb_p"]), st(outs["sb_s"]))
```

```python
import functools
import math

import numpy as np
import jax
import jax.numpy as jnp
from jax import lax
from jax.experimental import pallas as pl
from jax.experimental.pallas import tpu as pltpu

F32 = jnp.float32
BF16 = jnp.bfloat16

HEAD_DIM = 64
N_HEADS = 16
PAGE = 128
Q_TILE = 128
LN_EPS = 1e-5
NEG_INF = -1e30
DEPTH = 4
ALPHA = (2 * DEPTH) ** 0.25
N_BUCKETS = 32
MAX_DISTANCE = 2048
CMP_BLOCK = 32
CMP_STRIDE = 16
SLC_BLOCK = 64
N_SELECT = 16
NSA_WINDOW = 512
FORCE_BONUS = 1e4
DIL_PATTERNS = ((128, 1), (512, 4), (2048, 16))
ROPE_THETA = 10000.0
QK_ROPE = 32
KV_LORA = 256
Q_LORA = 256
SB_DEAD = -104.0
MASKED_ROW_FLOOR = -1e29
VMEM_LIMIT_BYTES = 56 * 1024 * 1024


def _cparams(*sem):
    return pltpu.CompilerParams(dimension_semantics=sem, vmem_limit_bytes=VMEM_LIMIT_BYTES)


def _dot(a, b):
    return jnp.dot(a, b, preferred_element_type=F32)


def _dot_nt(a, b):
    return lax.dot_general(a, b, (((1,), (1,)), ((), ())), preferred_element_type=F32)


def _const_spec(shape):
    nd = len(shape)
    return pl.BlockSpec(shape, lambda *_: (0,) * nd)


def _bucket_of_distance(n_dist):
    n = np.arange(n_dist, dtype=np.int64)
    max_exact = N_BUCKETS // 2
    ratio = np.maximum(n, max_exact).astype(np.float32) / np.float32(max_exact)
    log_ratio = np.log(ratio).astype(np.float32) / np.float32(math.log(MAX_DISTANCE / max_exact))
    large = np.minimum(max_exact + (log_ratio * np.float32(N_BUCKETS - max_exact)).astype(np.int32), N_BUCKETS - 1)
    return np.where(n < max_exact, n, large).astype(np.int32)


def _bias_by_distance(rel_bias, dists):
    d = np.asarray(dists)
    bucket = _bucket_of_distance(int(d.max()) + 1)[np.maximum(d, 0)]
    vals = jnp.moveaxis(rel_bias.astype(F32)[bucket], -1, 0)
    return jnp.where(jnp.asarray(d >= 0), vals, 0.0)


def _proj_body(x_ref, *refs, n_row, n_col):
    xb = x_ref[...].astype(BF16)
    n = n_row + n_col
    for w_ref, o_ref in zip(refs[:n_row], refs[n:n + n_row]):
        o_ref[...] = _dot(xb, w_ref[...])
    for w_ref, o_ref in zip(refs[n_row:n], refs[n + n_row:]):
        o_ref[0] = _dot_nt(w_ref[...], xb)


def _proj(x, ws, wts, n_batch):
    m, k = x.shape
    tm = _row_tile(m)
    t = m // n_batch
    per_b = t // tm
    return pl.pallas_call(
        functools.partial(_proj_body, n_row=len(ws), n_col=len(wts)),
        grid=(m // tm,),
        in_specs=[pl.BlockSpec((tm, k), lambda i: (i, 0))] + [_const_spec(w.shape) for w in ws + wts],
        out_specs=[pl.BlockSpec((tm, w.shape[1]), lambda i: (i, 0)) for w in ws]
        + [pl.BlockSpec((1, w.shape[0], tm), lambda i: (i // per_b, 0, i % per_b)) for w in wts],
        out_shape=[jax.ShapeDtypeStruct((m, w.shape[1]), F32) for w in ws]
        + [jax.ShapeDtypeStruct((n_batch, w.shape[0], t), F32) for w in wts],
        compiler_params=_cparams("parallel"),
        name="proj",
    )(x, *ws, *wts)


def _row_tile(m):
    return 256 if m % 256 == 0 else m


def _layer_norm(x, g, b):
    mu = jnp.mean(x, axis=-1, keepdims=True)
    xc = x - mu
    var = jnp.mean(xc * xc, axis=-1, keepdims=True)
    return xc * lax.rsqrt(var + LN_EPS) * g + b


FF_CHUNK = 256


def _tail_body(*refs, n_mix):
    mix_refs = refs[:n_mix]
    (x_ref, p_ref, wo_ref, g1_ref, b1_ref, g2_ref, b2_ref, wg_ref, wu_ref, wd_ref,
     pwg_ref, pwp_ref, o_ref) = refs[n_mix:]
    if n_mix == 1:
        attn = mix_refs[0][...]
    else:
        n_g = n_mix // 2
        lses = [r[...] for r in mix_refs[n_g:]]
        m = functools.reduce(jnp.maximum, lses)
        es = [jnp.exp(l - m) for l in lses]
        den = functools.reduce(lambda a, b: a + b, es)
        ws = [e / den for e in es]
        cols = []
        for h in range(N_HEADS):
            sl = slice(h * HEAD_DIM, (h + 1) * HEAD_DIM)
            acc = None
            for g in range(n_g):
                term = ws[g][:, h:h + 1] * mix_refs[g][:, sl]
                acc = term if acc is None else acc + term
            cols.append(acc)
        attn = jnp.concatenate(cols, axis=1)
    x = x_ref[...]
    mix = _dot(attn.astype(BF16), wo_ref[...])
    h1 = _layer_norm(ALPHA * x + mix, g1_ref[...], b1_ref[...])
    h1b = h1.astype(BF16)
    d_ff = wg_ref.shape[1]
    acc = jnp.zeros(x.shape, F32)
    for c in range(d_ff // FF_CHUNK):
        sl = slice(c * FF_CHUNK, (c + 1) * FF_CHUNK)
        g = _dot(h1b, wg_ref[:, sl])
        u = _dot(h1b, wu_ref[:, sl])
        acc = acc + _dot((g * jax.nn.sigmoid(g) * u).astype(BF16), wd_ref[sl, :])
    h2 = _layer_norm(ALPHA * h1 + acc, g2_ref[...], b2_ref[...])
    gate = jax.nn.sigmoid(_dot(h2.astype(BF16), pwg_ref[...]))
    o_ref[...] = h2 + gate * _dot(p_ref[...].astype(BF16), pwp_ref[...])


def _tail(mix_list, x, p_all, layer, wo, g1, b1, g2, b2, wg, wu, wd, pwg, pwp):
    m, d = x.shape
    tm = _row_tile(m)
    consts = [wo, g1, b1, g2, b2, wg, wu, wd, pwg, pwp]
    row = lambda a: pl.BlockSpec((tm, a.shape[1]), lambda i: (i, 0))
    single = lambda a: pl.BlockSpec(a.shape, lambda i: (0, 0), pipeline_mode=pl.Buffered(1))
    return pl.pallas_call(
        functools.partial(_tail_body, n_mix=len(mix_list)),
        grid=(m // tm,),
        in_specs=[row(a) for a in mix_list]
        + [row(x), pl.BlockSpec((None, tm, p_all.shape[2]), lambda i: (layer, i, 0))] + [single(a) for a in consts],
        out_specs=pl.BlockSpec((tm, d), lambda i: (i, 0)),
        out_shape=jax.ShapeDtypeStruct((m, d), F32),
        compiler_params=_cparams("parallel"),
        name="tail",
    )(*mix_list, x, p_all, *consts)


def _page_copies(pool_ref, buf_ref, sem_ref, pt_ref, b, slot, n_pages, start):
    def body(p, carry):
        if len(buf_ref.shape) == 4:
            dst = buf_ref.at[slot, p]
        else:
            rows = pool_ref.shape[1]
            dst = buf_ref.at[slot, pl.ds(pl.multiple_of(p * rows, rows), rows)]
        cp = pltpu.make_async_copy(pool_ref.at[pt_ref[b, p]], dst, sem_ref.at[slot])
        if start:
            cp.start()
        else:
            cp.wait()
        return carry
    lax.fori_loop(0, n_pages, body, 0)


def _paged_prefetch(pool_ref, buf_ref, sem_ref, pt_ref, n_pages):
    b = pl.program_id(0)
    nb = pl.num_programs(0)
    slot = b % 2

    @pl.when(b == 0)
    def _():
        _page_copies(pool_ref, buf_ref, sem_ref, pt_ref, 0, 0, n_pages, True)

    @pl.when(b + 1 < nb)
    def _():
        _page_copies(pool_ref, buf_ref, sem_ref, pt_ref, b + 1, 1 - slot, n_pages, True)

    _page_copies(pool_ref, buf_ref, sem_ref, pt_ref, b, slot, n_pages, False)
    return slot


def _sb_terms(z, valid):
    t = jnp.log1p(jnp.exp(-jnp.abs(z)))
    l1m = -jnp.maximum(z, 0.0) - t
    if valid is not None:
        l1m = jnp.where(valid, l1m, 0.0)
    ls = jnp.minimum(z, 0.0) - t
    return l1m, ls


def _strict_upper_sum_matrix(n):
    j = lax.broadcasted_iota(jnp.int32, (2 * n, n), 0) & (n - 1)
    s = lax.broadcasted_iota(jnp.int32, (2 * n, n), 1)
    return (j > s).astype(BF16)


def _tail_sums(l1m, uu):
    hi = l1m.astype(BF16)
    lo = (l1m - hi.astype(F32)).astype(BF16)
    return _dot(jnp.concatenate([hi, lo], axis=1), uu)


def _sb_prompt_body(q_ref, kv_ref, o_ref, r_ref, acc_ref, *, n_kv, grp):
    qi = pl.program_id(1)
    tq = q_ref.shape[1]
    rows = grp * tq
    uu = _strict_upper_sum_matrix(Q_TILE)
    qpos = qi * tq + (lax.broadcasted_iota(jnp.int32, (rows, Q_TILE), 0) & (tq - 1))
    lane = lax.broadcasted_iota(jnp.int32, (rows, Q_TILE), 1)
    for g in range(n_kv):
        qg = jnp.concatenate(
            [q_ref[0, :, (g * grp + u) * HEAD_DIM:(g * grp + u + 1) * HEAD_DIM] for u in range(grp)], axis=0)
        qg = (qg * (HEAD_DIM ** -0.5)).astype(BF16)
        r_ref[...] = jnp.zeros(r_ref.shape, F32)
        acc_ref[...] = jnp.zeros(acc_ref.shape, F32)

        def cond(c):
            return jnp.logical_and(c[0] >= 0, c[1] > SB_DEAD)

        def body(c):
            kj = c[0]
            off = pl.multiple_of(kj * Q_TILE, Q_TILE)
            k = kv_ref[0, g * HEAD_DIM:(g + 1) * HEAD_DIM, pl.ds(off, Q_TILE)].astype(BF16)
            v = kv_ref[0, (n_kv + g) * HEAD_DIM:(n_kv + g + 1) * HEAD_DIM, pl.ds(off, Q_TILE)].astype(BF16)
            z = _dot(qg, k)
            valid = (kj * Q_TILE + lane) < qpos
            l1m, ls = _sb_terms(z, valid)
            r = r_ref[...]
            a = jnp.where(valid, jnp.exp(ls + _tail_sums(l1m, uu) + r), 0.0)
            acc_ref[...] += _dot_nt(a.astype(BF16), v)
            rn = r + jnp.sum(l1m, axis=1, keepdims=True)
            r_ref[...] = rn
            return kj - 1, jnp.max(rn)

        lax.while_loop(cond, body, (qi, jnp.float32(0.0)))
        for u in range(grp):
            h = g * grp + u
            o_ref[0, :, h * HEAD_DIM:(h + 1) * HEAD_DIM] = acc_ref[u * tq:(u + 1) * tq, :]


def _sb_prompt(q, kv, n_kv):
    b, t, d = q.shape
    grp = N_HEADS // n_kv
    return pl.pallas_call(
        functools.partial(_sb_prompt_body, n_kv=n_kv, grp=grp),
        grid=(b, t // Q_TILE),
        in_specs=[pl.BlockSpec((1, Q_TILE, d), lambda i, j: (i, j, 0)),
                  pl.BlockSpec((1, kv.shape[1], t), lambda i, j: (i, 0, 0))],
        out_specs=pl.BlockSpec((1, Q_TILE, d), lambda i, j: (i, j, 0)),
        out_shape=jax.ShapeDtypeStruct((b, t, d), F32),
        scratch_shapes=[pltpu.VMEM((grp * Q_TILE, 1), F32), pltpu.VMEM((grp * Q_TILE, HEAD_DIM), F32)],
        compiler_params=_cparams("parallel", "arbitrary"),
        name="sb_prompt",
    )(q, kv)


def _head_spread(n_kv):
    d = lax.broadcasted_iota(jnp.int32, (HEAD_DIM, n_kv * HEAD_DIM), 0)
    c = lax.broadcasted_iota(jnp.int32, (HEAD_DIM, n_kv * HEAD_DIM), 1)
    return ((c & (HEAD_DIM - 1)) == d).astype(BF16)


def _block_diag_q(q, n_kv, scale):
    grp = N_HEADS // n_kv
    width = n_kv * HEAD_DIM
    spread = _dot((q * scale).astype(BF16), _head_spread(n_kv))
    row = lax.broadcasted_iota(jnp.int32, (N_HEADS, width), 0)
    col = lax.broadcasted_iota(jnp.int32, (N_HEADS, width), 1)
    own = (col >> 6) == (row >> int(math.log2(grp)))
    return jnp.where(own, spread, 0.0).astype(BF16)


def _block_diag_pick(o_full, n_kv):
    grp = N_HEADS // n_kv
    row = lax.broadcasted_iota(jnp.int32, (N_HEADS, HEAD_DIM), 0)
    out = jnp.zeros((N_HEADS, HEAD_DIM), F32)
    for g in range(n_kv):
        out = out + jnp.where((row >> int(math.log2(grp))) == g, o_full[:, g * HEAD_DIM:(g + 1) * HEAD_DIM], 0.0)
    return out


def _sb_sample_body(pt_ref, q_ref, pool_ref, o_ref, buf_ref, sem_ref, *, n_kv, n_pages):
    slot = _paged_prefetch(pool_ref, buf_ref, sem_ref, pt_ref, n_pages)
    width = n_kv * HEAD_DIM
    qbd = _block_diag_q(q_ref[0], n_kv, HEAD_DIM ** -0.5)
    uu = _strict_upper_sum_matrix(PAGE)

    def cond(c):
        return jnp.logical_and(c[0] >= 0, c[1] > SB_DEAD)

    def body(c):
        p, _, r, acc = c
        k = buf_ref[slot, p, 0:width, :].astype(BF16)
        v = buf_ref[slot, p, width:2 * width, :].astype(BF16)
        l1m, ls = _sb_terms(_dot(qbd, k), None)
        a = jnp.exp(ls + _tail_sums(l1m, uu) + r)
        acc = acc + _dot_nt(a.astype(BF16), v)
        rn = r + jnp.sum(l1m, axis=1, keepdims=True)
        return p - 1, jnp.max(rn), rn, acc

    init = (jnp.int32(n_pages - 1), jnp.float32(0.0), jnp.zeros((N_HEADS, 1), F32), jnp.zeros((N_HEADS, width), F32))
    acc = lax.while_loop(cond, body, init)[3]
    o_ref[0] = _block_diag_pick(acc, n_kv)


def _sb_sample(q, pool, page_table, n_kv):
    nb = q.shape[0]
    n_pages = page_table.shape[1]
    feat = pool.shape[1]
    grid_spec = pltpu.PrefetchScalarGridSpec(
        num_scalar_prefetch=1,
        grid=(nb,),
        in_specs=[pl.BlockSpec((1, N_HEADS, HEAD_DIM), lambda i, pt: (i, 0, 0)),
                  pl.BlockSpec(memory_space=pl.ANY)],
        out_specs=pl.BlockSpec((1, N_HEADS, HEAD_DIM), lambda i, pt: (i, 0, 0)),
        scratch_shapes=[pltpu.VMEM((2, n_pages, feat, PAGE), F32), pltpu.SemaphoreType.DMA((2,))],
    )
    return pl.pallas_call(
        functools.partial(_sb_sample_body, n_kv=n_kv, n_pages=n_pages),
        grid_spec=grid_spec,
        out_shape=jax.ShapeDtypeStruct((nb, N_HEADS, HEAD_DIM), F32),
        compiler_params=_cparams("arbitrary"),
        name="sb_sample",
    )(page_table, q, pool)


def _feature_major_pool(cache):
    n = cache.ndim
    return jnp.transpose(cache, (0,) + tuple(range(2, n)) + (1,)).reshape(cache.shape[0], -1, cache.shape[1])


def _token_major(x_t, feat_shape):
    b, _, t = x_t.shape
    nf = len(feat_shape)
    return jnp.transpose(x_t.reshape((b,) + tuple(feat_shape) + (t,)), (0, nf + 1) + tuple(range(1, nf + 1)))


def _sb_layer(xp, xs, cache, page_table, w_in):
    b, t, d = xp.shape
    n_kv = (w_in.shape[1] - d) // (2 * HEAD_DIM)
    ws, wts = [w_in[:, :d].astype(BF16)], [w_in[:, d:].T.astype(BF16)]
    qp, kvp_t = _proj(xp.reshape(b * t, d), ws, wts, b)
    attn_p = _sb_prompt(qp.reshape(b, t, d), kvp_t, n_kv)
    nb = xs.shape[0]
    qs, kvs_t = _proj(xs.reshape(nb, d), ws, wts, 1)
    attn_s = _sb_sample(qs.reshape(nb, N_HEADS, HEAD_DIM), _feature_major_pool(cache), page_table, n_kv)
    kv_s = _token_major(kvs_t, (2, n_kv, HEAD_DIM)).reshape(nb, 1, 2, n_kv, HEAD_DIM)
    return attn_p.reshape(b * t, d), attn_s.reshape(nb, d), _token_major(kvp_t, (2, n_kv, HEAD_DIM)), kv_s


def _toeplitz(base_row, rows, shift):
    return pltpu.roll(jnp.broadcast_to(base_row, (rows, base_row.shape[1])), shift, 1, stride=1, stride_axis=0)


def _dil_prompt_body(q_ref, kvc_ref, kvp_ref, base_ref, o_ref, lse_ref, *, n_kv, grp):
    mi = pl.program_id(2)
    tq = Q_TILE
    i_idx = lax.broadcasted_iota(jnp.int32, (tq, 2 * tq), 0)
    j_idx = lax.broadcasted_iota(jnp.int32, (tq, 2 * tq), 1)
    steps = i_idx - j_idx + tq
    valid = (steps >= 0) & (steps <= tq) & ((mi > 0) | (j_idx >= tq))
    lane = lax.broadcasted_iota(jnp.int32, (tq, 128), 1)
    lse_tile = jnp.zeros((tq, 128), F32)
    width = n_kv * HEAD_DIM
    for g in range(n_kv):
        ksl = slice(g * HEAD_DIM, (g + 1) * HEAD_DIM)
        vsl = slice(width + g * HEAD_DIM, width + (g + 1) * HEAD_DIM)
        k = jnp.concatenate([kvp_ref[0, :, ksl], kvc_ref[0, :, ksl]], axis=0).astype(BF16)
        v = jnp.concatenate([kvp_ref[0, :, vsl], kvc_ref[0, :, vsl]], axis=0).astype(BF16)
        for u in range(grp):
            h = g * grp + u
            hsl = slice(h * HEAD_DIM, (h + 1) * HEAD_DIM)
            qh = (q_ref[0, :, hsl] * (HEAD_DIM ** -0.5)).astype(BF16)
            s = _dot_nt(qh, k) + _toeplitz(base_ref[h:h + 1, :], tq, 0)
            s = jnp.where(valid, s, NEG_INF)
            m = jnp.max(s, axis=1, keepdims=True)
            e = jnp.where(valid, jnp.exp(s - m), 0.0)
            den = jnp.maximum(jnp.sum(e, axis=1, keepdims=True), 1e-30)
            o_ref[0, :, hsl] = _dot((e / den).astype(BF16), v)
            lse_tile = jnp.where(lane == h, m + jnp.log(den), lse_tile)
    lse_ref[0] = lse_tile


def _dil_prompt(q, kv, base, dil, n_kv):
    b, tm, _ = q.shape
    d = N_HEADS * HEAD_DIM
    kvw = 2 * n_kv * HEAD_DIM
    nm = tm // Q_TILE
    return pl.pallas_call(
        functools.partial(_dil_prompt_body, n_kv=n_kv, grp=N_HEADS // n_kv),
        grid=(b, dil, nm),
        in_specs=[pl.BlockSpec((1, Q_TILE, d), lambda i, r, m: (i, m, r)),
                  pl.BlockSpec((1, Q_TILE, kvw), lambda i, r, m: (i, m, r)),
                  pl.BlockSpec((1, Q_TILE, kvw), lambda i, r, m: (i, jnp.maximum(m - 1, 0), r)),
                  _const_spec(base.shape)],
        out_specs=[pl.BlockSpec((1, Q_TILE, d), lambda i, r, m: (i, m, r)),
                   pl.BlockSpec((1, Q_TILE, 128), lambda i, r, m: (i, m, r))],
        out_shape=[jax.ShapeDtypeStruct((b, tm, dil * d), F32), jax.ShapeDtypeStruct((b, tm, dil * 128), F32)],
        compiler_params=_cparams("parallel", "parallel", "arbitrary"),
        name="dil_prompt",
    )(q, kv, kv, base)


def _dil_sample_body(q_ref, kvn_ref, kvnt_ref, b0_ref, bm0_ref, bm1_ref, bm2_ref, st0_ref, st1_ref, st2_ref,
                     o_ref, so0_ref, so1_ref, so2_ref, *, n_kv):
    b = pl.program_id(0)
    width = n_kv * HEAD_DIM
    nb = kvnt_ref.shape[2]
    pick = lax.broadcasted_iota(jnp.int32, (2 * width, nb), 1) == b
    outs, lses = [], []
    groups = zip((st0_ref, st1_ref, st2_ref), (so0_ref, so1_ref, so2_ref), (bm0_ref, bm1_ref, bm2_ref))
    for g, (st, so, bm) in enumerate(groups):
        w = st.shape[2]
        kn = kvn_ref[0, g:g + 1, 0:width].astype(BF16).astype(F32)
        vn = kvn_ref[0, g:g + 1, width:2 * width].astype(BF16).astype(F32)
        qbd = _block_diag_q(q_ref[0, g], n_kv, HEAD_DIM ** -0.5)
        s_old = _dot(qbd, st[0, 0:width, :].astype(BF16)) + bm[...]
        s_new = jnp.sum(qbd.astype(F32) * kn, axis=1, keepdims=True) + b0_ref[:, 0:1]
        m = jnp.maximum(jnp.max(s_old, axis=1, keepdims=True), s_new)
        e_old = jnp.exp(s_old - m)
        e_new = jnp.exp(s_new - m)
        den = jnp.sum(e_old, axis=1, keepdims=True) + e_new
        o_full = (_dot_nt((e_old / den).astype(BF16), st[0, width:2 * width, :].astype(BF16))
                  + (e_new / den).astype(BF16).astype(F32) * vn)
        outs.append(_block_diag_pick(o_full, n_kv))
        lses.append(m + jnp.log(den))
        new_col = jnp.sum(jnp.where(pick, kvnt_ref[g], 0.0), axis=1, keepdims=True)
        last = lax.broadcasted_iota(jnp.int32, (2 * width, w), 1) == w - 1
        so[0] = jnp.where(last, new_col, pltpu.roll(st[0], w - 1, 1))
    m = functools.reduce(jnp.maximum, lses)
    es = [jnp.exp(l - m) for l in lses]
    den = functools.reduce(lambda a, b: a + b, es)
    o = None
    for e, og in zip(es, outs):
        o = (e / den) * og if o is None else o + (e / den) * og
    o_ref[0] = o


def _dil_sample(q, kvn, kvn_t, b0, bias_masks, states, n_kv):
    nb = q.shape[0]
    st_specs = [pl.BlockSpec((1,) + s.shape[1:], lambda i: (i, 0, 0)) for s in states]
    outs = pl.pallas_call(
        functools.partial(_dil_sample_body, n_kv=n_kv),
        grid=(nb,),
        in_specs=[pl.BlockSpec((1,) + q.shape[1:], lambda i: (i, 0, 0, 0)),
                  pl.BlockSpec((1,) + kvn.shape[1:], lambda i: (i, 0, 0)),
                  _const_spec(kvn_t.shape), _const_spec(b0.shape)]
        + [_const_spec(bm.shape) for bm in bias_masks] + st_specs,
        out_specs=[pl.BlockSpec((1, N_HEADS, HEAD_DIM), lambda i: (i, 0, 0))] + st_specs,
        out_shape=[jax.ShapeDtypeStruct((nb, N_HEADS, HEAD_DIM), F32)]
        + [jax.ShapeDtypeStruct(s.shape, F32) for s in states],
        compiler_params=_cparams("parallel"),
        name="dil_sample",
    )(q, kvn, kvn_t, b0, *bias_masks, *states)
    return outs[0], outs[1:]


def _dil_layer(xp, xs, states, w_in, rel_bias):
    b, t, d = xp.shape
    n_g = len(DIL_PATTERNS)
    w3 = w_in.reshape(d, n_g, -1)
    kvw = w3.shape[2] - d
    n_kv = kvw // (2 * HEAD_DIM)
    ws, wts = [], []
    for g in range(n_g):
        ws += [w3[:, g, :d].astype(BF16), w3[:, g, d:].astype(BF16)]
        wts.append(w3[:, g, d:].T.astype(BF16))
    outs_p = _proj(xp.reshape(b * t, d), ws, wts, b)
    lane = np.arange(2 * Q_TILE)
    mix, st_p = [], []
    for g, (w, dil) in enumerate(DIL_PATTERNS):
        assert w // dil == Q_TILE and t % (dil * Q_TILE) == 0
        base = _bias_by_distance(rel_bias, np.where(lane <= Q_TILE, (Q_TILE - lane) * dil, -1))
        o_g, lse_g = _dil_prompt(outs_p[2 * g].reshape(b, t // dil, dil * d),
                                 outs_p[2 * g + 1].reshape(b, t // dil, dil * kvw), base, dil, n_kv)
        mix.append((o_g.reshape(b * t, d), lse_g.reshape(b * t, 128)))
        st_p.append(_token_major(outs_p[2 * n_g + g][:, :, t - min(w, t):], (2, n_kv, HEAD_DIM)))
    mix_p = [m[0] for m in mix] + [m[1] for m in mix]
    nb = xs.shape[0]
    outs_s = _proj(xs.reshape(nb, d), ws, wts, 1)
    q_s = jnp.stack([outs_s[2 * g].reshape(nb, N_HEADS, HEAD_DIM) for g in range(n_g)], axis=1)
    kvn = jnp.stack([outs_s[2 * g + 1] for g in range(n_g)], axis=1)
    kvn_t = jnp.concatenate(outs_s[2 * n_g:], axis=0)
    bias_masks = []
    for g, (w, dil) in enumerate(DIL_PATTERNS):
        pos = np.arange(states[g].shape[1])
        bias = _bias_by_distance(rel_bias, np.where(pos % dil == 0, w - pos, -1))
        bias_masks.append(jnp.where(jnp.asarray(pos % dil == 0), bias, NEG_INF))
    b0 = _bias_by_distance(rel_bias, np.zeros((128,), np.int64))
    st_t = [jnp.transpose(s, (0, 2, 3, 4, 1)).reshape(nb, kvw, s.shape[1]) for s in states]
    attn_s, st_s = _dil_sample(q_s, kvn, kvn_t, b0, bias_masks, st_t, n_kv)
    st_s = [_token_major(s, (2, n_kv, HEAD_DIM)) for s in st_s]
    return mix_p, attn_s.reshape(nb, d), st_p, st_s


HEAD_PAD = 128
ROPE_HALF = QK_ROPE // 2
MLA_SCALE = (HEAD_DIM + QK_ROPE) ** -0.5


def _rms(x, g):
    return x * lax.rsqrt(jnp.mean(x * x, axis=-1, keepdims=True) + LN_EPS) * g


def _mla_project_body(x_ref, c_ref, s_ref, wdq_ref, qn_ref, kvn_ref, wq_ref, wqs_ref, wk_ref, wv_ref, vone_ref,
                      q_ref, k_ref, v_ref, lat_ref, latt_ref):
    h = _dot(x_ref[...].astype(BF16), wdq_ref[...])
    cq = _rms(h[:, 0:Q_LORA], qn_ref[...]).astype(BF16)
    ckv = _rms(h[:, Q_LORA:Q_LORA + KV_LORA], kvn_ref[...])
    ckvb = ckv.astype(BF16)
    cos, sin = c_ref[...], s_ref[...]
    base = Q_LORA + KV_LORA
    kr = h[:, base:base + HEAD_PAD] * cos + h[:, base + HEAD_PAD:base + 2 * HEAD_PAD] * sin
    cos_all = jnp.concatenate([cos] * N_HEADS, axis=1)
    sin_all = jnp.concatenate([sin] * N_HEADS, axis=1)
    q_ref[...] = ((_dot(cq, wq_ref[...]) * cos_all + _dot(cq, wqs_ref[...]) * sin_all) * MLA_SCALE).astype(BF16)
    k_ref[...] = (_dot(ckvb, wk_ref[...]) + jnp.concatenate([kr] * N_HEADS, axis=1)).astype(BF16)
    v_ref[...] = (_dot(ckvb, wv_ref[...]) + vone_ref[...]).astype(BF16)
    lat_ref[:, 0:KV_LORA] = ckv
    lat_ref[:, KV_LORA:KV_LORA + QK_ROPE] = kr[:, HEAD_DIM:HEAD_DIM + QK_ROPE]
    latt_ref[0, 0:KV_LORA, :] = ckv.T
    latt_ref[0, KV_LORA:KV_LORA + QK_ROPE, :] = kr.T[HEAD_DIM:HEAD_DIM + QK_ROPE, :]


def _mla_project(x, cos, sin, consts, n_batch):
    m, d = x.shape
    tm = _row_tile(m)
    t = m // n_batch
    per_b = t // tm
    pos_blocks = cos.shape[0] // tm
    lat_w = KV_LORA + QK_ROPE
    wide = N_HEADS * HEAD_PAD
    rope_spec = pl.BlockSpec((tm, HEAD_PAD), lambda i: (i % pos_blocks, 0))
    return pl.pallas_call(
        _mla_project_body,
        grid=(m // tm,),
        in_specs=[pl.BlockSpec((tm, d), lambda i: (i, 0)), rope_spec, rope_spec] + [_const_spec(c.shape) for c in consts],
        out_specs=[pl.BlockSpec((tm, wide), lambda i: (i, 0)), pl.BlockSpec((tm, wide), lambda i: (i, 0)),
                   pl.BlockSpec((tm, wide), lambda i: (i, 0)), pl.BlockSpec((tm, lat_w), lambda i: (i, 0)),
                   pl.BlockSpec((1, lat_w, tm), lambda i: (i // per_b, 0, i % per_b))],
        out_shape=[jax.ShapeDtypeStruct((m, wide), BF16), jax.ShapeDtypeStruct((m, wide), BF16),
                   jax.ShapeDtypeStruct((m, wide), BF16), jax.ShapeDtypeStruct((m, lat_w), F32),
                   jax.ShapeDtypeStruct((n_batch, lat_w, t), F32)],
        compiler_params=_cparams("parallel"),
        name="mla_project",
    )(x, cos, sin, *consts)


MLA_TQ = 256
MLA_CHUNK = 512


def _mla_prompt_body(q_ref, k_ref, v_ref, o_ref, m_ref, acc_ref):
    qi = pl.program_id(2)
    tq, ch = MLA_TQ, MLA_CHUNK
    q0 = qi * tq
    m_ref[...] = jnp.full(m_ref.shape, NEG_INF, F32)
    acc_ref[...] = jnp.zeros(acc_ref.shape, F32)
    i_idx = lax.broadcasted_iota(jnp.int32, (tq, ch), 0)
    j_idx = lax.broadcasted_iota(jnp.int32, (tq, ch), 1)

    def chunk(c, carry):
        k0 = pl.multiple_of(c * ch, ch)
        mask = jnp.where(q0 + i_idx >= k0 + j_idx, 0.0, NEG_INF)
        for hh in range(2):
            rows = slice(hh * tq, (hh + 1) * tq)
            lanes = slice(hh * HEAD_PAD, (hh + 1) * HEAD_PAD)
            s = _dot_nt(q_ref[0, :, lanes], k_ref[0, pl.ds(k0, ch), lanes]) + mask
            m = m_ref[rows, :]
            mn = jnp.maximum(m, jnp.max(s, axis=1, keepdims=True))
            p = jnp.exp(s - jnp.maximum(mn, MASKED_ROW_FLOOR))
            m_ref[rows, :] = mn
            acc_ref[rows, :] = jnp.exp(m - mn) * acc_ref[rows, :] + _dot(p.astype(BF16), v_ref[0, pl.ds(k0, ch), lanes])
        return carry

    lax.fori_loop(0, (q0 + tq + ch - 1) // ch, chunk, 0)
    for hh in range(2):
        acc = acc_ref[hh * tq:(hh + 1) * tq, :]
        o_ref[0, :, hh * HEAD_DIM:(hh + 1) * HEAD_DIM] = acc[:, 0:HEAD_DIM] / jnp.maximum(acc[:, HEAD_DIM:HEAD_DIM + 1], 1e-30)


def _mla_prompt(q, k, v):
    b, t, _ = q.shape
    tq = min(MLA_TQ, t)
    assert tq == MLA_TQ and t % MLA_CHUNK == 0
    pair = 2 * HEAD_PAD
    return pl.pallas_call(
        _mla_prompt_body,
        grid=(b, N_HEADS // 2, t // tq),
        in_specs=[pl.BlockSpec((1, tq, pair), lambda i, h, j: (i, j, h)),
                  pl.BlockSpec((1, t, pair), lambda i, h, j: (i, 0, h)),
                  pl.BlockSpec((1, t, pair), lambda i, h, j: (i, 0, h))],
        out_specs=pl.BlockSpec((1, tq, 2 * HEAD_DIM), lambda i, h, j: (i, j, h)),
        out_shape=jax.ShapeDtypeStruct((b, t, N_HEADS * HEAD_DIM), F32),
        scratch_shapes=[pltpu.VMEM((2 * tq, 1), F32), pltpu.VMEM((2 * tq, HEAD_PAD), F32)],
        compiler_params=_cparams("parallel", "parallel", "arbitrary"),
        name="mla_prompt",
    )(q, k, v)


def _mla_absorb_body(q_ref, wuk_ref, o_ref):
    for h in range(N_HEADS):
        qn = q_ref[:, h * HEAD_PAD:h * HEAD_PAD + HEAD_DIM]
        o_ref[:, h * KV_LORA:(h + 1) * KV_LORA] = _dot_nt(qn, wuk_ref[:, h * HEAD_DIM:(h + 1) * HEAD_DIM])


def _mla_unabsorb_body(o_ref, wuv_ref, y_ref):
    for h in range(N_HEADS):
        y_ref[:, h * HEAD_DIM:(h + 1) * HEAD_DIM] = _dot(
            o_ref[:, h * KV_LORA:(h + 1) * KV_LORA].astype(BF16), wuv_ref[:, h * HEAD_DIM:(h + 1) * HEAD_DIM])


def _whole_call(body, out_shape, name, *args):
    return pl.pallas_call(
        body, grid=(1,),
        in_specs=[_const_spec(a.shape) for a in args],
        out_specs=_const_spec(out_shape.shape),
        out_shape=out_shape, compiler_params=_cparams("arbitrary"), name=name)(*args)


MLA_CHUNK_PAGES = 4


def _mla_sample_body(pt_ref, ql_ref, q_ref, new_ref, pool_ref, o_ref, buf_ref, sem_ref, *, n_pages):
    slot = _paged_prefetch(pool_ref, buf_ref, sem_ref, pt_ref, n_pages)
    ql = ql_ref[0].astype(BF16)
    qr = q_ref[0, :, HEAD_DIM:HEAD_DIM + QK_ROPE]
    new_c = new_ref[0, :, 0:KV_LORA].astype(BF16)
    new_r = new_ref[0, :, KV_LORA:KV_LORA + QK_ROPE].astype(BF16)
    s_new = (jnp.sum(ql.astype(F32) * new_c.astype(F32), axis=1, keepdims=True)
             + jnp.sum(qr.astype(F32) * new_r.astype(F32), axis=1, keepdims=True))

    def body(c, carry):
        m, l, acc = carry
        pages = [buf_ref[slot, c * MLA_CHUNK_PAGES + i] for i in range(MLA_CHUNK_PAGES)]
        ckv = jnp.concatenate([pg[0:KV_LORA, :] for pg in pages], axis=1).astype(BF16)
        kr = jnp.concatenate([pg[KV_LORA:KV_LORA + QK_ROPE, :] for pg in pages], axis=1).astype(BF16)
        s = _dot(ql, ckv) + _dot(qr, kr)
        mn = jnp.maximum(m, jnp.max(s, axis=1, keepdims=True))
        p = jnp.exp(s - mn)
        alpha = jnp.exp(m - mn)
        return mn, alpha * l + jnp.sum(p, axis=1, keepdims=True), alpha * acc + _dot_nt(p.astype(BF16), ckv)

    init = (s_new, jnp.ones((N_HEADS, 1), F32), jnp.broadcast_to(new_c.astype(F32), (N_HEADS, KV_LORA)))
    m, l, acc = lax.fori_loop(0, n_pages // MLA_CHUNK_PAGES, body, init)
    o_ref[0] = acc / l


def _mla_sample(ql, q, lat_new, pool, page_table):
    nb = ql.shape[0]
    n_pages = page_table.shape[1]
    assert n_pages % MLA_CHUNK_PAGES == 0
    feat = pool.shape[1]
    grid_spec = pltpu.PrefetchScalarGridSpec(
        num_scalar_prefetch=1,
        grid=(nb,),
        in_specs=[pl.BlockSpec((1, N_HEADS, KV_LORA), lambda i, pt: (i, 0, 0)),
                  pl.BlockSpec((1, N_HEADS, HEAD_PAD), lambda i, pt: (i, 0, 0)),
                  pl.BlockSpec((1, 1, lat_new.shape[2]), lambda i, pt: (i, 0, 0)),
                  pl.BlockSpec(memory_space=pl.ANY)],
        out_specs=pl.BlockSpec((1, N_HEADS, KV_LORA), lambda i, pt: (i, 0, 0)),
        scratch_shapes=[pltpu.VMEM((2, n_pages, feat, PAGE), F32), pltpu.SemaphoreType.DMA((2,))],
    )
    return pl.pallas_call(
        functools.partial(_mla_sample_body, n_pages=n_pages),
        grid_spec=grid_spec,
        out_shape=jax.ShapeDtypeStruct((nb, N_HEADS, KV_LORA), F32),
        compiler_params=_cparams("arbitrary"),
        name="mla_sample",
    )(page_table, ql, q, lat_new, pool)


def _pad_heads(w, parts):
    out = jnp.zeros((w.shape[0], N_HEADS, HEAD_PAD), w.dtype)
    for src, size, dst in parts:
        out = out.at[:, :, dst:dst + size].set(w[:, :, src:src + size])
    return out.reshape(w.shape[0], N_HEADS * HEAD_PAD)


def _mla_layer(xp, xs, cache, page_table, w_dq, q_norm, kv_norm, w_uq, w_uk, w_uv, past_len):
    b, t, d = xp.shape
    nb = xs.shape[0]
    r0, r1 = HEAD_DIM, HEAD_DIM + ROPE_HALF
    keep = [(0, HEAD_DIM, 0), (HEAD_DIM, ROPE_HALF, r0), (r1, ROPE_HALF, r1)]
    swap = [(r1, ROPE_HALF, r0), (HEAD_DIM, ROPE_HALF, r1)]
    base = Q_LORA + KV_LORA
    x1w, x2w = w_dq[:, base:base + ROPE_HALF], w_dq[:, base + ROPE_HALF:base + QK_ROPE]
    z_lo, z_hi = jnp.zeros((d, HEAD_DIM), F32), jnp.zeros((d, HEAD_PAD - HEAD_DIM - QK_ROPE), F32)
    kr_keep = jnp.concatenate([z_lo, x1w, x2w, z_hi], axis=1)
    kr_swap = jnp.concatenate([z_lo, x2w, x1w, z_hi], axis=1)
    wdq = jnp.concatenate([w_dq[:, :base], kr_keep, kr_swap], axis=1).astype(BF16)
    wq3 = w_uq.reshape(Q_LORA, N_HEADS, HEAD_DIM + QK_ROPE)
    wq = _pad_heads(wq3, keep).astype(BF16)
    wqs = _pad_heads(wq3, swap).astype(BF16)
    wk = _pad_heads(w_uk, [(0, HEAD_DIM, 0)]).astype(BF16)
    wv = w_uv.reshape(KV_LORA, N_HEADS * HEAD_DIM).astype(BF16)
    wv_pad = _pad_heads(w_uv, [(0, HEAD_DIM, 0)]).astype(BF16)
    vone = jnp.asarray((np.arange(N_HEADS * HEAD_PAD) % HEAD_PAD == HEAD_DIM).astype(np.float32)).reshape(1, -1)
    consts = [wdq, q_norm.reshape(1, -1), kv_norm.reshape(1, -1), wq, wqs, wk, wv_pad, vone]

    def rope_tables(pos):
        inv = ROPE_THETA ** (-jnp.arange(ROPE_HALF, dtype=F32) / ROPE_HALF)
        ang = pos.astype(F32)[:, None] * inv[None, :]
        cos, sin = jnp.cos(ang), jnp.sin(ang)
        n = pos.shape[0]
        c = jnp.concatenate([jnp.ones((n, HEAD_DIM), F32), cos, cos, jnp.zeros((n, HEAD_PAD - r1 - ROPE_HALF), F32)], axis=1)
        s = jnp.concatenate([jnp.zeros((n, HEAD_DIM), F32), -sin, sin, jnp.zeros((n, HEAD_PAD - r1 - ROPE_HALF), F32)], axis=1)
        return c, s

    cos_p, sin_p = rope_tables(jnp.arange(t))
    q, k, v, _, lat_p_t = _mla_project(xp.reshape(b * t, d), cos_p, sin_p, consts, b)
    wide = N_HEADS * HEAD_PAD
    attn_p = _mla_prompt(q.reshape(b, t, wide), k.reshape(b, t, wide), v.reshape(b, t, wide))
    cos_s, sin_s = rope_tables(jnp.full((nb,), past_len, jnp.int32))
    qs, _, _, lat_s, lat_s_t = _mla_project(xs.reshape(nb, d), cos_s, sin_s, consts, 1)
    wuk2 = w_uk.reshape(KV_LORA, N_HEADS * HEAD_DIM).astype(BF16)
    ql = _whole_call(_mla_absorb_body, jax.ShapeDtypeStruct((nb, N_HEADS * KV_LORA), F32), "mla_absorb", qs, wuk2)
    o_lat = _mla_sample(ql.reshape(nb, N_HEADS, KV_LORA), qs.reshape(nb, N_HEADS, HEAD_PAD),
                        lat_s.reshape(nb, 1, -1), _feature_major_pool(cache), page_table)
    attn_s = _whole_call(_mla_unabsorb_body, jax.ShapeDtypeStruct((nb, d), F32), "mla_unabsorb",
                         o_lat.reshape(nb, N_HEADS * KV_LORA), wv)
    return (attn_p.reshape(b * t, d), attn_s, jnp.transpose(lat_p_t, (0, 2, 1)),
            jnp.transpose(lat_s_t, (2, 0, 1)))


CMP_HIDDEN = 2 * HEAD_DIM
KV_PAIR = 2 * HEAD_DIM


def _nsa_pe_body(pe_ref, w1_ref, o_ref):
    for c in range(2):
        o_ref[:, c * CMP_HIDDEN:(c + 1) * CMP_HIDDEN] = _dot(pe_ref[c], w1_ref[c])


def _compress(load_rows, n_h, wblk_ref, peh_ref, w2k_ref, w2v_ref):
    hid = jnp.zeros((n_h, 4 * CMP_HIDDEN), F32)
    for s in range(CMP_STRIDE):
        hid = hid + _dot(load_rows(s).astype(BF16), wblk_ref[s])
    up = lambda x: pltpu.roll(x, n_h - 1, 0)
    peh = peh_ref[0:1, :]
    hk = peh[:, 0:CMP_HIDDEN] + hid[:, 0:CMP_HIDDEN] + up(hid[:, CMP_HIDDEN:2 * CMP_HIDDEN])
    hv = peh[:, CMP_HIDDEN:] + hid[:, 2 * CMP_HIDDEN:3 * CMP_HIDDEN] + up(hid[:, 3 * CMP_HIDDEN:])
    return _dot(jax.nn.gelu(hk).astype(BF16), w2k_ref[...]) + _dot(jax.nn.gelu(hv).astype(BF16), w2v_ref[...])


def _nsa_compress_body(cmp_ref, wblk_ref, peh_ref, w2k_ref, w2v_ref, o_ref):
    n_h = o_ref.shape[1]
    o_ref[0] = _compress(lambda s: cmp_ref[0, pl.ds(s, n_h, stride=CMP_STRIDE), :], n_h,
                         wblk_ref, peh_ref, w2k_ref, w2v_ref)


def _nsa_compress_prompt(cmp, consts):
    b, t, _ = cmp.shape
    n_h = t // CMP_STRIDE
    return pl.pallas_call(
        _nsa_compress_body,
        grid=(b,),
        in_specs=[pl.BlockSpec((1, t, KV_PAIR), lambda i: (i, 0, 0))] + [_const_spec(c.shape) for c in consts],
        out_specs=pl.BlockSpec((1, n_h, KV_PAIR), lambda i: (i, 0, 0)),
        out_shape=jax.ShapeDtypeStruct((b, n_h, KV_PAIR), F32),
        compiler_params=_cparams("parallel"),
        name="nsa_compress",
    )(cmp, *consts)


def _intersect_matrix(n_c, n_s):
    n = lax.broadcasted_iota(jnp.int32, (n_c, n_s), 0) * CMP_STRIDE
    j = lax.broadcasted_iota(jnp.int32, (n_c, n_s), 1) * SLC_BLOCK
    return ((n < j + SLC_BLOCK) & (n + CMP_BLOCK > j)).astype(BF16)


def _split_dot(x, w):
    hi = x.astype(BF16)
    lo = (x - hi.astype(F32)).astype(BF16)
    return _dot(hi, w) + _dot(lo, w)


def _nsa_cmp_bias_body(base_ref, o_ref):
    n_c = o_ref.shape[1]
    for c in range(CMP_STRIDE):
        o_ref[0, :, c * n_c:(c + 1) * n_c] = _toeplitz(base_ref[0, c:c + 1, :], n_c, 1)[:, 0:n_c]


def _nsa_cmp_bias(rel_bias, t):
    n_c = t // CMP_STRIDE
    u = np.arange(2 * n_c)[None, :]
    c = np.arange(CMP_STRIDE)[:, None]
    dist = np.where(u >= n_c, CMP_STRIDE * (2 * n_c - 1 - u) + c - (CMP_BLOCK - 1), -1)
    base = _bias_by_distance(rel_bias, dist)
    out = pl.pallas_call(
        _nsa_cmp_bias_body,
        grid=(N_HEADS,),
        in_specs=[pl.BlockSpec((1, CMP_STRIDE, 2 * n_c), lambda h: (h, 0, 0))],
        out_specs=pl.BlockSpec((1, n_c, CMP_STRIDE * n_c), lambda h: (h, 0, 0)),
        out_shape=jax.ShapeDtypeStruct((N_HEADS, n_c, CMP_STRIDE * n_c), F32),
        compiler_params=_cparams("parallel"),
        name="nsa_cmp_bias",
    )(base)
    return out.reshape(N_HEADS, t, n_c)


NSA_PCHUNK = 512
NSA_WKEYS = NSA_WINDOW + Q_TILE


def _value_ones(kv):
    lane = lax.broadcasted_iota(jnp.int32, kv.shape, 1)
    rolled = pltpu.roll(kv, HEAD_DIM, 1)
    return jnp.where(lane < HEAD_DIM, rolled, jnp.where(lane == HEAD_DIM, 1.0, 0.0)).astype(BF16)


def _value_ones_t(v_t):
    first = lax.broadcasted_iota(jnp.int32, v_t.shape, 0) == 0
    return jnp.concatenate([v_t, jnp.where(first, 1.0, 0.0)], axis=0).astype(BF16)


def _nsa_prompt_body(q_ref, gate_ref, slc_ref, win_ref, kvc_ref, biasc_ref, rev_ref, o_ref,
                     qst_ref, oc_ref, base_ref, ms_ref, accs_ref, *, n_s):
    qi = pl.program_id(1)
    nq = pl.num_programs(1)
    tq = Q_TILE
    n_c = kvc_ref.shape[1]
    q0 = qi * tq
    for h in range(N_HEADS):
        qst_ref[h * tq:(h + 1) * tq, :] = (q_ref[0, :, h * HEAD_DIM:(h + 1) * HEAD_DIM] * (HEAD_DIM ** -0.5)).astype(BF16)

    kvc = kvc_ref[0]
    kcb = kvc[:, 0:HEAD_DIM].astype(BF16)
    vc = _value_ones(kvc)
    qpos_c = q0 + lax.broadcasted_iota(jnp.int32, (tq, n_c), 0)
    cend = lax.broadcasted_iota(jnp.int32, (tq, n_c), 1) * CMP_STRIDE + CMP_BLOCK
    mask_c = jnp.where(cend <= qpos_c + 1, 0.0, NEG_INF)
    psum = jnp.zeros((tq, n_c), F32)
    for h in range(N_HEADS):
        rows = slice(h * tq, (h + 1) * tq)
        s = _dot_nt(qst_ref[rows, :], kcb) + biasc_ref[h] + mask_c
        m = jnp.maximum(jnp.max(s, axis=1, keepdims=True), MASKED_ROW_FLOOR)
        e = jnp.exp(s - m)
        p = e / jnp.maximum(jnp.sum(e, axis=1, keepdims=True), 1e-30)
        oc_ref[rows, :] = _dot(p.astype(BF16), vc)[:, 0:HEAD_DIM]
        psum = psum + p
    imp = _split_dot(psum, _intersect_matrix(n_c, n_s))

    qblk = (q0 + lax.broadcasted_iota(jnp.int32, (tq, n_s), 0)) >> 6
    jb = lax.broadcasted_iota(jnp.int32, (tq, n_s), 1)
    forced = (jb == 0) | (jb == qblk) | (jb == qblk - 1)
    score = jnp.where(jb <= qblk, imp + jnp.where(forced, FORCE_BONUS, 0.0), NEG_INF)
    rank = jnp.zeros((tq, n_s), F32)
    for j in range(n_s):
        col = score[:, j:j + 1]
        rank = rank + ((col > score) | ((col == score) & (jb > j))).astype(F32)
    sel = (rank < N_SELECT).astype(BF16)

    def load_base(blk0, n_blk):
        for t in range(n_blk):
            base_ref[:, t * tq:(t + 1) * tq] = rev_ref[blk0 + t]

    def head_bias(h, n_keys):
        width = n_keys + tq
        return _toeplitz(base_ref[h:h + 1, 0:width], tq, width - (tq - 1))[:, 0:n_keys]

    ch = NSA_PCHUNK
    ms_ref[...] = jnp.full(ms_ref.shape, NEG_INF, F32)
    accs_ref[...] = jnp.zeros(accs_ref.shape, F32)
    i_idx = lax.broadcasted_iota(jnp.int32, (tq, ch), 0)
    j_idx = lax.broadcasted_iota(jnp.int32, (tq, ch), 1)
    e_row = lax.broadcasted_iota(jnp.int32, (n_s, ch), 0)
    e_lane = lax.broadcasted_iota(jnp.int32, (n_s, ch), 1)

    def slc_chunk(c, carry):
        k0 = c * ch
        expand = (((k0 + e_lane) >> 6) == e_row).astype(BF16)
        valid = (_dot(sel, expand) > 0.5) & (q0 + i_idx >= k0 + j_idx)
        mask = jnp.where(valid, 0.0, NEG_INF)

        @pl.when(jnp.max(mask) > -1.0)
        def _():
            kv = slc_ref[0, :, pl.ds(pl.multiple_of(k0, ch), ch)]
            kb = kv[0:HEAD_DIM, :].astype(BF16)
            va = _value_ones_t(kv[HEAD_DIM:KV_PAIR, :])
            load_base(nq - 1 - qi + c * (ch // tq), ch // tq + 1)
            for h in range(N_HEADS):
                rows = slice(h * tq, (h + 1) * tq)
                s = _dot(qst_ref[rows, :], kb) + head_bias(h, ch) + mask
                m = ms_ref[rows, :]
                mn = jnp.maximum(m, jnp.max(s, axis=1, keepdims=True))
                p = jnp.exp(s - jnp.maximum(mn, MASKED_ROW_FLOOR))
                ms_ref[rows, :] = mn
                accs_ref[rows, :] = jnp.exp(m - mn) * accs_ref[rows, :] + _dot_nt(p.astype(BF16), va)
        return carry

    lax.fori_loop(0, (q0 + tq + ch - 1) // ch, slc_chunk, 0)

    wk = NSA_WKEYS
    k0w = jnp.maximum(q0 - NSA_WINDOW, 0)
    dist = (q0 - k0w) + lax.broadcasted_iota(jnp.int32, (tq, wk), 0) - lax.broadcasted_iota(jnp.int32, (tq, wk), 1)
    mask_w = jnp.where((dist >= 0) & (dist <= NSA_WINDOW), 0.0, NEG_INF)
    kvw = win_ref[0, :, pl.ds(pl.multiple_of(k0w, tq), wk)]
    kwb = kvw[0:HEAD_DIM, :].astype(BF16)
    vwa = _value_ones_t(kvw[HEAD_DIM:KV_PAIR, :])
    load_base(nq - 1 - jnp.minimum(qi, NSA_WINDOW // tq), wk // tq + 1)
    gates = jax.nn.sigmoid(gate_ref[0])
    for h in range(N_HEADS):
        rows = slice(h * tq, (h + 1) * tq)
        s = _dot(qst_ref[rows, :], kwb) + head_bias(h, wk) + mask_w
        m = jnp.maximum(jnp.max(s, axis=1, keepdims=True), MASKED_ROW_FLOOR)
        acc_w = _dot_nt(jnp.exp(s - m).astype(BF16), vwa)
        o_w = acc_w[:, 0:HEAD_DIM] / jnp.maximum(acc_w[:, HEAD_DIM:HEAD_DIM + 1], 1e-30)
        acc_s = accs_ref[rows, :]
        o_s = acc_s[:, 0:HEAD_DIM] / jnp.maximum(acc_s[:, HEAD_DIM:HEAD_DIM + 1], 1e-30)
        o_ref[0, :, h * HEAD_DIM:(h + 1) * HEAD_DIM] = (
            gates[:, h:h + 1] * oc_ref[rows, :] + gates[:, N_HEADS + h:N_HEADS + h + 1] * o_s
            + gates[:, 2 * N_HEADS + h:2 * N_HEADS + h + 1] * o_w)


def _nsa_prompt(q, gate, kv_t, win_t, kvc, bias_c, rev):
    b, t, d = q.shape
    tq = Q_TILE
    n_c = kvc.shape[1]
    rows = N_HEADS * tq
    assert t % NSA_PCHUNK == 0 and t >= NSA_WKEYS
    return pl.pallas_call(
        functools.partial(_nsa_prompt_body, n_s=t // SLC_BLOCK),
        grid=(b, t // tq),
        in_specs=[pl.BlockSpec((1, tq, d), lambda i, j: (i, j, 0)),
                  pl.BlockSpec((1, tq, gate.shape[2]), lambda i, j: (i, j, 0)),
                  pl.BlockSpec((1, KV_PAIR, t), lambda i, j: (i, 1, 0)),
                  pl.BlockSpec((1, KV_PAIR, t), lambda i, j: (i, 0, 0)),
                  pl.BlockSpec((1, n_c, KV_PAIR), lambda i, j: (i, 0, 0)),
                  pl.BlockSpec((N_HEADS, tq, n_c), lambda i, j: (0, j, 0)),
                  _const_spec(rev.shape)],
        out_specs=pl.BlockSpec((1, tq, d), lambda i, j: (i, j, 0)),
        out_shape=jax.ShapeDtypeStruct((b, t, d), F32),
        scratch_shapes=[pltpu.VMEM((rows, HEAD_DIM), BF16), pltpu.VMEM((rows, HEAD_DIM), F32),
                        pltpu.VMEM((N_HEADS, NSA_WKEYS + tq), F32),
                        pltpu.VMEM((rows, 1), F32), pltpu.VMEM((rows, KV_PAIR), F32)],
        compiler_params=_cparams("parallel", "arbitrary"),
        name="nsa_prompt",
    )(q, gate, kv_t, win_t, kvc, bias_c, rev)


NSA_CHUNK = 512


def _nsa_sample_body(pt_ref, q_ref, gate_ref, new_ref, wnew_ref, cwin_ref, bc_ref, bs_ref, bw_ref, b0_ref,
                     wblk_ref, peh_ref, w2k_ref, w2v_ref, pool_ref, o_ref, wout_ref, buf_ref, sem_ref, cmp_ref,
                     *, n_pages, n_sp):
    slot = _paged_prefetch(pool_ref, buf_ref, sem_ref, pt_ref, n_pages)
    b = pl.program_id(0)
    past = n_pages * PAGE
    n_h = past // CMP_STRIDE
    q = (q_ref[0] * (HEAD_DIM ** -0.5)).astype(BF16)
    qf = q.astype(F32)

    def to_rows(p, carry):
        cmp_ref[pl.ds(pl.multiple_of(p * PAGE, PAGE), PAGE), :] = buf_ref[slot, p, 0:KV_PAIR, :].T
        return carry
    lax.fori_loop(0, n_pages, to_rows, 0)
    kvc = _compress(lambda s: cmp_ref[pl.ds(s, n_h, stride=CMP_STRIDE), :], n_h, wblk_ref, peh_ref, w2k_ref, w2v_ref)
    kc = kvc[:, 0:HEAD_DIM].astype(BF16)
    vc = kvc[:, HEAD_DIM:KV_PAIR].astype(BF16)
    cend = lax.broadcasted_iota(jnp.int32, (N_HEADS, n_h), 1) * CMP_STRIDE + CMP_BLOCK
    valid_c = cend <= past + 1
    s = jnp.where(valid_c, _dot_nt(q, kc) + bc_ref[...], NEG_INF)
    m = jnp.max(s, axis=1, keepdims=True)
    e = jnp.where(valid_c, jnp.exp(s - m), 0.0)
    p_c = e / jnp.maximum(jnp.sum(e, axis=1, keepdims=True), 1e-30)
    o_c = _dot(p_c.astype(BF16), vc)

    psum = jnp.broadcast_to(jnp.sum(p_c, axis=0, keepdims=True), (8, n_h))
    imp = _split_dot(psum, _intersect_matrix(n_h, n_sp))[0:1, :]
    qblk = past // SLC_BLOCK
    jb = lax.broadcasted_iota(jnp.int32, (1, n_sp), 1)
    forced = (jb == 0) | (jb == qblk) | (jb == qblk - 1)
    score = jnp.where(jb <= qblk, imp + jnp.where(forced, FORCE_BONUS, 0.0), NEG_INF)
    r_idx = lax.broadcasted_iota(jnp.int32, (n_sp, n_sp), 0)
    c_idx = lax.broadcasted_iota(jnp.int32, (n_sp, n_sp), 1)
    score_b = jnp.broadcast_to(score, (n_sp, n_sp))
    score_col = jnp.sum(jnp.where(r_idx == c_idx, score_b, 0.0), axis=1, keepdims=True)
    beats = (score_col > score_b) | ((score_col == score_b) & (r_idx < c_idx))
    rank = jnp.sum(beats.astype(F32), axis=0, keepdims=True)
    sel = jnp.broadcast_to((rank < N_SELECT).astype(BF16), (8, n_sp))
    sel_new = rank[:, qblk:qblk + 1] < N_SELECT

    def fold_new(kv_new, valid):
        k_new = kv_new[:, 0:HEAD_DIM].astype(BF16).astype(F32)
        v_new = kv_new[:, HEAD_DIM:KV_PAIR].astype(BF16).astype(F32)
        s_new = jnp.sum(qf * k_new, axis=1, keepdims=True) + b0_ref[:, 0:1]
        ok = jnp.broadcast_to(valid, (N_HEADS, 1))
        return (jnp.where(ok, s_new, NEG_INF), jnp.where(ok, 1.0, 0.0),
                jnp.where(ok, jnp.broadcast_to(v_new, (N_HEADS, HEAD_DIM)), 0.0))

    def update(carry, s, valid, v_t):
        m, l, acc = carry
        mn = jnp.maximum(m, jnp.max(s, axis=1, keepdims=True))
        p = jnp.where(valid, jnp.exp(s - mn), 0.0)
        alpha = jnp.exp(m - mn)
        return mn, alpha * l + jnp.sum(p, axis=1, keepdims=True), alpha * acc + _dot_nt(p.astype(BF16), v_t)

    carry = fold_new(new_ref[0, 0:1, :], sel_new)
    pages_per_chunk = NSA_CHUNK // PAGE
    e_row = lax.broadcasted_iota(jnp.int32, (n_sp, NSA_CHUNK), 0)
    e_lane = lax.broadcasted_iota(jnp.int32, (n_sp, NSA_CHUNK), 1)
    for c in range(past // NSA_CHUNK):
        pages = [buf_ref[slot, c * pages_per_chunk + i, KV_PAIR:2 * KV_PAIR, :] for i in range(pages_per_chunk)]
        k_t = jnp.concatenate([pg[0:HEAD_DIM, :] for pg in pages], axis=1).astype(BF16)
        v_t = jnp.concatenate([pg[HEAD_DIM:KV_PAIR, :] for pg in pages], axis=1).astype(BF16)
        expand = (((c * NSA_CHUNK + e_lane) >> 6) == e_row).astype(BF16)
        valid = jnp.broadcast_to(_dot(sel, expand)[0:1, :] > 0.5, (N_HEADS, NSA_CHUNK))
        s = jnp.where(valid, _dot(q, k_t) + bs_ref[:, c * NSA_CHUNK:(c + 1) * NSA_CHUNK], NEG_INF)
        carry = update(carry, s, valid, v_t)
    o_s = carry[2] / jnp.maximum(carry[1], 1e-30)

    carry = fold_new(new_ref[0, 1:2, :], jnp.full((1, 1), True))
    cwin = cwin_ref[0]
    s = _dot(q, cwin[0:HEAD_DIM, :].astype(BF16)) + bw_ref[...]
    carry = update(carry, s, jnp.full(s.shape, True), cwin[HEAD_DIM:KV_PAIR, :].astype(BF16))
    o_w = carry[2] / jnp.maximum(carry[1], 1e-30)

    gates = jax.nn.sigmoid(gate_ref[0])
    o_ref[0] = gates[:, 0:1] * o_c + gates[:, 1:2] * o_s + gates[:, 2:3] * o_w

    wb = cwin.shape[1]
    pick = lax.broadcasted_iota(jnp.int32, wnew_ref.shape, 1) == b
    new_col = jnp.sum(jnp.where(pick, wnew_ref[...], 0.0), axis=1, keepdims=True)
    last = lax.broadcasted_iota(jnp.int32, cwin.shape, 1) == wb - 1
    wout_ref[0] = jnp.where(last, new_col, pltpu.roll(cwin, wb - 1, 1))


def _nsa_sample(q, gate_t, new, wnew_t, cwin_t, biases, consts, pool, page_table):
    nb = q.shape[0]
    n_pages = page_table.shape[1]
    past = n_pages * PAGE
    assert past % NSA_CHUNK == 0 and cwin_t.shape[2] <= NSA_WINDOW
    n_s = past // SLC_BLOCK + 1
    n_sp = -(-n_s // 128) * 128
    per_b = lambda a: pl.BlockSpec((1,) + a.shape[1:], lambda i, pt: (i,) + (0,) * (a.ndim - 1))
    const = lambda a: pl.BlockSpec(a.shape, lambda i, pt: (0,) * a.ndim)
    grid_spec = pltpu.PrefetchScalarGridSpec(
        num_scalar_prefetch=1,
        grid=(nb,),
        in_specs=[per_b(q), per_b(gate_t), per_b(new), const(wnew_t), per_b(cwin_t)]
        + [const(a) for a in biases] + [const(a) for a in consts]
        + [pl.BlockSpec(memory_space=pl.ANY)],
        out_specs=[pl.BlockSpec((1, N_HEADS, HEAD_DIM), lambda i, pt: (i, 0, 0)), per_b(cwin_t)],
        scratch_shapes=[pltpu.VMEM((2, n_pages, pool.shape[1], PAGE), F32), pltpu.SemaphoreType.DMA((2,)),
                        pltpu.VMEM((past, KV_PAIR), F32)],
    )
    return pl.pallas_call(
        functools.partial(_nsa_sample_body, n_pages=n_pages, n_sp=n_sp),
        grid_spec=grid_spec,
        out_shape=[jax.ShapeDtypeStruct((nb, N_HEADS, HEAD_DIM), F32), jax.ShapeDtypeStruct(cwin_t.shape, F32)],
        compiler_params=_cparams("arbitrary"),
        name="nsa_sample",
    )(page_table, q, gate_t, new, wnew_t, cwin_t, *biases, *consts, pool)


def _nsa_layer(xp, xs, cache_kv, cache_win, page_table, w_in, pe, w1, w2, rel_bias):
    b, t, d = xp.shape
    nb = xs.shape[0]
    past = page_table.shape[1] * PAGE
    w_q, w_gate = w_in[:, :d].astype(BF16), w_in[:, d + 3 * KV_PAIR:].astype(BF16)
    w_cmp, w_slc, w_win = [w_in[:, d + i * KV_PAIR:d + (i + 1) * KV_PAIR].astype(BF16) for i in range(3)]
    wts = [w_in[:, d:d + 2 * KV_PAIR].T.astype(BF16), w_in[:, d + 2 * KV_PAIR:d + 3 * KV_PAIR].T.astype(BF16)]

    w1r = w1.reshape(2, 2, CMP_STRIDE, HEAD_DIM, CMP_HIDDEN)
    zero = jnp.zeros((CMP_STRIDE, HEAD_DIM, CMP_HIDDEN), F32)
    top = jnp.concatenate([w1r[0, 0], w1r[0, 1], zero, zero], axis=2)
    bot = jnp.concatenate([zero, zero, w1r[1, 0], w1r[1, 1]], axis=2)
    wblk = jnp.concatenate([top, bot], axis=1).astype(BF16)
    pe8 = jnp.broadcast_to(pe.reshape(2, 1, -1), (2, 8, CMP_BLOCK * HEAD_DIM)).astype(BF16)
    peh = _whole_call(_nsa_pe_body, jax.ShapeDtypeStruct((8, 2 * CMP_HIDDEN), F32), "nsa_pe", pe8, w1.astype(BF16))
    zpad = jnp.zeros((CMP_HIDDEN, HEAD_DIM), F32)
    w2k = jnp.concatenate([w2[0], zpad], axis=1).astype(BF16)
    w2v = jnp.concatenate([zpad, w2[1]], axis=1).astype(BF16)
    consts = [wblk, peh, w2k, w2v]

    qp, cmp_p, gate_p, kv_t, win_t = _proj(xp.reshape(b * t, d), [w_q, w_cmp, w_gate], wts, b)
    r3 = lambda a: a.reshape(b, t, -1)
    kvc = _nsa_compress_prompt(r3(cmp_p), consts)
    nq = t // Q_TILE
    bias_c = _nsa_cmp_bias(rel_bias, t)
    n_rev = nq + NSA_WKEYS // Q_TILE
    rev = _bias_by_distance(rel_bias, Q_TILE * nq - 1 - np.arange(n_rev * Q_TILE))
    rev = jnp.transpose(rev.reshape(N_HEADS, n_rev, Q_TILE), (1, 0, 2))
    attn_p = _nsa_prompt(r3(qp), r3(gate_p), kv_t, win_t, kvc, bias_c, rev)
    kv_p = _token_major(kv_t, (4, 1, HEAD_DIM))
    win_out_p = _token_major(win_t[:, :, t - min(NSA_WINDOW, t):], (2, 1, HEAD_DIM))

    qs, slc_s, win_s, gate_s, kvs_t, wins_t = _proj(xs.reshape(nb, d), [w_q, w_slc, w_win, w_gate], wts, 1)
    wb = cache_win.shape[1]
    n_h = past // CMP_STRIDE
    bc = _bias_by_distance(rel_bias, past - (np.arange(n_h) * CMP_STRIDE + CMP_BLOCK - 1))
    bs = _bias_by_distance(rel_bias, past - np.arange(past))
    bw = _bias_by_distance(rel_bias, wb - np.arange(wb))
    b0 = _bias_by_distance(rel_bias, np.zeros((128,), np.int64))
    gate_t = jnp.transpose(gate_s.reshape(nb, 3, N_HEADS), (0, 2, 1))
    new = jnp.stack([slc_s, win_s], axis=1)
    cwin_t = jnp.transpose(cache_win, (0, 2, 3, 4, 1)).reshape(nb, KV_PAIR, wb)
    attn_s, wout_t = _nsa_sample(qs.reshape(nb, N_HEADS, HEAD_DIM), gate_t, new, wins_t[0], cwin_t,
                                 [bc, bs, bw, b0], consts, _feature_major_pool(cache_kv), page_table)
    kv_s = jnp.transpose(kvs_t[0], (1, 0)).reshape(nb, 1, 4, 1, HEAD_DIM)
    win_out_s = _token_major(wout_t, (2, 1, HEAD_DIM))
    return attn_p.reshape(b * t, d), attn_s.reshape(nb, d), kv_p, kv_s, win_out_p, win_out_s


def kernel(x_prompt, x_sample, cache_nsa_kv, cache_nsa_win, cache_mla, state_dil_w128, state_dil_w512,
           state_dil_w2048, cache_sb_kv, page_table, p_prompt, p_sample, rel_bias, ln1_g, ln1_b, ln2_g, ln2_b,
           ffn_wg, ffn_wu, ffn_wd, ple_wg, ple_wp, nsa_w_in, nsa_cmp_pe, nsa_cmp_w1, nsa_cmp_w2, nsa_w_out,
           mla_w_dq, mla_q_norm, mla_kv_norm, mla_w_uq, mla_w_uk, mla_w_uv, mla_w_out, dil_w_in, dil_w_out,
           sb_w_in, sb_w_out):
    b, t, d = x_prompt.shape
    nb = x_sample.shape[0]
    past_len = page_table.shape[1] * PAGE
    depth = p_prompt.shape[0]
    n_mixers = 4
    dil_states = (state_dil_w128, state_dil_w512, state_dil_w2048)
    xp = x_prompt.reshape(b * t, d)
    xs = x_sample.reshape(nb, d)
    pp_all = p_prompt.reshape(depth, b * t, -1)
    ps_all = p_sample.reshape(depth, nb, -1)
    outs = {k: [] for k in ("nsa_kv_p", "nsa_kv_s", "nsa_win_p", "nsa_win_s", "mla_p", "mla_s", "sb_p", "sb_s")}
    dil_p = [[] for _ in DIL_PATTERNS]
    dil_s = [[] for _ in DIL_PATTERNS]
    for i in range(depth):
        kind, j = i % n_mixers, i // n_mixers
        xp3, xs3 = xp.reshape(b, t, d), xs.reshape(nb, 1, d)
        if kind == 0:
            mp, ms, a, b_, c, e = _nsa_layer(xp3, xs3, cache_nsa_kv[j], cache_nsa_win[j], page_table, nsa_w_in[j],
                                            nsa_cmp_pe[j], nsa_cmp_w1[j], nsa_cmp_w2[j], rel_bias)
            mp, ms, w_out = [mp], [ms], nsa_w_out[j]
            outs["nsa_kv_p"].append(a)
            outs["nsa_kv_s"].append(b_)
            outs["nsa_win_p"].append(c)
            outs["nsa_win_s"].append(e)
        elif kind == 1:
            mp, ms, a, b_ = _mla_layer(xp3, xs3, cache_mla[j], page_table, mla_w_dq[j], mla_q_norm[j], mla_kv_norm[j],
                                       mla_w_uq[j], mla_w_uk[j], mla_w_uv[j], past_len)
            mp, ms, w_out = [mp], [ms], mla_w_out[j]
            outs["mla_p"].append(a)
            outs["mla_s"].append(b_)
        elif kind == 2:
            mp, ms, st_p, st_s = _dil_layer(xp3, xs3, [s[j] for s in dil_states], dil_w_in[j], rel_bias)
            ms, w_out = [ms], dil_w_out[j]
            for g in range(len(DIL_PATTERNS)):
                dil_p[g].append(st_p[g])
                dil_s[g].append(st_s[g])
        else:
            mp, ms, a, b_ = _sb_layer(xp3, xs3, cache_sb_kv[j], page_table, sb_w_in[j])
            mp, ms, w_out = [mp], [ms], sb_w_out[j]
            outs["sb_p"].append(a)
            outs["sb_s"].append(b_)
        row = lambda v: v.reshape(1, -1)
        consts = (w_out.astype(BF16), row(ln1_g[i]), row(ln1_b[i]), row(ln2_g[i]), row(ln2_b[i]),
                  ffn_wg[i].astype(BF16), ffn_wu[i].astype(BF16), ffn_wd[i].astype(BF16),
                  ple_wg[i].astype(BF16), ple_wp[i].astype(BF16))
        xp = _tail(mp, xp, pp_all, i, *consts)
        xs = _tail(ms, xs, ps_all, i, *consts)
    st = jnp.stack
    return (xp.reshape(b, t, d), xs.reshape(nb, 1, d),
            st(outs["nsa_kv_p"]), st(outs["nsa_kv_s"]), st(outs["nsa_win_p"]), st(outs["nsa_win_s"]),
            st(outs["mla_p"]), st(outs["mla_s"]),
            st(dil_p[0]), st(dil_s[0]), st(dil_p[1]), st(dil_s[1]), st(dil_p[2]), st(dil_s[2]),
            st(outs["sb_p"]), st(outs["sb_s"]))
```

```python
import functools
import math

import numpy as np
import jax
import jax.numpy as jnp
from jax import lax
from jax.experimental import pallas as pl
from jax.experimental.pallas import tpu as pltpu

F32 = jnp.float32
BF16 = jnp.bfloat16

HEAD_DIM = 64
N_HEADS = 16
PAGE = 128
Q_TILE = 128
LN_EPS = 1e-5
NEG_INF = -1e30
DEPTH = 4
ALPHA = (2 * DEPTH) ** 0.25
N_BUCKETS = 32
MAX_DISTANCE = 2048
CMP_BLOCK = 32
CMP_STRIDE = 16
SLC_BLOCK = 64
N_SELECT = 16
NSA_WINDOW = 512
FORCE_BONUS = 1e4
DIL_PATTERNS = ((128, 1), (512, 4), (2048, 16))
ROPE_THETA = 10000.0
QK_ROPE = 32
KV_LORA = 256
Q_LORA = 256
SB_DEAD = -104.0
MASKED_ROW_FLOOR = -1e29
VMEM_LIMIT_BYTES = 56 * 1024 * 1024


def _cparams(*sem):
    return pltpu.CompilerParams(dimension_semantics=sem, vmem_limit_bytes=VMEM_LIMIT_BYTES)


def _dot(a, b):
    return jnp.dot(a, b, preferred_element_type=F32)


def _dot_nt(a, b):
    return lax.dot_general(a, b, (((1,), (1,)), ((), ())), preferred_element_type=F32)


def _const_spec(shape):
    nd = len(shape)
    return pl.BlockSpec(shape, lambda *_: (0,) * nd)


def _bucket_of_distance(n_dist):
    n = np.arange(n_dist, dtype=np.int64)
    max_exact = N_BUCKETS // 2
    ratio = np.maximum(n, max_exact).astype(np.float32) / np.float32(max_exact)
    log_ratio = np.log(ratio).astype(np.float32) / np.float32(math.log(MAX_DISTANCE / max_exact))
    large = np.minimum(max_exact + (log_ratio * np.float32(N_BUCKETS - max_exact)).astype(np.int32), N_BUCKETS - 1)
    return np.where(n < max_exact, n, large).astype(np.int32)


def _bias_by_distance(rel_bias, dists):
    d = np.asarray(dists)
    bucket = _bucket_of_distance(int(d.max()) + 1)[np.maximum(d, 0)]
    vals = jnp.moveaxis(rel_bias.astype(F32)[bucket], -1, 0)
    return jnp.where(jnp.asarray(d >= 0), vals, 0.0)


def _proj_body(x_ref, *refs, n_row, n_col):
    xb = x_ref[...].astype(BF16)
    n = n_row + n_col
    for w_ref, o_ref in zip(refs[:n_row], refs[n:n + n_row]):
        o_ref[...] = _dot(xb, w_ref[...])
    for w_ref, o_ref in zip(refs[n_row:n], refs[n + n_row:]):
        o_ref[0] = _dot_nt(w_ref[...], xb)


def _proj(x, ws, wts, n_batch):
    m, k = x.shape
    tm = _row_tile(m)
    t = m // n_batch
    per_b = t // tm
    return pl.pallas_call(
        functools.partial(_proj_body, n_row=len(ws), n_col=len(wts)),
        grid=(m // tm,),
        in_specs=[pl.BlockSpec((tm, k), lambda i: (i, 0))] + [_const_spec(w.shape) for w in ws + wts],
        out_specs=[pl.BlockSpec((tm, w.shape[1]), lambda i: (i, 0)) for w in ws]
        + [pl.BlockSpec((1, w.shape[0], tm), lambda i: (i // per_b, 0, i % per_b)) for w in wts],
        out_shape=[jax.ShapeDtypeStruct((m, w.shape[1]), F32) for w in ws]
        + [jax.ShapeDtypeStruct((n_batch, w.shape[0], t), F32) for w in wts],
        compiler_params=_cparams("parallel"),
        name="proj",
    )(x, *ws, *wts)


def _row_tile(m):
    return 256 if m % 256 == 0 else m


def _layer_norm(x, g, b):
    mu = jnp.mean(x, axis=-1, keepdims=True)
    xc = x - mu
    var = jnp.mean(xc * xc, axis=-1, keepdims=True)
    return xc * lax.rsqrt(var + LN_EPS) * g + b


FF_CHUNK = 256


def _tail_body(*refs, n_mix):
    mix_refs = refs[:n_mix]
    (x_ref, p_ref, wo_ref, g1_ref, b1_ref, g2_ref, b2_ref, wg_ref, wu_ref, wd_ref,
     pwg_ref, pwp_ref, o_ref) = refs[n_mix:]
    if n_mix == 1:
        attn = mix_refs[0][...]
    else:
        n_g = n_mix // 2
        lses = [r[...] for r in mix_refs[n_g:]]
        m = functools.reduce(jnp.maximum, lses)
        es = [jnp.exp(l - m) for l in lses]
        den = functools.reduce(lambda a, b: a + b, es)
        ws = [e / den for e in es]
        cols = []
        for h in range(N_HEADS):
            sl = slice(h * HEAD_DIM, (h + 1) * HEAD_DIM)
            acc = None
            for g in range(n_g):
                term = ws[g][:, h:h + 1] * mix_refs[g][:, sl]
                acc = term if acc is None else acc + term
            cols.append(acc)
        attn = jnp.concatenate(cols, axis=1)
    x = x_ref[...]
    mix = _dot(attn.astype(BF16), wo_ref[...])
    h1 = _layer_norm(ALPHA * x + mix, g1_ref[...], b1_ref[...])
    h1b = h1.astype(BF16)
    d_ff = wg_ref.shape[1]
    acc = jnp.zeros(x.shape, F32)
    for c in range(d_ff // FF_CHUNK):
        sl = slice(c * FF_CHUNK, (c + 1) * FF_CHUNK)
        g = _dot(h1b, wg_ref[:, sl])
        u = _dot(h1b, wu_ref[:, sl])
        acc = acc + _dot((g * jax.nn.sigmoid(g) * u).astype(BF16), wd_ref[sl, :])
    h2 = _layer_norm(ALPHA * h1 + acc, g2_ref[...], b2_ref[...])
    gate = jax.nn.sigmoid(_dot(h2.astype(BF16), pwg_ref[...]))
    o_ref[...] = h2 + gate * _dot(p_ref[...].astype(BF16), pwp_ref[...])


def _tail(mix_list, x, p_all, layer, wo, g1, b1, g2, b2, wg, wu, wd, pwg, pwp):
    m, d = x.shape
    tm = _row_tile(m)
    consts = [wo, g1, b1, g2, b2, wg, wu, wd, pwg, pwp]
    row = lambda a: pl.BlockSpec((tm, a.shape[1]), lambda i: (i, 0))
    single = lambda a: pl.BlockSpec(a.shape, lambda i: (0, 0), pipeline_mode=pl.Buffered(1))
    return pl.pallas_call(
        functools.partial(_tail_body, n_mix=len(mix_list)),
        grid=(m // tm,),
        in_specs=[row(a) for a in mix_list]
        + [row(x), pl.BlockSpec((None, tm, p_all.shape[2]), lambda i: (layer, i, 0))] + [single(a) for a in consts],
        out_specs=pl.BlockSpec((tm, d), lambda i: (i, 0)),
        out_shape=jax.ShapeDtypeStruct((m, d), F32),
        compiler_params=_cparams("parallel"),
        name="tail",
    )(*mix_list, x, p_all, *consts)


def _page_copies(pool_ref, buf_ref, sem_ref, pt_ref, b, slot, n_pages, start):
    def body(p, carry):
        if len(buf_ref.shape) == 4:
            dst = buf_ref.at[slot, p]
        else:
            rows = pool_ref.shape[1]
            dst = buf_ref.at[slot, pl.ds(pl.multiple_of(p * rows, rows), rows)]
        cp = pltpu.make_async_copy(pool_ref.at[pt_ref[b, p]], dst, sem_ref.at[slot])
        if start:
            cp.start()
        else:
            cp.wait()
        return carry
    lax.fori_loop(0, n_pages, body, 0)


def _paged_prefetch(pool_ref, buf_ref, sem_ref, pt_ref, n_pages):
    b = pl.program_id(0)
    nb = pl.num_programs(0)
    slot = b % 2

    @pl.when(b == 0)
    def _():
        _page_copies(pool_ref, buf_ref, sem_ref, pt_ref, 0, 0, n_pages, True)

    @pl.when(b + 1 < nb)
    def _():
        _page_copies(pool_ref, buf_ref, sem_ref, pt_ref, b + 1, 1 - slot, n_pages, True)

    _page_copies(pool_ref, buf_ref, sem_ref, pt_ref, b, slot, n_pages, False)
    return slot


def _sb_terms(z, valid):
    t = jnp.log1p(jnp.exp(-jnp.abs(z)))
    l1m = -jnp.maximum(z, 0.0) - t
    if valid is not None:
        l1m = jnp.where(valid, l1m, 0.0)
    ls = jnp.minimum(z, 0.0) - t
    return l1m, ls


def _strict_upper_sum_matrix(n):
    j = lax.broadcasted_iota(jnp.int32, (2 * n, n), 0) & (n - 1)
    s = lax.broadcasted_iota(jnp.int32, (2 * n, n), 1)
    return (j > s).astype(BF16)


def _tail_sums(l1m, uu):
    hi = l1m.astype(BF16)
    lo = (l1m - hi.astype(F32)).astype(BF16)
    return _dot(jnp.concatenate([hi, lo], axis=1), uu)


def _sb_prompt_body(q_ref, kv_ref, o_ref, r_ref, acc_ref, *, n_kv, grp):
    qi = pl.program_id(1)
    tq = q_ref.shape[1]
    rows = grp * tq
    uu = _strict_upper_sum_matrix(Q_TILE)
    qpos = qi * tq + (lax.broadcasted_iota(jnp.int32, (rows, Q_TILE), 0) & (tq - 1))
    lane = lax.broadcasted_iota(jnp.int32, (rows, Q_TILE), 1)
    for g in range(n_kv):
        qg = jnp.concatenate(
            [q_ref[0, :, (g * grp + u) * HEAD_DIM:(g * grp + u + 1) * HEAD_DIM] for u in range(grp)], axis=0)
        qg = (qg * (HEAD_DIM ** -0.5)).astype(BF16)
        r_ref[...] = jnp.zeros(r_ref.shape, F32)
        acc_ref[...] = jnp.zeros(acc_ref.shape, F32)

        def cond(c):
            return jnp.logical_and(c[0] >= 0, c[1] > SB_DEAD)

        def body(c):
            kj = c[0]
            off = pl.multiple_of(kj * Q_TILE, Q_TILE)
            k = kv_ref[0, g * HEAD_DIM:(g + 1) * HEAD_DIM, pl.ds(off, Q_TILE)].astype(BF16)
            v = kv_ref[0, (n_kv + g) * HEAD_DIM:(n_kv + g + 1) * HEAD_DIM, pl.ds(off, Q_TILE)].astype(BF16)
            z = _dot(qg, k)
            valid = (kj * Q_TILE + lane) < qpos
            l1m, ls = _sb_terms(z, valid)
            r = r_ref[...]
            a = jnp.where(valid, jnp.exp(ls + _tail_sums(l1m, uu) + r), 0.0)
            acc_ref[...] += _dot_nt(a.astype(BF16), v)
            rn = r + jnp.sum(l1m, axis=1, keepdims=True)
            r_ref[...] = rn
            return kj - 1, jnp.max(rn)

        lax.while_loop(cond, body, (qi, jnp.float32(0.0)))
        for u in range(grp):
            h = g * grp + u
            o_ref[0, :, h * HEAD_DIM:(h + 1) * HEAD_DIM] = acc_ref[u * tq:(u + 1) * tq, :]


def _sb_prompt(q, kv, n_kv):
    b, t, d = q.shape
    grp = N_HEADS // n_kv
    return pl.pallas_call(
        functools.partial(_sb_prompt_body, n_kv=n_kv, grp=grp),
        grid=(b, t // Q_TILE),
        in_specs=[pl.BlockSpec((1, Q_TILE, d), lambda i, j: (i, j, 0)),
                  pl.BlockSpec((1, kv.shape[1], t), lambda i, j: (i, 0, 0))],
        out_specs=pl.BlockSpec((1, Q_TILE, d), lambda i, j: (i, j, 0)),
        out_shape=jax.ShapeDtypeStruct((b, t, d), F32),
        scratch_shapes=[pltpu.VMEM((grp * Q_TILE, 1), F32), pltpu.VMEM((grp * Q_TILE, HEAD_DIM), F32)],
        compiler_params=_cparams("parallel", "arbitrary"),
        name="sb_prompt",
    )(q, kv)


def _head_spread(n_kv):
    d = lax.broadcasted_iota(jnp.int32, (HEAD_DIM, n_kv * HEAD_DIM), 0)
    c = lax.broadcasted_iota(jnp.int32, (HEAD_DIM, n_kv * HEAD_DIM), 1)
    return ((c & (HEAD_DIM - 1)) == d).astype(BF16)


def _block_diag_q(q, n_kv, scale):
    grp = N_HEADS // n_kv
    width = n_kv * HEAD_DIM
    spread = _dot((q * scale).astype(BF16), _head_spread(n_kv))
    row = lax.broadcasted_iota(jnp.int32, (N_HEADS, width), 0)
    col = lax.broadcasted_iota(jnp.int32, (N_HEADS, width), 1)
    own = (col >> 6) == (row >> int(math.log2(grp)))
    return jnp.where(own, spread, 0.0).astype(BF16)


def _block_diag_pick(o_full, n_kv):
    grp = N_HEADS // n_kv
    row = lax.broadcasted_iota(jnp.int32, (N_HEADS, HEAD_DIM), 0)
    out = jnp.zeros((N_HEADS, HEAD_DIM), F32)
    for g in range(n_kv):
        out = out + jnp.where((row >> int(math.log2(grp))) == g, o_full[:, g * HEAD_DIM:(g + 1) * HEAD_DIM], 0.0)
    return out


SB_EAGER_PAGES = 4


def _sb_sample_body(pt_ref, q_ref, pool_ref, o_ref, buf_ref, sem_ref, old_ref, old_sem, *, n_kv, n_pages):
    b = pl.program_id(0)
    nb = pl.num_programs(0)
    slot = b % 2
    n_eager = buf_ref.shape[1]
    first = n_pages - n_eager

    def eager(bb, sl, start):
        for i in range(n_eager):
            cp = pltpu.make_async_copy(pool_ref.at[pt_ref[bb, first + i]], buf_ref.at[sl, i], sem_ref.at[sl])
            if start:
                cp.start()
            else:
                cp.wait()

    @pl.when(b == 0)
    def _():
        eager(0, 0, True)

    @pl.when(b + 1 < nb)
    def _():
        eager(b + 1, 1 - slot, True)

    eager(b, slot, False)
    width = n_kv * HEAD_DIM
    qbd = _block_diag_q(q_ref[0], n_kv, HEAD_DIM ** -0.5)
    uu = _strict_upper_sum_matrix(PAGE)

    def step(page, r, acc):
        l1m, ls = _sb_terms(_dot(qbd, page[0:width, :].astype(BF16)), None)
        a = jnp.exp(ls + _tail_sums(l1m, uu) + r)
        acc = acc + _dot_nt(a.astype(BF16), page[width:2 * width, :].astype(BF16))
        rn = r + jnp.sum(l1m, axis=1, keepdims=True)
        return jnp.max(rn), rn, acc

    def cond(c):
        return jnp.logical_and(c[0] >= 0, c[1] > SB_DEAD)

    def newest(c):
        i, _, r, acc = c
        return (i - 1,) + step(buf_ref[slot, i], r, acc)

    def older(c):
        p, _, r, acc = c
        cp = pltpu.make_async_copy(pool_ref.at[pt_ref[b, p]], old_ref, old_sem.at[0])
        cp.start()
        cp.wait()
        return (p - 1,) + step(old_ref[...], r, acc)

    init = (jnp.int32(n_eager - 1), jnp.float32(0.0), jnp.zeros((N_HEADS, 1), F32), jnp.zeros((N_HEADS, width), F32))
    _, alive, r, acc = lax.while_loop(cond, newest, init)
    acc = lax.while_loop(cond, older, (jnp.int32(first - 1), alive, r, acc))[3]
    o_ref[0] = _block_diag_pick(acc, n_kv)


def _sb_sample(q, pool, page_table, n_kv):
    nb = q.shape[0]
    n_pages = page_table.shape[1]
    feat = pool.shape[1]
    n_eager = min(SB_EAGER_PAGES, n_pages)
    grid_spec = pltpu.PrefetchScalarGridSpec(
        num_scalar_prefetch=1,
        grid=(nb,),
        in_specs=[pl.BlockSpec((1, N_HEADS, HEAD_DIM), lambda i, pt: (i, 0, 0)),
                  pl.BlockSpec(memory_space=pl.ANY)],
        out_specs=pl.BlockSpec((1, N_HEADS, HEAD_DIM), lambda i, pt: (i, 0, 0)),
        scratch_shapes=[pltpu.VMEM((2, n_eager, feat, PAGE), F32), pltpu.SemaphoreType.DMA((2,)),
                        pltpu.VMEM((feat, PAGE), F32), pltpu.SemaphoreType.DMA((1,))],
    )
    return pl.pallas_call(
        functools.partial(_sb_sample_body, n_kv=n_kv, n_pages=n_pages),
        grid_spec=grid_spec,
        out_shape=jax.ShapeDtypeStruct((nb, N_HEADS, HEAD_DIM), F32),
        compiler_params=_cparams("arbitrary"),
        name="sb_sample",
    )(page_table, q, pool)


def _feature_major_pool(cache):
    n = cache.ndim
    return jnp.transpose(cache, (0,) + tuple(range(2, n)) + (1,)).reshape(cache.shape[0], -1, cache.shape[1])


def _token_major(x_t, feat_shape):
    b, _, t = x_t.shape
    nf = len(feat_shape)
    return jnp.transpose(x_t.reshape((b,) + tuple(feat_shape) + (t,)), (0, nf + 1) + tuple(range(1, nf + 1)))


def _sb_layer(xp, xs, cache, page_table, w_in):
    b, t, d = xp.shape
    n_kv = (w_in.shape[1] - d) // (2 * HEAD_DIM)
    ws, wts = [w_in[:, :d].astype(BF16)], [w_in[:, d:].T.astype(BF16)]
    qp, kvp_t = _proj(xp.reshape(b * t, d), ws, wts, b)
    attn_p = _sb_prompt(qp.reshape(b, t, d), kvp_t, n_kv)
    nb = xs.shape[0]
    qs, kvs_t = _proj(xs.reshape(nb, d), ws, wts, 1)
    attn_s = _sb_sample(qs.reshape(nb, N_HEADS, HEAD_DIM), _feature_major_pool(cache), page_table, n_kv)
    kv_s = _token_major(kvs_t, (2, n_kv, HEAD_DIM)).reshape(nb, 1, 2, n_kv, HEAD_DIM)
    return attn_p.reshape(b * t, d), attn_s.reshape(nb, d), _token_major(kvp_t, (2, n_kv, HEAD_DIM)), kv_s


def _toeplitz(base_row, rows, shift):
    return pltpu.roll(jnp.broadcast_to(base_row, (rows, base_row.shape[1])), shift, 1, stride=1, stride_axis=0)


def _dil_prompt_body(q_ref, kvc_ref, kvp_ref, base_ref, o_ref, lse_ref, *, n_kv, grp):
    mi = pl.program_id(2)
    tq = Q_TILE
    i_idx = lax.broadcasted_iota(jnp.int32, (tq, 2 * tq), 0)
    j_idx = lax.broadcasted_iota(jnp.int32, (tq, 2 * tq), 1)
    steps = i_idx - j_idx + tq
    mask = jnp.where((steps >= 0) & (steps <= tq) & ((mi > 0) | (j_idx >= tq)), 0.0, NEG_INF)
    lane = lax.broadcasted_iota(jnp.int32, (tq, 128), 1)
    lse_tile = jnp.zeros((tq, 128), F32)
    width = n_kv * HEAD_DIM
    for g in range(n_kv):
        ksl = slice(g * HEAD_DIM, (g + 1) * HEAD_DIM)
        vsl = slice(width + g * HEAD_DIM, width + (g + 1) * HEAD_DIM)
        k = jnp.concatenate([kvp_ref[0, :, ksl], kvc_ref[0, :, ksl]], axis=0).astype(BF16)
        v = jnp.concatenate([kvp_ref[0, :, vsl], kvc_ref[0, :, vsl]], axis=0).astype(BF16)
        heads = range(g * grp, (g + 1) * grp)
        qg = jnp.concatenate([q_ref[0, :, h * HEAD_DIM:(h + 1) * HEAD_DIM] for h in heads], axis=0)
        bias = jnp.stack([_toeplitz(base_ref[h:h + 1, :], tq, 0) for h in heads])
        s = _dot_nt((qg * (HEAD_DIM ** -0.5)).astype(BF16), k)
        s = (s.reshape(grp, tq, 2 * tq) + bias + mask[None]).reshape(grp * tq, 2 * tq)
        m = jnp.maximum(jnp.max(s, axis=1, keepdims=True), MASKED_ROW_FLOOR)
        e = jnp.exp(s - m)
        den = jnp.maximum(jnp.sum(e, axis=1, keepdims=True), 1e-30)
        o = _dot((e / den).astype(BF16), v)
        lse = m + jnp.log(den)
        for u, h in enumerate(heads):
            o_ref[0, :, h * HEAD_DIM:(h + 1) * HEAD_DIM] = o[u * tq:(u + 1) * tq, :]
            lse_tile = jnp.where(lane == h, lse[u * tq:(u + 1) * tq, :], lse_tile)
    lse_ref[0] = lse_tile


def _dil_prompt(q, kv, base, dil, n_kv):
    b, tm, _ = q.shape
    d = N_HEADS * HEAD_DIM
    kvw = 2 * n_kv * HEAD_DIM
    nm = tm // Q_TILE
    return pl.pallas_call(
        functools.partial(_dil_prompt_body, n_kv=n_kv, grp=N_HEADS // n_kv),
        grid=(b, dil, nm),
        in_specs=[pl.BlockSpec((1, Q_TILE, d), lambda i, r, m: (i, m, r)),
                  pl.BlockSpec((1, Q_TILE, kvw), lambda i, r, m: (i, m, r)),
                  pl.BlockSpec((1, Q_TILE, kvw), lambda i, r, m: (i, jnp.maximum(m - 1, 0), r)),
                  _const_spec(base.shape)],
        out_specs=[pl.BlockSpec((1, Q_TILE, d), lambda i, r, m: (i, m, r)),
                   pl.BlockSpec((1, Q_TILE, 128), lambda i, r, m: (i, m, r))],
        out_shape=[jax.ShapeDtypeStruct((b, tm, dil * d), F32), jax.ShapeDtypeStruct((b, tm, dil * 128), F32)],
        compiler_params=_cparams("parallel", "parallel", "arbitrary"),
        name="dil_prompt",
    )(q, kv, kv, base)


def _dil_sample_body(q_ref, kvn_ref, kvnt_ref, b0_ref, bm0_ref, bm1_ref, bm2_ref, st0_ref, st1_ref, st2_ref,
                     o_ref, so0_ref, so1_ref, so2_ref, *, n_kv):
    b = pl.program_id(0)
    width = n_kv * HEAD_DIM
    nb = kvnt_ref.shape[2]
    pick = lax.broadcasted_iota(jnp.int32, (2 * width, nb), 1) == b
    outs, lses = [], []
    groups = zip((st0_ref, st1_ref, st2_ref), (so0_ref, so1_ref, so2_ref), (bm0_ref, bm1_ref, bm2_ref))
    for g, (st, so, bm) in enumerate(groups):
        w = st.shape[2]
        kn = kvn_ref[0, g:g + 1, 0:width].astype(BF16).astype(F32)
        vn = kvn_ref[0, g:g + 1, width:2 * width].astype(BF16).astype(F32)
        qbd = _block_diag_q(q_ref[0, g], n_kv, HEAD_DIM ** -0.5)
        s_old = _dot(qbd, st[0, 0:width, :].astype(BF16)) + bm[...]
        s_new = jnp.sum(qbd.astype(F32) * kn, axis=1, keepdims=True) + b0_ref[:, 0:1]
        m = jnp.maximum(jnp.max(s_old, axis=1, keepdims=True), s_new)
        e_old = jnp.exp(s_old - m)
        e_new = jnp.exp(s_new - m)
        den = jnp.sum(e_old, axis=1, keepdims=True) + e_new
        o_full = (_dot_nt((e_old / den).astype(BF16), st[0, width:2 * width, :].astype(BF16))
                  + (e_new / den).astype(BF16).astype(F32) * vn)
        outs.append(_block_diag_pick(o_full, n_kv))
        lses.append(m + jnp.log(den))
        new_col = jnp.sum(jnp.where(pick, kvnt_ref[g], 0.0), axis=1, keepdims=True)
        last = lax.broadcasted_iota(jnp.int32, (2 * width, w), 1) == w - 1
        so[0] = jnp.where(last, new_col, pltpu.roll(st[0], w - 1, 1))
    m = functools.reduce(jnp.maximum, lses)
    es = [jnp.exp(l - m) for l in lses]
    den = functools.reduce(lambda a, b: a + b, es)
    o = None
    for e, og in zip(es, outs):
        o = (e / den) * og if o is None else o + (e / den) * og
    o_ref[0] = o


def _dil_sample(q, kvn, kvn_t, b0, bias_masks, states, n_kv):
    nb = q.shape[0]
    st_specs = [pl.BlockSpec((1,) + s.shape[1:], lambda i: (i, 0, 0)) for s in states]
    outs = pl.pallas_call(
        functools.partial(_dil_sample_body, n_kv=n_kv),
        grid=(nb,),
        in_specs=[pl.BlockSpec((1,) + q.shape[1:], lambda i: (i, 0, 0, 0)),
                  pl.BlockSpec((1,) + kvn.shape[1:], lambda i: (i, 0, 0)),
                  _const_spec(kvn_t.shape), _const_spec(b0.shape)]
        + [_const_spec(bm.shape) for bm in bias_masks] + st_specs,
        out_specs=[pl.BlockSpec((1, N_HEADS, HEAD_DIM), lambda i: (i, 0, 0))] + st_specs,
        out_shape=[jax.ShapeDtypeStruct((nb, N_HEADS, HEAD_DIM), F32)]
        + [jax.ShapeDtypeStruct(s.shape, F32) for s in states],
        compiler_params=_cparams("parallel"),
        name="dil_sample",
    )(q, kvn, kvn_t, b0, *bias_masks, *states)
    return outs[0], outs[1:]


def _dil_layer(xp, xs, states, w_in, rel_bias):
    b, t, d = xp.shape
    n_g = len(DIL_PATTERNS)
    w3 = w_in.reshape(d, n_g, -1)
    kvw = w3.shape[2] - d
    n_kv = kvw // (2 * HEAD_DIM)
    ws, wts = [], []
    for g in range(n_g):
        ws += [w3[:, g, :d].astype(BF16), w3[:, g, d:].astype(BF16)]
        wts.append(w3[:, g, d:].T.astype(BF16))
    outs_p = _proj(xp.reshape(b * t, d), ws, wts, b)
    lane = np.arange(2 * Q_TILE)
    mix, st_p = [], []
    for g, (w, dil) in enumerate(DIL_PATTERNS):
        assert w // dil == Q_TILE and t % (dil * Q_TILE) == 0
        base = _bias_by_distance(rel_bias, np.where(lane <= Q_TILE, (Q_TILE - lane) * dil, -1))
        o_g, lse_g = _dil_prompt(outs_p[2 * g].reshape(b, t // dil, dil * d),
                                 outs_p[2 * g + 1].reshape(b, t // dil, dil * kvw), base, dil, n_kv)
        mix.append((o_g.reshape(b * t, d), lse_g.reshape(b * t, 128)))
        st_p.append(_token_major(outs_p[2 * n_g + g][:, :, t - min(w, t):], (2, n_kv, HEAD_DIM)))
    mix_p = [m[0] for m in mix] + [m[1] for m in mix]
    nb = xs.shape[0]
    outs_s = _proj(xs.reshape(nb, d), ws, wts, 1)
    q_s = jnp.stack([outs_s[2 * g].reshape(nb, N_HEADS, HEAD_DIM) for g in range(n_g)], axis=1)
    kvn = jnp.stack([outs_s[2 * g + 1] for g in range(n_g)], axis=1)
    kvn_t = jnp.concatenate(outs_s[2 * n_g:], axis=0)
    bias_masks = []
    for g, (w, dil) in enumerate(DIL_PATTERNS):
        pos = np.arange(states[g].shape[1])
        bias = _bias_by_distance(rel_bias, np.where(pos % dil == 0, w - pos, -1))
        bias_masks.append(jnp.where(jnp.asarray(pos % dil == 0), bias, NEG_INF))
    b0 = _bias_by_distance(rel_bias, np.zeros((128,), np.int64))
    st_t = [jnp.transpose(s, (0, 2, 3, 4, 1)).reshape(nb, kvw, s.shape[1]) for s in states]
    attn_s, st_s = _dil_sample(q_s, kvn, kvn_t, b0, bias_masks, st_t, n_kv)
    st_s = [_token_major(s, (2, n_kv, HEAD_DIM)) for s in st_s]
    return mix_p, attn_s.reshape(nb, d), st_p, st_s


HEAD_PAD = 128
ROPE_HALF = QK_ROPE // 2
MLA_SCALE = (HEAD_DIM + QK_ROPE) ** -0.5


def _rms(x, g):
    return x * lax.rsqrt(jnp.mean(x * x, axis=-1, keepdims=True) + LN_EPS) * g


def _mla_project_body(x_ref, c_ref, s_ref, wdq_ref, qn_ref, kvn_ref, wq_ref, wqs_ref, wk_ref, wv_ref, vone_ref,
                      q_ref, k_ref, v_ref, lat_ref, latt_ref):
    h = _dot(x_ref[...].astype(BF16), wdq_ref[...])
    cq = _rms(h[:, 0:Q_LORA], qn_ref[...]).astype(BF16)
    ckv = _rms(h[:, Q_LORA:Q_LORA + KV_LORA], kvn_ref[...])
    ckvb = ckv.astype(BF16)
    cos, sin = c_ref[...], s_ref[...]
    base = Q_LORA + KV_LORA
    kr = h[:, base:base + HEAD_PAD] * cos + h[:, base + HEAD_PAD:base + 2 * HEAD_PAD] * sin
    cos_all = jnp.concatenate([cos] * N_HEADS, axis=1)
    sin_all = jnp.concatenate([sin] * N_HEADS, axis=1)
    q_ref[...] = ((_dot(cq, wq_ref[...]) * cos_all + _dot(cq, wqs_ref[...]) * sin_all) * MLA_SCALE).astype(BF16)
    k_ref[...] = (_dot(ckvb, wk_ref[...]) + jnp.concatenate([kr] * N_HEADS, axis=1)).astype(BF16)
    v_ref[...] = (_dot(ckvb, wv_ref[...]) + vone_ref[...]).astype(BF16)
    lat_ref[:, 0:KV_LORA] = ckv
    lat_ref[:, KV_LORA:KV_LORA + QK_ROPE] = kr[:, HEAD_DIM:HEAD_DIM + QK_ROPE]
    latt_ref[0, 0:KV_LORA, :] = ckv.T
    latt_ref[0, KV_LORA:KV_LORA + QK_ROPE, :] = kr.T[HEAD_DIM:HEAD_DIM + QK_ROPE, :]


def _mla_project(x, cos, sin, consts, n_batch):
    m, d = x.shape
    tm = _row_tile(m)
    t = m // n_batch
    per_b = t // tm
    pos_blocks = cos.shape[0] // tm
    lat_w = KV_LORA + QK_ROPE
    wide = N_HEADS * HEAD_PAD
    rope_spec = pl.BlockSpec((tm, HEAD_PAD), lambda i: (i % pos_blocks, 0))
    return pl.pallas_call(
        _mla_project_body,
        grid=(m // tm,),
        in_specs=[pl.BlockSpec((tm, d), lambda i: (i, 0)), rope_spec, rope_spec] + [_const_spec(c.shape) for c in consts],
        out_specs=[pl.BlockSpec((tm, wide), lambda i: (i, 0)), pl.BlockSpec((tm, wide), lambda i: (i, 0)),
                   pl.BlockSpec((tm, wide), lambda i: (i, 0)), pl.BlockSpec((tm, lat_w), lambda i: (i, 0)),
                   pl.BlockSpec((1, lat_w, tm), lambda i: (i // per_b, 0, i % per_b))],
        out_shape=[jax.ShapeDtypeStruct((m, wide), BF16), jax.ShapeDtypeStruct((m, wide), BF16),
                   jax.ShapeDtypeStruct((m, wide), BF16), jax.ShapeDtypeStruct((m, lat_w), F32),
                   jax.ShapeDtypeStruct((n_batch, lat_w, t), F32)],
        compiler_params=_cparams("parallel"),
        name="mla_project",
    )(x, cos, sin, *consts)


MLA_TQ = 256
MLA_CHUNK = 512


def _mla_prompt_body(q_ref, k_ref, v_ref, o_ref, m_ref, acc_ref):
    qi = pl.program_id(2)
    tq, ch = MLA_TQ, MLA_CHUNK
    q0 = qi * tq
    m_ref[...] = jnp.full(m_ref.shape, NEG_INF, F32)
    acc_ref[...] = jnp.zeros(acc_ref.shape, F32)
    i_idx = lax.broadcasted_iota(jnp.int32, (tq, ch), 0)
    j_idx = lax.broadcasted_iota(jnp.int32, (tq, ch), 1)

    def chunk(c, carry):
        k0 = pl.multiple_of(c * ch, ch)
        mask = jnp.where(q0 + i_idx >= k0 + j_idx, 0.0, NEG_INF)
        for hh in range(2):
            rows = slice(hh * tq, (hh + 1) * tq)
            lanes = slice(hh * HEAD_PAD, (hh + 1) * HEAD_PAD)
            s = _dot_nt(q_ref[0, :, lanes], k_ref[0, pl.ds(k0, ch), lanes]) + mask
            m = m_ref[rows, :]
            mn = jnp.maximum(m, jnp.max(s, axis=1, keepdims=True))
            p = jnp.exp(s - jnp.maximum(mn, MASKED_ROW_FLOOR))
            m_ref[rows, :] = mn
            acc_ref[rows, :] = jnp.exp(m - mn) * acc_ref[rows, :] + _dot(p.astype(BF16), v_ref[0, pl.ds(k0, ch), lanes])
        return carry

    lax.fori_loop(0, (q0 + tq + ch - 1) // ch, chunk, 0)
    for hh in range(2):
        acc = acc_ref[hh * tq:(hh + 1) * tq, :]
        o_ref[0, :, hh * HEAD_DIM:(hh + 1) * HEAD_DIM] = acc[:, 0:HEAD_DIM] / jnp.maximum(acc[:, HEAD_DIM:HEAD_DIM + 1], 1e-30)


def _mla_prompt(q, k, v):
    b, t, _ = q.shape
    tq = min(MLA_TQ, t)
    assert tq == MLA_TQ and t % MLA_CHUNK == 0
    pair = 2 * HEAD_PAD
    return pl.pallas_call(
        _mla_prompt_body,
        grid=(b, N_HEADS // 2, t // tq),
        in_specs=[pl.BlockSpec((1, tq, pair), lambda i, h, j: (i, j, h)),
                  pl.BlockSpec((1, t, pair), lambda i, h, j: (i, 0, h)),
                  pl.BlockSpec((1, t, pair), lambda i, h, j: (i, 0, h))],
        out_specs=pl.BlockSpec((1, tq, 2 * HEAD_DIM), lambda i, h, j: (i, j, h)),
        out_shape=jax.ShapeDtypeStruct((b, t, N_HEADS * HEAD_DIM), F32),
        scratch_shapes=[pltpu.VMEM((2 * tq, 1), F32), pltpu.VMEM((2 * tq, HEAD_PAD), F32)],
        compiler_params=_cparams("parallel", "parallel", "arbitrary"),
        name="mla_prompt",
    )(q, k, v)


def _mla_absorb_body(q_ref, wuk_ref, o_ref):
    for h in range(N_HEADS):
        qn = q_ref[:, h * HEAD_PAD:h * HEAD_PAD + HEAD_DIM]
        o_ref[:, h * KV_LORA:(h + 1) * KV_LORA] = _dot_nt(qn, wuk_ref[:, h * HEAD_DIM:(h + 1) * HEAD_DIM])


def _mla_unabsorb_body(o_ref, wuv_ref, y_ref):
    for h in range(N_HEADS):
        y_ref[:, h * HEAD_DIM:(h + 1) * HEAD_DIM] = _dot(
            o_ref[:, h * KV_LORA:(h + 1) * KV_LORA].astype(BF16), wuv_ref[:, h * HEAD_DIM:(h + 1) * HEAD_DIM])


def _whole_call(body, out_shape, name, *args):
    return pl.pallas_call(
        body, grid=(1,),
        in_specs=[_const_spec(a.shape) for a in args],
        out_specs=_const_spec(out_shape.shape),
        out_shape=out_shape, compiler_params=_cparams("arbitrary"), name=name)(*args)


def _mla_sample_body(pt_ref, ql_ref, q_ref, new_ref, pool_ref, o_ref, buf_ref, sem_ref, ckv_ref, kr_ref, *, n_pages):
    slot = _paged_prefetch(pool_ref, buf_ref, sem_ref, pt_ref, n_pages)
    ql = ql_ref[0].astype(BF16)
    qr = q_ref[0, :, HEAD_DIM:HEAD_DIM + QK_ROPE]
    new_c = new_ref[0, :, 0:KV_LORA].astype(BF16).astype(F32)
    new_r = new_ref[0, :, KV_LORA:KV_LORA + QK_ROPE].astype(BF16).astype(F32)
    s_new = (jnp.sum(ql.astype(F32) * new_c, axis=1, keepdims=True)
             + jnp.sum(qr.astype(F32) * new_r, axis=1, keepdims=True))
    for p in range(n_pages):
        ckv_ref[:, p * PAGE:(p + 1) * PAGE] = buf_ref[slot, p, 0:KV_LORA, :].astype(BF16)
        kr_ref[:, p * PAGE:(p + 1) * PAGE] = buf_ref[slot, p, KV_LORA:KV_LORA + QK_ROPE, :].astype(BF16)
    s = _dot(ql, ckv_ref[...]) + _dot(qr, kr_ref[...])
    m = jnp.maximum(jnp.max(s, axis=1, keepdims=True), s_new)
    p_old = jnp.exp(s - m)
    p_new = jnp.exp(s_new - m)
    den = jnp.sum(p_old, axis=1, keepdims=True) + p_new
    acc = _dot_nt(p_old.astype(BF16), ckv_ref[...]) + p_new * new_c
    o_ref[0] = acc / den


def _mla_sample(ql, q, lat_new, pool, page_table):
    nb = ql.shape[0]
    n_pages = page_table.shape[1]
    feat = pool.shape[1]
    grid_spec = pltpu.PrefetchScalarGridSpec(
        num_scalar_prefetch=1,
        grid=(nb,),
        in_specs=[pl.BlockSpec((1, N_HEADS, KV_LORA), lambda i, pt: (i, 0, 0)),
                  pl.BlockSpec((1, N_HEADS, HEAD_PAD), lambda i, pt: (i, 0, 0)),
                  pl.BlockSpec((1, 1, lat_new.shape[2]), lambda i, pt: (i, 0, 0)),
                  pl.BlockSpec(memory_space=pl.ANY)],
        out_specs=pl.BlockSpec((1, N_HEADS, KV_LORA), lambda i, pt: (i, 0, 0)),
        scratch_shapes=[pltpu.VMEM((2, n_pages, feat, PAGE), F32), pltpu.SemaphoreType.DMA((2,)),
                        pltpu.VMEM((KV_LORA, n_pages * PAGE), BF16), pltpu.VMEM((QK_ROPE, n_pages * PAGE), BF16)],
    )
    return pl.pallas_call(
        functools.partial(_mla_sample_body, n_pages=n_pages),
        grid_spec=grid_spec,
        out_shape=jax.ShapeDtypeStruct((nb, N_HEADS, KV_LORA), F32),
        compiler_params=_cparams("arbitrary"),
        name="mla_sample",
    )(page_table, ql, q, lat_new, pool)


def _pad_heads(w, parts):
    out = jnp.zeros((w.shape[0], N_HEADS, HEAD_PAD), w.dtype)
    for src, size, dst in parts:
        out = out.at[:, :, dst:dst + size].set(w[:, :, src:src + size])
    return out.reshape(w.shape[0], N_HEADS * HEAD_PAD)


def _mla_layer(xp, xs, cache, page_table, w_dq, q_norm, kv_norm, w_uq, w_uk, w_uv, past_len):
    b, t, d = xp.shape
    nb = xs.shape[0]
    r0, r1 = HEAD_DIM, HEAD_DIM + ROPE_HALF
    keep = [(0, HEAD_DIM, 0), (HEAD_DIM, ROPE_HALF, r0), (r1, ROPE_HALF, r1)]
    swap = [(r1, ROPE_HALF, r0), (HEAD_DIM, ROPE_HALF, r1)]
    base = Q_LORA + KV_LORA
    x1w, x2w = w_dq[:, base:base + ROPE_HALF], w_dq[:, base + ROPE_HALF:base + QK_ROPE]
    z_lo, z_hi = jnp.zeros((d, HEAD_DIM), F32), jnp.zeros((d, HEAD_PAD - HEAD_DIM - QK_ROPE), F32)
    kr_keep = jnp.concatenate([z_lo, x1w, x2w, z_hi], axis=1)
    kr_swap = jnp.concatenate([z_lo, x2w, x1w, z_hi], axis=1)
    wdq = jnp.concatenate([w_dq[:, :base], kr_keep, kr_swap], axis=1).astype(BF16)
    wq3 = w_uq.reshape(Q_LORA, N_HEADS, HEAD_DIM + QK_ROPE)
    wq = _pad_heads(wq3, keep).astype(BF16)
    wqs = _pad_heads(wq3, swap).astype(BF16)
    wk = _pad_heads(w_uk, [(0, HEAD_DIM, 0)]).astype(BF16)
    wv = w_uv.reshape(KV_LORA, N_HEADS * HEAD_DIM).astype(BF16)
    wv_pad = _pad_heads(w_uv, [(0, HEAD_DIM, 0)]).astype(BF16)
    vone = jnp.asarray((np.arange(N_HEADS * HEAD_PAD) % HEAD_PAD == HEAD_DIM).astype(np.float32)).reshape(1, -1)
    consts = [wdq, q_norm.reshape(1, -1), kv_norm.reshape(1, -1), wq, wqs, wk, wv_pad, vone]

    def rope_tables(pos):
        inv = ROPE_THETA ** (-jnp.arange(ROPE_HALF, dtype=F32) / ROPE_HALF)
        ang = pos.astype(F32)[:, None] * inv[None, :]
        cos, sin = jnp.cos(ang), jnp.sin(ang)
        n = pos.shape[0]
        c = jnp.concatenate([jnp.ones((n, HEAD_DIM), F32), cos, cos, jnp.zeros((n, HEAD_PAD - r1 - ROPE_HALF), F32)], axis=1)
        s = jnp.concatenate([jnp.zeros((n, HEAD_DIM), F32), -sin, sin, jnp.zeros((n, HEAD_PAD - r1 - ROPE_HALF), F32)], axis=1)
        return c, s

    cos_p, sin_p = rope_tables(jnp.arange(t))
    q, k, v, _, lat_p_t = _mla_project(xp.reshape(b * t, d), cos_p, sin_p, consts, b)
    wide = N_HEADS * HEAD_PAD
    attn_p = _mla_prompt(q.reshape(b, t, wide), k.reshape(b, t, wide), v.reshape(b, t, wide))
    cos_s, sin_s = rope_tables(jnp.full((nb,), past_len, jnp.int32))
    qs, _, _, lat_s, lat_s_t = _mla_project(xs.reshape(nb, d), cos_s, sin_s, consts, 1)
    wuk2 = w_uk.reshape(KV_LORA, N_HEADS * HEAD_DIM).astype(BF16)
    ql = _whole_call(_mla_absorb_body, jax.ShapeDtypeStruct((nb, N_HEADS * KV_LORA), F32), "mla_absorb", qs, wuk2)
    o_lat = _mla_sample(ql.reshape(nb, N_HEADS, KV_LORA), qs.reshape(nb, N_HEADS, HEAD_PAD),
                        lat_s.reshape(nb, 1, -1), _feature_major_pool(cache), page_table)
    attn_s = _whole_call(_mla_unabsorb_body, jax.ShapeDtypeStruct((nb, d), F32), "mla_unabsorb",
                         o_lat.reshape(nb, N_HEADS * KV_LORA), wv)
    return (attn_p.reshape(b * t, d), attn_s, jnp.transpose(lat_p_t, (0, 2, 1)),
            jnp.transpose(lat_s_t, (2, 0, 1)))


CMP_HIDDEN = 2 * HEAD_DIM
KV_PAIR = 2 * HEAD_DIM


def _nsa_pe_body(pe_ref, w1_ref, o_ref):
    for c in range(2):
        o_ref[:, c * CMP_HIDDEN:(c + 1) * CMP_HIDDEN] = _dot(pe_ref[c], w1_ref[c])


def _compress(load_rows, n_h, wblk_ref, peh_ref, w2k_ref, w2v_ref):
    hid = jnp.zeros((n_h, 4 * CMP_HIDDEN), F32)
    for s in range(0, CMP_STRIDE, 2):
        rows = jnp.concatenate([load_rows(s), load_rows(s + 1)], axis=1).astype(BF16)
        hid = hid + _dot(rows, wblk_ref[s // 2])
    up = lambda x: pltpu.roll(x, n_h - 1, 0)
    peh = peh_ref[0:1, :]
    hk = peh[:, 0:CMP_HIDDEN] + hid[:, 0:CMP_HIDDEN] + up(hid[:, CMP_HIDDEN:2 * CMP_HIDDEN])
    hv = peh[:, CMP_HIDDEN:] + hid[:, 2 * CMP_HIDDEN:3 * CMP_HIDDEN] + up(hid[:, 3 * CMP_HIDDEN:])
    return _dot(jax.nn.gelu(hk).astype(BF16), w2k_ref[...]) + _dot(jax.nn.gelu(hv).astype(BF16), w2v_ref[...])


def _nsa_compress_body(cmp_ref, wblk_ref, peh_ref, w2k_ref, w2v_ref, o_ref):
    n_h = o_ref.shape[1]
    o_ref[0] = _compress(lambda s: cmp_ref[0, pl.ds(s, n_h, stride=CMP_STRIDE), :], n_h,
                         wblk_ref, peh_ref, w2k_ref, w2v_ref)


def _nsa_compress_prompt(cmp, consts):
    b, t, _ = cmp.shape
    n_h = t // CMP_STRIDE
    return pl.pallas_call(
        _nsa_compress_body,
        grid=(b,),
        in_specs=[pl.BlockSpec((1, t, KV_PAIR), lambda i: (i, 0, 0))] + [_const_spec(c.shape) for c in consts],
        out_specs=pl.BlockSpec((1, n_h, KV_PAIR), lambda i: (i, 0, 0)),
        out_shape=jax.ShapeDtypeStruct((b, n_h, KV_PAIR), F32),
        compiler_params=_cparams("parallel"),
        name="nsa_compress",
    )(cmp, *consts)


def _intersect_matrix(n_c, n_s):
    n = lax.broadcasted_iota(jnp.int32, (n_c, n_s), 0) * CMP_STRIDE
    j = lax.broadcasted_iota(jnp.int32, (n_c, n_s), 1) * SLC_BLOCK
    return ((n < j + SLC_BLOCK) & (n + CMP_BLOCK > j)).astype(BF16)


def _split_dot(x, w):
    hi = x.astype(BF16)
    lo = (x - hi.astype(F32)).astype(BF16)
    return _dot(hi, w) + _dot(lo, w)


def _nsa_cmp_bias_body(base_ref, o_ref):
    n_c = o_ref.shape[1]
    for c in range(CMP_STRIDE):
        o_ref[0, :, c * n_c:(c + 1) * n_c] = _toeplitz(base_ref[0, c:c + 1, :], n_c, 1)[:, 0:n_c]


def _nsa_cmp_bias(rel_bias, t):
    n_c = t // CMP_STRIDE
    u = np.arange(2 * n_c)[None, :]
    c = np.arange(CMP_STRIDE)[:, None]
    dist = np.where(u >= n_c, CMP_STRIDE * (2 * n_c - 1 - u) + c - (CMP_BLOCK - 1), -1)
    base = _bias_by_distance(rel_bias, dist)
    out = pl.pallas_call(
        _nsa_cmp_bias_body,
        grid=(N_HEADS,),
        in_specs=[pl.BlockSpec((1, CMP_STRIDE, 2 * n_c), lambda h: (h, 0, 0))],
        out_specs=pl.BlockSpec((1, n_c, CMP_STRIDE * n_c), lambda h: (h, 0, 0)),
        out_shape=jax.ShapeDtypeStruct((N_HEADS, n_c, CMP_STRIDE * n_c), F32),
        compiler_params=_cparams("parallel"),
        name="nsa_cmp_bias",
    )(base)
    return out.reshape(N_HEADS, t, n_c)


NSA_PCHUNK = 512
NSA_WKEYS = NSA_WINDOW + Q_TILE


def _value_ones(kv):
    lane = lax.broadcasted_iota(jnp.int32, kv.shape, 1)
    rolled = pltpu.roll(kv, HEAD_DIM, 1)
    return jnp.where(lane < HEAD_DIM, rolled, jnp.where(lane == HEAD_DIM, 1.0, 0.0)).astype(BF16)


def _value_ones_t(v_t):
    first = lax.broadcasted_iota(jnp.int32, v_t.shape, 0) == 0
    return jnp.concatenate([v_t, jnp.where(first, 1.0, 0.0)], axis=0).astype(BF16)


def _nsa_bias_tiles_body(rev_ref, o_ref, base_ref, *, nq):
    delta = pl.program_id(0)
    tq = Q_TILE
    width = o_ref.shape[3]
    for t in range(width // tq + 1):
        base_ref[:, t * tq:(t + 1) * tq] = rev_ref[nq - 1 - delta + t]
    for h in range(N_HEADS):
        o_ref[0, h] = _toeplitz(base_ref[h:h + 1, :], tq, width + 1)[:, 0:width]


def _nsa_bias_tiles(rev, nq, n_delta, width):
    return pl.pallas_call(
        functools.partial(_nsa_bias_tiles_body, nq=nq),
        grid=(n_delta,),
        in_specs=[_const_spec(rev.shape)],
        out_specs=pl.BlockSpec((1, N_HEADS, Q_TILE, width), lambda i: (i, 0, 0, 0)),
        out_shape=jax.ShapeDtypeStruct((n_delta, N_HEADS, Q_TILE, width), F32),
        scratch_shapes=[pltpu.VMEM((N_HEADS, width + Q_TILE), F32)],
        compiler_params=_cparams("parallel"),
        name="nsa_bias_tiles",
    )(rev)


def _nsa_prompt_body(q_ref, gate_ref, slc_ref, win_ref, kvc_ref, biasc_ref, tile_ref, wtile_ref, o_ref,
                     qst_ref, oc_ref, ms_ref, accs_ref, sel_ref, *, n_s):
    qi = pl.program_id(1)
    c = pl.program_id(2)
    tq, ch = Q_TILE, NSA_PCHUNK
    n_c = kvc_ref.shape[1]
    q0 = qi * tq
    rows_all = N_HEADS * tq

    @pl.when(c == 0)
    def _():
        for h in range(N_HEADS):
            qst_ref[h * tq:(h + 1) * tq, :] = (
                q_ref[0, :, h * HEAD_DIM:(h + 1) * HEAD_DIM] * (HEAD_DIM ** -0.5)).astype(BF16)

        kvc = kvc_ref[0]
        kcb = kvc[:, 0:HEAD_DIM].astype(BF16)
        vc = _value_ones(kvc)
        qpos_c = q0 + lax.broadcasted_iota(jnp.int32, (tq, n_c), 0)
        cend = lax.broadcasted_iota(jnp.int32, (tq, n_c), 1) * CMP_STRIDE + CMP_BLOCK
        mask_c = jnp.where(cend <= qpos_c + 1, 0.0, NEG_INF)
        s = (_dot_nt(qst_ref[...], kcb).reshape(N_HEADS, tq, n_c) + biasc_ref[...] + mask_c[None]).reshape(rows_all, n_c)
        m = jnp.maximum(jnp.max(s, axis=1, keepdims=True), MASKED_ROW_FLOOR)
        e = jnp.exp(s - m)
        p = e / jnp.maximum(jnp.sum(e, axis=1, keepdims=True), 1e-30)
        oc_ref[...] = _dot(p.astype(BF16), vc)[:, 0:HEAD_DIM]
        imp = _split_dot(jnp.sum(p.reshape(N_HEADS, tq, n_c), axis=0), _intersect_matrix(n_c, n_s))

        qblk = (q0 + lax.broadcasted_iota(jnp.int32, (tq, n_s), 0)) >> 6
        jb = lax.broadcasted_iota(jnp.int32, (tq, n_s), 1)
        forced = (jb == 0) | (jb == qblk) | (jb == qblk - 1)
        score = jnp.where(jb <= qblk, imp + jnp.where(forced, FORCE_BONUS, 0.0), NEG_INF)
        rank = jnp.zeros((tq, n_s), F32)
        for j in range(n_s):
            col = score[:, j:j + 1]
            rank = rank + ((col > score) | ((col == score) & (jb > j))).astype(F32)
        sel_ref[...] = (rank < N_SELECT).astype(BF16)
        ms_ref[...] = jnp.full(ms_ref.shape, NEG_INF, F32)
        accs_ref[...] = jnp.zeros(accs_ref.shape, F32)

    @pl.when(c * ch < q0 + tq)
    def _():
        k0 = c * ch
        i_idx = lax.broadcasted_iota(jnp.int32, (tq, ch), 0)
        j_idx = lax.broadcasted_iota(jnp.int32, (tq, ch), 1)
        e_row = lax.broadcasted_iota(jnp.int32, (n_s, ch), 0)
        e_lane = lax.broadcasted_iota(jnp.int32, (n_s, ch), 1)
        expand = (((k0 + e_lane) >> 6) == e_row).astype(BF16)
        valid = (_dot(sel_ref[...], expand) > 0.5) & (q0 + i_idx >= k0 + j_idx)
        mask = jnp.where(valid, 0.0, NEG_INF)

        @pl.when(jnp.max(mask) > -1.0)
        def _():
            kv = slc_ref[0, :, pl.ds(pl.multiple_of(k0, ch), ch)]
            kb = kv[0:HEAD_DIM, :].astype(BF16)
            va = _value_ones_t(kv[HEAD_DIM:KV_PAIR, :])
            s = (_dot(qst_ref[...], kb).reshape(N_HEADS, tq, ch) + tile_ref[0] + mask[None]).reshape(N_HEADS * tq, ch)
            m = ms_ref[...]
            mn = jnp.maximum(m, jnp.max(s, axis=1, keepdims=True))
            p = jnp.exp(s - jnp.maximum(mn, MASKED_ROW_FLOOR))
            ms_ref[...] = mn
            accs_ref[...] = jnp.exp(m - mn) * accs_ref[...] + _dot_nt(p.astype(BF16), va)

    @pl.when(c == pl.num_programs(2) - 1)
    def _():
        wk = NSA_WKEYS
        k0w = jnp.maximum(q0 - NSA_WINDOW, 0)
        dist = ((q0 - k0w) + lax.broadcasted_iota(jnp.int32, (tq, wk), 0)
                - lax.broadcasted_iota(jnp.int32, (tq, wk), 1))
        mask_w = jnp.where((dist >= 0) & (dist <= NSA_WINDOW), 0.0, NEG_INF)
        kvw = win_ref[0, :, pl.ds(pl.multiple_of(k0w, tq), wk)]
        kwb = kvw[0:HEAD_DIM, :].astype(BF16)
        vwa = _value_ones_t(kvw[HEAD_DIM:KV_PAIR, :])
        s = (_dot(qst_ref[...], kwb).reshape(N_HEADS, tq, wk) + wtile_ref[0] + mask_w[None]).reshape(rows_all, wk)
        m = jnp.maximum(jnp.max(s, axis=1, keepdims=True), MASKED_ROW_FLOOR)
        acc_w = _dot_nt(jnp.exp(s - m).astype(BF16), vwa)
        o_w = acc_w[:, 0:HEAD_DIM] / jnp.maximum(acc_w[:, HEAD_DIM:HEAD_DIM + 1], 1e-30)
        acc_s = accs_ref[...]
        o_s = acc_s[:, 0:HEAD_DIM] / jnp.maximum(acc_s[:, HEAD_DIM:HEAD_DIM + 1], 1e-30)
        gates = jax.nn.sigmoid(gate_ref[0])
        for h in range(N_HEADS):
            rows = slice(h * tq, (h + 1) * tq)
            o_ref[0, :, h * HEAD_DIM:(h + 1) * HEAD_DIM] = (
                gates[:, h:h + 1] * oc_ref[rows, :] + gates[:, N_HEADS + h:N_HEADS + h + 1] * o_s[rows, :]
                + gates[:, 2 * N_HEADS + h:2 * N_HEADS + h + 1] * o_w[rows, :])


def _nsa_prompt(q, gate, kv_t, win_t, kvc, bias_c, tiles, wtiles):
    b, t, d = q.shape
    tq, ch = Q_TILE, NSA_PCHUNK
    n_c = kvc.shape[1]
    n_s = t // SLC_BLOCK
    rows = N_HEADS * tq
    assert t % ch == 0 and t >= NSA_WKEYS
    step = ch // tq

    def tile_index(i, j, c):
        return (j - step * jnp.minimum(c, (j * tq + tq - 1) // ch), 0, 0, 0)

    return pl.pallas_call(
        functools.partial(_nsa_prompt_body, n_s=n_s),
        grid=(b, t // tq, t // ch),
        in_specs=[pl.BlockSpec((1, tq, d), lambda i, j, c: (i, j, 0)),
                  pl.BlockSpec((1, tq, gate.shape[2]), lambda i, j, c: (i, j, 0)),
                  pl.BlockSpec((1, KV_PAIR, t), lambda i, j, c: (i, 1, 0)),
                  pl.BlockSpec((1, KV_PAIR, t), lambda i, j, c: (i, 0, 0)),
                  pl.BlockSpec((1, n_c, KV_PAIR), lambda i, j, c: (i, 0, 0)),
                  pl.BlockSpec((N_HEADS, tq, n_c), lambda i, j, c: (0, j, 0)),
                  pl.BlockSpec((1, N_HEADS, tq, ch), tile_index),
                  pl.BlockSpec((1, N_HEADS, tq, NSA_WKEYS), lambda i, j, c: (jnp.minimum(j, NSA_WINDOW // tq), 0, 0, 0))],
        out_specs=pl.BlockSpec((1, tq, d), lambda i, j, c: (i, j, 0)),
        out_shape=jax.ShapeDtypeStruct((b, t, d), F32),
        scratch_shapes=[pltpu.VMEM((rows, HEAD_DIM), BF16), pltpu.VMEM((rows, HEAD_DIM), F32),
                        pltpu.VMEM((rows, 1), F32), pltpu.VMEM((rows, KV_PAIR), F32),
                        pltpu.VMEM((tq, n_s), BF16)],
        compiler_params=_cparams("parallel", "arbitrary", "arbitrary"),
        name="nsa_prompt",
    )(q, gate, kv_t, win_t, kvc, bias_c, tiles, wtiles)


def _nsa_sample_body(pt_ref, q_ref, gate_ref, new_ref, wnew_ref, cwin_ref, bc_ref, bs_ref, bw_ref, b0_ref,
                     wblk_ref, peh_ref, w2k_ref, w2v_ref, pool_ref, o_ref, wout_ref, buf_ref, sem_ref, cmp_ref,
                     ks_ref, vs_ref, expand_ref, *, n_pages, n_sp):
    slot = _paged_prefetch(pool_ref, buf_ref, sem_ref, pt_ref, n_pages)
    b = pl.program_id(0)
    past = n_pages * PAGE
    n_h = past // CMP_STRIDE
    q = (q_ref[0] * (HEAD_DIM ** -0.5)).astype(BF16)
    qf = q.astype(F32)

    def to_rows(p, carry):
        cmp_ref[pl.ds(pl.multiple_of(p * PAGE, PAGE), PAGE), :] = buf_ref[slot, p, 0:KV_PAIR, :].T
        return carry
    lax.fori_loop(0, n_pages, to_rows, 0)
    kvc = _compress(lambda s: cmp_ref[pl.ds(s, n_h, stride=CMP_STRIDE), :], n_h, wblk_ref, peh_ref, w2k_ref, w2v_ref)
    kc = kvc[:, 0:HEAD_DIM].astype(BF16)
    vc = kvc[:, HEAD_DIM:KV_PAIR].astype(BF16)
    cend = lax.broadcasted_iota(jnp.int32, (N_HEADS, n_h), 1) * CMP_STRIDE + CMP_BLOCK
    valid_c = cend <= past + 1
    s = jnp.where(valid_c, _dot_nt(q, kc) + bc_ref[...], NEG_INF)
    m = jnp.max(s, axis=1, keepdims=True)
    e = jnp.where(valid_c, jnp.exp(s - m), 0.0)
    p_c = e / jnp.maximum(jnp.sum(e, axis=1, keepdims=True), 1e-30)
    o_c = _dot(p_c.astype(BF16), vc)

    psum = jnp.broadcast_to(jnp.sum(p_c, axis=0, keepdims=True), (8, n_h))
    imp = _split_dot(psum, _intersect_matrix(n_h, n_sp))[0:1, :]
    qblk = past // SLC_BLOCK
    jb = lax.broadcasted_iota(jnp.int32, (1, n_sp), 1)
    forced = (jb == 0) | (jb == qblk) | (jb == qblk - 1)
    score = jnp.where(jb <= qblk, imp + jnp.where(forced, FORCE_BONUS, 0.0), NEG_INF)
    r_idx = lax.broadcasted_iota(jnp.int32, (n_sp, n_sp), 0)
    c_idx = lax.broadcasted_iota(jnp.int32, (n_sp, n_sp), 1)
    score_b = jnp.broadcast_to(score, (n_sp, n_sp))
    score_col = jnp.sum(jnp.where(r_idx == c_idx, score_b, 0.0), axis=1, keepdims=True)
    beats = (score_col > score_b) | ((score_col == score_b) & (r_idx < c_idx))
    rank = jnp.sum(beats.astype(F32), axis=0, keepdims=True)
    sel = jnp.broadcast_to((rank < N_SELECT).astype(BF16), (8, n_sp))
    sel_new = rank[:, qblk:qblk + 1] < N_SELECT

    def attend(s_old, v_old_t, kv_new, new_ok):
        k_new = kv_new[:, 0:HEAD_DIM].astype(BF16).astype(F32)
        v_new = kv_new[:, HEAD_DIM:KV_PAIR].astype(BF16).astype(F32)
        s_new = jnp.where(new_ok, jnp.sum(qf * k_new, axis=1, keepdims=True) + b0_ref[:, 0:1], NEG_INF)
        m = jnp.maximum(jnp.maximum(jnp.max(s_old, axis=1, keepdims=True), s_new), MASKED_ROW_FLOOR)
        p_old = jnp.exp(s_old - m)
        p_new = jnp.exp(s_new - m)
        den = jnp.sum(p_old, axis=1, keepdims=True) + p_new
        acc = _dot_nt(p_old.astype(BF16), v_old_t) + p_new.astype(BF16).astype(F32) * v_new
        return acc / jnp.maximum(den, 1e-30)

    @pl.when(b == 0)
    def _():
        blk = lax.broadcasted_iota(jnp.int32, expand_ref.shape, 1) >> 6
        expand_ref[...] = (blk == lax.broadcasted_iota(jnp.int32, expand_ref.shape, 0)).astype(BF16)

    for p in range(n_pages):
        lanes = slice(p * PAGE, (p + 1) * PAGE)
        ks_ref[:, lanes] = buf_ref[slot, p, KV_PAIR:KV_PAIR + HEAD_DIM, :].astype(BF16)
        vs_ref[:, lanes] = buf_ref[slot, p, KV_PAIR + HEAD_DIM:2 * KV_PAIR, :].astype(BF16)
    picked = _dot(sel, expand_ref[...])[0:1, :] > 0.5
    s = _dot(q, ks_ref[...]) + bs_ref[...] + jnp.where(picked, 0.0, NEG_INF)
    o_s = attend(s, vs_ref[...], new_ref[0, 0:1, :], jnp.broadcast_to(sel_new, (N_HEADS, 1)))

    cwin = cwin_ref[0]
    s = _dot(q, cwin[0:HEAD_DIM, :].astype(BF16)) + bw_ref[...]
    o_w = attend(s, cwin[HEAD_DIM:KV_PAIR, :].astype(BF16), new_ref[0, 1:2, :], jnp.full((N_HEADS, 1), True))

    gates = jax.nn.sigmoid(gate_ref[0])
    o_ref[0] = gates[:, 0:1] * o_c + gates[:, 1:2] * o_s + gates[:, 2:3] * o_w

    wb = cwin.shape[1]
    pick = lax.broadcasted_iota(jnp.int32, wnew_ref.shape, 1) == b
    new_col = jnp.sum(jnp.where(pick, wnew_ref[...], 0.0), axis=1, keepdims=True)
    last = lax.broadcasted_iota(jnp.int32, cwin.shape, 1) == wb - 1
    wout_ref[0] = jnp.where(last, new_col, pltpu.roll(cwin, wb - 1, 1))


def _nsa_sample(q, gate_t, new, wnew_t, cwin_t, biases, consts, pool, page_table):
    nb = q.shape[0]
    n_pages = page_table.shape[1]
    past = n_pages * PAGE
    assert cwin_t.shape[2] <= NSA_WINDOW
    n_s = past // SLC_BLOCK + 1
    n_sp = -(-n_s // 128) * 128
    per_b = lambda a: pl.BlockSpec((1,) + a.shape[1:], lambda i, pt: (i,) + (0,) * (a.ndim - 1))
    const = lambda a: pl.BlockSpec(a.shape, lambda i, pt: (0,) * a.ndim)
    grid_spec = pltpu.PrefetchScalarGridSpec(
        num_scalar_prefetch=1,
        grid=(nb,),
        in_specs=[per_b(q), per_b(gate_t), per_b(new), const(wnew_t), per_b(cwin_t)]
        + [const(a) for a in biases] + [const(a) for a in consts]
        + [pl.BlockSpec(memory_space=pl.ANY)],
        out_specs=[pl.BlockSpec((1, N_HEADS, HEAD_DIM), lambda i, pt: (i, 0, 0)), per_b(cwin_t)],
        scratch_shapes=[pltpu.VMEM((2, n_pages, pool.shape[1], PAGE), F32), pltpu.SemaphoreType.DMA((2,)),
                        pltpu.VMEM((past, KV_PAIR), F32), pltpu.VMEM((HEAD_DIM, past), BF16),
                        pltpu.VMEM((HEAD_DIM, past), BF16), pltpu.VMEM((n_sp, past), BF16)],
    )
    return pl.pallas_call(
        functools.partial(_nsa_sample_body, n_pages=n_pages, n_sp=n_sp),
        grid_spec=grid_spec,
        out_shape=[jax.ShapeDtypeStruct((nb, N_HEADS, HEAD_DIM), F32), jax.ShapeDtypeStruct(cwin_t.shape, F32)],
        compiler_params=_cparams("arbitrary"),
        name="nsa_sample",
    )(page_table, q, gate_t, new, wnew_t, cwin_t, *biases, *consts, pool)


def _nsa_layer(xp, xs, cache_kv, cache_win, page_table, w_in, pe, w1, w2, rel_bias):
    b, t, d = xp.shape
    nb = xs.shape[0]
    past = page_table.shape[1] * PAGE
    w_q, w_gate = w_in[:, :d].astype(BF16), w_in[:, d + 3 * KV_PAIR:].astype(BF16)
    w_cmp, w_slc, w_win = [w_in[:, d + i * KV_PAIR:d + (i + 1) * KV_PAIR].astype(BF16) for i in range(3)]
    wts = [w_in[:, d:d + 2 * KV_PAIR].T.astype(BF16), w_in[:, d + 2 * KV_PAIR:d + 3 * KV_PAIR].T.astype(BF16)]

    w1r = w1.reshape(2, 2, CMP_STRIDE, HEAD_DIM, CMP_HIDDEN)
    zero = jnp.zeros((CMP_STRIDE, HEAD_DIM, CMP_HIDDEN), F32)
    top = jnp.concatenate([w1r[0, 0], w1r[0, 1], zero, zero], axis=2)
    bot = jnp.concatenate([zero, zero, w1r[1, 0], w1r[1, 1]], axis=2)
    wblk = jnp.concatenate([top, bot], axis=1).astype(BF16).reshape(CMP_STRIDE // 2, 2 * KV_PAIR, -1)
    pe8 = jnp.broadcast_to(pe.reshape(2, 1, -1), (2, 8, CMP_BLOCK * HEAD_DIM)).astype(BF16)
    peh = _whole_call(_nsa_pe_body, jax.ShapeDtypeStruct((8, 2 * CMP_HIDDEN), F32), "nsa_pe", pe8, w1.astype(BF16))
    zpad = jnp.zeros((CMP_HIDDEN, HEAD_DIM), F32)
    w2k = jnp.concatenate([w2[0], zpad], axis=1).astype(BF16)
    w2v = jnp.concatenate([zpad, w2[1]], axis=1).astype(BF16)
    consts = [wblk, peh, w2k, w2v]

    qp, cmp_p, gate_p, kv_t, win_t = _proj(xp.reshape(b * t, d), [w_q, w_cmp, w_gate], wts, b)
    r3 = lambda a: a.reshape(b, t, -1)
    kvc = _nsa_compress_prompt(r3(cmp_p), consts)
    nq = t // Q_TILE
    bias_c = _nsa_cmp_bias(rel_bias, t)
    n_rev = nq + NSA_WKEYS // Q_TILE
    rev = _bias_by_distance(rel_bias, Q_TILE * nq - 1 - np.arange(n_rev * Q_TILE))
    rev = jnp.transpose(rev.reshape(N_HEADS, n_rev, Q_TILE), (1, 0, 2))
    tiles = _nsa_bias_tiles(rev, nq, nq, NSA_PCHUNK)
    wtiles = _nsa_bias_tiles(rev, nq, NSA_WINDOW // Q_TILE + 1, NSA_WKEYS)
    attn_p = _nsa_prompt(r3(qp), r3(gate_p), kv_t, win_t, kvc, bias_c, tiles, wtiles)
    kv_p = _token_major(kv_t, (4, 1, HEAD_DIM))
    win_out_p = _token_major(win_t[:, :, t - min(NSA_WINDOW, t):], (2, 1, HEAD_DIM))

    qs, slc_s, win_s, gate_s, kvs_t, wins_t = _proj(xs.reshape(nb, d), [w_q, w_slc, w_win, w_gate], wts, 1)
    wb = cache_win.shape[1]
    n_h = past // CMP_STRIDE
    bc = _bias_by_distance(rel_bias, past - (np.arange(n_h) * CMP_STRIDE + CMP_BLOCK - 1))
    bs = _bias_by_distance(rel_bias, past - np.arange(past))
    bw = _bias_by_distance(rel_bias, wb - np.arange(wb))
    b0 = _bias_by_distance(rel_bias, np.zeros((128,), np.int64))
    gate_t = jnp.transpose(gate_s.reshape(nb, 3, N_HEADS), (0, 2, 1))
    new = jnp.stack([slc_s, win_s], axis=1)
    cwin_t = jnp.transpose(cache_win, (0, 2, 3, 4, 1)).reshape(nb, KV_PAIR, wb)
    attn_s, wout_t = _nsa_sample(qs.reshape(nb, N_HEADS, HEAD_DIM), gate_t, new, wins_t[0], cwin_t,
                                 [bc, bs, bw, b0], consts, _feature_major_pool(cache_kv), page_table)
    kv_s = jnp.transpose(kvs_t[0], (1, 0)).reshape(nb, 1, 4, 1, HEAD_DIM)
    win_out_s = _token_major(wout_t, (2, 1, HEAD_DIM))
    return attn_p.reshape(b * t, d), attn_s.reshape(nb, d), kv_p, kv_s, win_out_p, win_out_s


def kernel(x_prompt, x_sample, cache_nsa_kv, cache_nsa_win, cache_mla, state_dil_w128, state_dil_w512,
           state_dil_w2048, cache_sb_kv, page_table, p_prompt, p_sample, rel_bias, ln1_g, ln1_b, ln2_g, ln2_b,
           ffn_wg, ffn_wu, ffn_wd, ple_wg, ple_wp, nsa_w_in, nsa_cmp_pe, nsa_cmp_w1, nsa_cmp_w2, nsa_w_out,
           mla_w_dq, mla_q_norm, mla_kv_norm, mla_w_uq, mla_w_uk, mla_w_uv, mla_w_out, dil_w_in, dil_w_out,
           sb_w_in, sb_w_out):
    b, t, d = x_prompt.shape
    nb = x_sample.shape[0]
    past_len = page_table.shape[1] * PAGE
    depth = p_prompt.shape[0]
    n_mixers = 4
    dil_states = (state_dil_w128, state_dil_w512, state_dil_w2048)
    xp = x_prompt.reshape(b * t, d)
    xs = x_sample.reshape(nb, d)
    pp_all = p_prompt.reshape(depth, b * t, -1)
    ps_all = p_sample.reshape(depth, nb, -1)
    outs = {k: [] for k in ("nsa_kv_p", "nsa_kv_s", "nsa_win_p", "nsa_win_s", "mla_p", "mla_s", "sb_p", "sb_s")}
    dil_p = [[] for _ in DIL_PATTERNS]
    dil_s = [[] for _ in DIL_PATTERNS]
    for i in range(depth):
        kind, j = i % n_mixers, i // n_mixers
        xp3, xs3 = xp.reshape(b, t, d), xs.reshape(nb, 1, d)
        if kind == 0:
            mp, ms, a, b_, c, e = _nsa_layer(xp3, xs3, cache_nsa_kv[j], cache_nsa_win[j], page_table, nsa_w_in[j],
                                            nsa_cmp_pe[j], nsa_cmp_w1[j], nsa_cmp_w2[j], rel_bias)
            mp, ms, w_out = [mp], [ms], nsa_w_out[j]
            outs["nsa_kv_p"].append(a)
            outs["nsa_kv_s"].append(b_)
            outs["nsa_win_p"].append(c)
            outs["nsa_win_s"].append(e)
        elif kind == 1:
            mp, ms, a, b_ = _mla_layer(xp3, xs3, cache_mla[j], page_table, mla_w_dq[j], mla_q_norm[j], mla_kv_norm[j],
                                       mla_w_uq[j], mla_w_uk[j], mla_w_uv[j], past_len)
            mp, ms, w_out = [mp], [ms], mla_w_out[j]
            outs["mla_p"].append(a)
            outs["mla_s"].append(b_)
        elif kind == 2:
            mp, ms, st_p, st_s = _dil_layer(xp3, xs3, [s[j] for s in dil_states], dil_w_in[j], rel_bias)
            ms, w_out = [ms], dil_w_out[j]
            for g in range(len(DIL_PATTERNS)):
                dil_p[g].append(st_p[g])
                dil_s[g].append(st_s[g])
        else:
            mp, ms, a, b_ = _sb_layer(xp3, xs3, cache_sb_kv[j], page_table, sb_w_in[j])
            mp, ms, w_out = [mp], [ms], sb_w_out[j]
            outs["sb_p"].append(a)
            outs["sb_s"].append(b_)
        row = lambda v: v.reshape(1, -1)
        consts = (w_out.astype(BF16), row(ln1_g[i]), row(ln1_b[i]), row(ln2_g[i]), row(ln2_b[i]),
                  ffn_wg[i].astype(BF16), ffn_wu[i].astype(BF16), ffn_wd[i].astype(BF16),
                  ple_wg[i].astype(BF16), ple_wp[i].astype(BF16))
        xp = _tail(mp, xp, pp_all, i, *consts)
        xs = _tail(ms, xs, ps_all, i, *consts)
    st = jnp.stack
    return (xp.reshape(b, t, d), xs.reshape(nb, 1, d),
            st(outs["nsa_kv_p"]), st(outs["nsa_kv_s"]), st(outs["nsa_win_p"]), st(outs["nsa_win_s"]),
            st(outs["mla_p"]), st(outs["mla_s"]),
            st(dil_p[0]), st(dil_s[0]), st(dil_p[1]), st(dil_s[1]), st(dil_p[2]), st(dil_s[2]),
            st(outs["sb_p"]), st(outs["sb_s"]))
```

```python
import functools
import math

import numpy as np
import jax
import jax.numpy as jnp
from jax import lax
from jax.experimental import pallas as pl
from jax.experimental.pallas import tpu as pltpu

F32 = jnp.float32
BF16 = jnp.bfloat16

HEAD_DIM = 64
N_HEADS = 16
PAGE = 128
Q_TILE = 128
LN_EPS = 1e-5
NEG_INF = -1e30
DEPTH = 4
ALPHA = (2 * DEPTH) ** 0.25
N_BUCKETS = 32
MAX_DISTANCE = 2048
CMP_BLOCK = 32
CMP_STRIDE = 16
SLC_BLOCK = 64
N_SELECT = 16
NSA_WINDOW = 512
FORCE_BONUS = 1e4
DIL_PATTERNS = ((128, 1), (512, 4), (2048, 16))
ROPE_THETA = 10000.0
QK_ROPE = 32
KV_LORA = 256
Q_LORA = 256
SB_DEAD = -104.0
MASKED_ROW_FLOOR = -1e29
VMEM_LIMIT_BYTES = 56 * 1024 * 1024


def _cparams(*sem):
    return pltpu.CompilerParams(dimension_semantics=sem, vmem_limit_bytes=VMEM_LIMIT_BYTES)


def _dot(a, b):
    return jnp.dot(a, b, preferred_element_type=F32)


def _dot_nt(a, b):
    return lax.dot_general(a, b, (((1,), (1,)), ((), ())), preferred_element_type=F32)


def _const_spec(shape):
    nd = len(shape)
    return pl.BlockSpec(shape, lambda *_: (0,) * nd)


def _bucket_of_distance(n_dist):
    n = np.arange(n_dist, dtype=np.int64)
    max_exact = N_BUCKETS // 2
    ratio = np.maximum(n, max_exact).astype(np.float32) / np.float32(max_exact)
    log_ratio = np.log(ratio).astype(np.float32) / np.float32(math.log(MAX_DISTANCE / max_exact))
    large = np.minimum(max_exact + (log_ratio * np.float32(N_BUCKETS - max_exact)).astype(np.int32), N_BUCKETS - 1)
    return np.where(n < max_exact, n, large).astype(np.int32)


def _bias_by_distance(rel_bias, dists):
    d = np.asarray(dists)
    bucket = _bucket_of_distance(int(d.max()) + 1)[np.maximum(d, 0)]
    vals = jnp.moveaxis(rel_bias.astype(F32)[bucket], -1, 0)
    return jnp.where(jnp.asarray(d >= 0), vals, 0.0)


def _proj_body(x_ref, *refs, n_row, n_col):
    xb = x_ref[...].astype(BF16)
    n = n_row + n_col
    for w_ref, o_ref in zip(refs[:n_row], refs[n:n + n_row]):
        o_ref[...] = _dot(xb, w_ref[...])
    for w_ref, o_ref in zip(refs[n_row:n], refs[n + n_row:]):
        o_ref[0] = _dot_nt(w_ref[...], xb)


def _proj(x, ws, wts, n_batch):
    m, k = x.shape
    tm = _row_tile(m)
    t = m // n_batch
    per_b = t // tm
    return pl.pallas_call(
        functools.partial(_proj_body, n_row=len(ws), n_col=len(wts)),
        grid=(m // tm,),
        in_specs=[pl.BlockSpec((tm, k), lambda i: (i, 0))] + [_const_spec(w.shape) for w in ws + wts],
        out_specs=[pl.BlockSpec((tm, w.shape[1]), lambda i: (i, 0)) for w in ws]
        + [pl.BlockSpec((1, w.shape[0], tm), lambda i: (i // per_b, 0, i % per_b)) for w in wts],
        out_shape=[jax.ShapeDtypeStruct((m, w.shape[1]), F32) for w in ws]
        + [jax.ShapeDtypeStruct((n_batch, w.shape[0], t), F32) for w in wts],
        compiler_params=_cparams("parallel"),
        name="proj",
    )(x, *ws, *wts)


def _row_tile(m):
    return 256 if m % 256 == 0 else m


def _layer_norm(x, g, b):
    mu = jnp.mean(x, axis=-1, keepdims=True)
    xc = x - mu
    var = jnp.mean(xc * xc, axis=-1, keepdims=True)
    return xc * lax.rsqrt(var + LN_EPS) * g + b


FF_CHUNK = 256


def _tail_body(*refs, n_mix):
    mix_refs = refs[:n_mix]
    (x_ref, p_ref, wo_ref, g1_ref, b1_ref, g2_ref, b2_ref, wg_ref, wu_ref, wd_ref,
     pwg_ref, pwp_ref, o_ref) = refs[n_mix:]
    if n_mix == 1:
        attn = mix_refs[0][...]
    else:
        n_g = n_mix // 2
        lses = [r[...] for r in mix_refs[n_g:]]
        m = functools.reduce(jnp.maximum, lses)
        es = [jnp.exp(l - m) for l in lses]
        den = functools.reduce(lambda a, b: a + b, es)
        ws = [e / den for e in es]
        cols = []
        for h in range(N_HEADS):
            sl = slice(h * HEAD_DIM, (h + 1) * HEAD_DIM)
            acc = None
            for g in range(n_g):
                term = ws[g][:, h:h + 1] * mix_refs[g][:, sl]
                acc = term if acc is None else acc + term
            cols.append(acc)
        attn = jnp.concatenate(cols, axis=1)
    x = x_ref[...]
    mix = _dot(attn.astype(BF16), wo_ref[...])
    h1 = _layer_norm(ALPHA * x + mix, g1_ref[...], b1_ref[...])
    h1b = h1.astype(BF16)
    d_ff = wg_ref.shape[1]
    acc = jnp.zeros(x.shape, F32)
    for c in range(d_ff // FF_CHUNK):
        sl = slice(c * FF_CHUNK, (c + 1) * FF_CHUNK)
        g = _dot(h1b, wg_ref[:, sl])
        u = _dot(h1b, wu_ref[:, sl])
        acc = acc + _dot((g * jax.nn.sigmoid(g) * u).astype(BF16), wd_ref[sl, :])
    h2 = _layer_norm(ALPHA * h1 + acc, g2_ref[...], b2_ref[...])
    gate = jax.nn.sigmoid(_dot(h2.astype(BF16), pwg_ref[...]))
    o_ref[...] = h2 + gate * _dot(p_ref[...].astype(BF16), pwp_ref[...])


def _tail(mix_list, x, p_all, layer, wo, g1, b1, g2, b2, wg, wu, wd, pwg, pwp):
    m, d = x.shape
    tm = _row_tile(m)
    consts = [wo, g1, b1, g2, b2, wg, wu, wd, pwg, pwp]
    row = lambda a: pl.BlockSpec((tm, a.shape[1]), lambda i: (i, 0))
    single = lambda a: pl.BlockSpec(a.shape, lambda i: (0, 0), pipeline_mode=pl.Buffered(1))
    return pl.pallas_call(
        functools.partial(_tail_body, n_mix=len(mix_list)),
        grid=(m // tm,),
        in_specs=[row(a) for a in mix_list]
        + [row(x), pl.BlockSpec((None, tm, p_all.shape[2]), lambda i: (layer, i, 0))] + [single(a) for a in consts],
        out_specs=pl.BlockSpec((tm, d), lambda i: (i, 0)),
        out_shape=jax.ShapeDtypeStruct((m, d), F32),
        compiler_params=_cparams("parallel"),
        name="tail",
    )(*mix_list, x, p_all, *consts)


def _page_copies(pool_ref, buf_ref, sem_ref, pt_ref, b, slot, n_pages, start):
    def body(p, carry):
        if len(buf_ref.shape) == 4:
            dst = buf_ref.at[slot, p]
        else:
            rows = pool_ref.shape[1]
            dst = buf_ref.at[slot, pl.ds(pl.multiple_of(p * rows, rows), rows)]
        cp = pltpu.make_async_copy(pool_ref.at[pt_ref[b, p]], dst, sem_ref.at[slot])
        if start:
            cp.start()
        else:
            cp.wait()
        return carry
    lax.fori_loop(0, n_pages, body, 0)


def _paged_prefetch(pool_ref, buf_ref, sem_ref, pt_ref, n_pages):
    b = pl.program_id(0)
    nb = pl.num_programs(0)
    slot = b % 2

    @pl.when(b == 0)
    def _():
        _page_copies(pool_ref, buf_ref, sem_ref, pt_ref, 0, 0, n_pages, True)

    @pl.when(b + 1 < nb)
    def _():
        _page_copies(pool_ref, buf_ref, sem_ref, pt_ref, b + 1, 1 - slot, n_pages, True)

    _page_copies(pool_ref, buf_ref, sem_ref, pt_ref, b, slot, n_pages, False)
    return slot


def _sb_terms(z, valid):
    t = jnp.log(1.0 + jnp.exp(-jnp.abs(z)))
    l1m = -jnp.maximum(z, 0.0) - t
    if valid is not None:
        l1m = jnp.where(valid, l1m, 0.0)
    ls = jnp.minimum(z, 0.0) - t
    return l1m, ls


def _strict_upper_sum_matrix(n):
    j = lax.broadcasted_iota(jnp.int32, (2 * n, n), 0) & (n - 1)
    s = lax.broadcasted_iota(jnp.int32, (2 * n, n), 1)
    return (j > s).astype(BF16)


def _tail_sums(l1m, uu):
    hi = l1m.astype(BF16)
    lo = (l1m - hi.astype(F32)).astype(BF16)
    return _dot(jnp.concatenate([hi, lo], axis=1), uu)


SB_GROUPS_PER_LOOP = 2


def _sb_prompt_body(q_ref, kv_ref, o_ref, r_ref, acc_ref, *, n_kv, grp):
    qi = pl.program_id(1)
    tq = q_ref.shape[1]
    rows = grp * tq
    uu = _strict_upper_sum_matrix(Q_TILE)
    qpos = qi * tq + (lax.broadcasted_iota(jnp.int32, (rows, Q_TILE), 0) & (tq - 1))
    lane = lax.broadcasted_iota(jnp.int32, (rows, Q_TILE), 1)
    for g0 in range(0, n_kv, SB_GROUPS_PER_LOOP):
        groups = range(g0, g0 + SB_GROUPS_PER_LOOP)
        qgs = []
        for g in groups:
            qg = jnp.concatenate(
                [q_ref[0, :, (g * grp + u) * HEAD_DIM:(g * grp + u + 1) * HEAD_DIM] for u in range(grp)], axis=0)
            qgs.append((qg * (HEAD_DIM ** -0.5)).astype(BF16))
        r_ref[...] = jnp.zeros(r_ref.shape, F32)
        acc_ref[...] = jnp.zeros(acc_ref.shape, F32)

        def cond(c):
            return jnp.logical_and(c[0] >= 0, c[1] > SB_DEAD)

        def body(c):
            kj = c[0]
            off = pl.multiple_of(kj * Q_TILE, Q_TILE)
            valid = (kj * Q_TILE + lane) < qpos
            alive = jnp.float32(-jnp.inf)
            for i, g in enumerate(groups):
                part = slice(i * rows, (i + 1) * rows)
                k = kv_ref[0, g * HEAD_DIM:(g + 1) * HEAD_DIM, pl.ds(off, Q_TILE)].astype(BF16)
                v = kv_ref[0, (n_kv + g) * HEAD_DIM:(n_kv + g + 1) * HEAD_DIM, pl.ds(off, Q_TILE)].astype(BF16)
                l1m, ls = _sb_terms(_dot(qgs[i], k), valid)
                r = r_ref[part, :]
                a = jnp.where(valid, jnp.exp(ls + _tail_sums(l1m, uu) + r), 0.0)
                acc_ref[part, :] += _dot_nt(a.astype(BF16), v)
                rn = r + jnp.sum(l1m, axis=1, keepdims=True)
                r_ref[part, :] = rn
                alive = jnp.maximum(alive, jnp.max(rn))
            return kj - 1, alive

        lax.while_loop(cond, body, (qi, jnp.float32(0.0)))
        for i, g in enumerate(groups):
            for u in range(grp):
                h = g * grp + u
                o_ref[0, :, h * HEAD_DIM:(h + 1) * HEAD_DIM] = acc_ref[i * rows + u * tq:i * rows + (u + 1) * tq, :]


def _sb_prompt(q, kv, n_kv):
    b, t, d = q.shape
    grp = N_HEADS // n_kv
    assert n_kv % SB_GROUPS_PER_LOOP == 0
    rows = SB_GROUPS_PER_LOOP * grp * Q_TILE
    return pl.pallas_call(
        functools.partial(_sb_prompt_body, n_kv=n_kv, grp=grp),
        grid=(b, t // Q_TILE),
        in_specs=[pl.BlockSpec((1, Q_TILE, d), lambda i, j: (i, j, 0)),
                  pl.BlockSpec((1, kv.shape[1], t), lambda i, j: (i, 0, 0))],
        out_specs=pl.BlockSpec((1, Q_TILE, d), lambda i, j: (i, j, 0)),
        out_shape=jax.ShapeDtypeStruct((b, t, d), F32),
        scratch_shapes=[pltpu.VMEM((rows, 1), F32), pltpu.VMEM((rows, HEAD_DIM), F32)],
        compiler_params=_cparams("parallel", "arbitrary"),
        name="sb_prompt",
    )(q, kv)


def _head_spread(n_kv):
    d = lax.broadcasted_iota(jnp.int32, (HEAD_DIM, n_kv * HEAD_DIM), 0)
    c = lax.broadcasted_iota(jnp.int32, (HEAD_DIM, n_kv * HEAD_DIM), 1)
    return ((c & (HEAD_DIM - 1)) == d).astype(BF16)


def _block_diag_q(q, n_kv, scale):
    grp = N_HEADS // n_kv
    width = n_kv * HEAD_DIM
    spread = _dot((q * scale).astype(BF16), _head_spread(n_kv))
    row = lax.broadcasted_iota(jnp.int32, (N_HEADS, width), 0)
    col = lax.broadcasted_iota(jnp.int32, (N_HEADS, width), 1)
    own = (col >> 6) == (row >> int(math.log2(grp)))
    return jnp.where(own, spread, 0.0).astype(BF16)


def _block_diag_pick(o_full, n_kv):
    grp = N_HEADS // n_kv
    row = lax.broadcasted_iota(jnp.int32, (N_HEADS, HEAD_DIM), 0)
    out = jnp.zeros((N_HEADS, HEAD_DIM), F32)
    for g in range(n_kv):
        out = out + jnp.where((row >> int(math.log2(grp))) == g, o_full[:, g * HEAD_DIM:(g + 1) * HEAD_DIM], 0.0)
    return out


SB_EAGER_PAGES = 4


def _sb_sample_body(pt_ref, q_ref, pool_ref, o_ref, buf_ref, sem_ref, old_ref, old_sem, *, n_kv, n_pages):
    b = pl.program_id(0)
    nb = pl.num_programs(0)
    slot = b % 2
    n_eager = buf_ref.shape[1]
    first = n_pages - n_eager

    def eager(bb, sl, start):
        for i in range(n_eager):
            cp = pltpu.make_async_copy(pool_ref.at[pt_ref[bb, first + i]], buf_ref.at[sl, i], sem_ref.at[sl])
            if start:
                cp.start()
            else:
                cp.wait()

    @pl.when(b == 0)
    def _():
        eager(0, 0, True)

    @pl.when(b + 1 < nb)
    def _():
        eager(b + 1, 1 - slot, True)

    eager(b, slot, False)
    width = n_kv * HEAD_DIM
    qbd = _block_diag_q(q_ref[0], n_kv, HEAD_DIM ** -0.5)
    uu = _strict_upper_sum_matrix(PAGE)

    def step(page, r, acc):
        l1m, ls = _sb_terms(_dot(qbd, page[0:width, :].astype(BF16)), None)
        a = jnp.exp(ls + _tail_sums(l1m, uu) + r)
        acc = acc + _dot_nt(a.astype(BF16), page[width:2 * width, :].astype(BF16))
        rn = r + jnp.sum(l1m, axis=1, keepdims=True)
        return jnp.max(rn), rn, acc

    def cond(c):
        return jnp.logical_and(c[0] >= 0, c[1] > SB_DEAD)

    def newest(c):
        i, _, r, acc = c
        return (i - 1,) + step(buf_ref[slot, i], r, acc)

    def older(c):
        p, _, r, acc = c
        cp = pltpu.make_async_copy(pool_ref.at[pt_ref[b, p]], old_ref, old_sem.at[0])
        cp.start()
        cp.wait()
        return (p - 1,) + step(old_ref[...], r, acc)

    init = (jnp.int32(n_eager - 1), jnp.float32(0.0), jnp.zeros((N_HEADS, 1), F32), jnp.zeros((N_HEADS, width), F32))
    _, alive, r, acc = lax.while_loop(cond, newest, init)
    acc = lax.while_loop(cond, older, (jnp.int32(first - 1), alive, r, acc))[3]
    o_ref[0] = _block_diag_pick(acc, n_kv)


def _sb_sample(q, pool, page_table, n_kv):
    nb = q.shape[0]
    n_pages = page_table.shape[1]
    feat = pool.shape[1]
    n_eager = min(SB_EAGER_PAGES, n_pages)
    grid_spec = pltpu.PrefetchScalarGridSpec(
        num_scalar_prefetch=1,
        grid=(nb,),
        in_specs=[pl.BlockSpec((1, N_HEADS, HEAD_DIM), lambda i, pt: (i, 0, 0)),
                  pl.BlockSpec(memory_space=pl.ANY)],
        out_specs=pl.BlockSpec((1, N_HEADS, HEAD_DIM), lambda i, pt: (i, 0, 0)),
        scratch_shapes=[pltpu.VMEM((2, n_eager, feat, PAGE), F32), pltpu.SemaphoreType.DMA((2,)),
                        pltpu.VMEM((feat, PAGE), F32), pltpu.SemaphoreType.DMA((1,))],
    )
    return pl.pallas_call(
        functools.partial(_sb_sample_body, n_kv=n_kv, n_pages=n_pages),
        grid_spec=grid_spec,
        out_shape=jax.ShapeDtypeStruct((nb, N_HEADS, HEAD_DIM), F32),
        compiler_params=_cparams("arbitrary"),
        name="sb_sample",
    )(page_table, q, pool)


def _feature_major_pool(cache):
    n = cache.ndim
    return jnp.transpose(cache, (0,) + tuple(range(2, n)) + (1,)).reshape(cache.shape[0], -1, cache.shape[1])


def _token_major(x_t, feat_shape):
    b, _, t = x_t.shape
    nf = len(feat_shape)
    return jnp.transpose(x_t.reshape((b,) + tuple(feat_shape) + (t,)), (0, nf + 1) + tuple(range(1, nf + 1)))


def _sb_layer(xp, xs, cache, page_table, w_in):
    b, t, d = xp.shape
    n_kv = (w_in.shape[1] - d) // (2 * HEAD_DIM)
    ws, wts = [w_in[:, :d].astype(BF16)], [w_in[:, d:].T.astype(BF16)]
    qp, kvp_t = _proj(xp.reshape(b * t, d), ws, wts, b)
    attn_p = _sb_prompt(qp.reshape(b, t, d), kvp_t, n_kv)
    nb = xs.shape[0]
    qs, kvs_t = _proj(xs.reshape(nb, d), ws, wts, 1)
    attn_s = _sb_sample(qs.reshape(nb, N_HEADS, HEAD_DIM), _feature_major_pool(cache), page_table, n_kv)
    kv_s = _token_major(kvs_t, (2, n_kv, HEAD_DIM)).reshape(nb, 1, 2, n_kv, HEAD_DIM)
    return attn_p.reshape(b * t, d), attn_s.reshape(nb, d), _token_major(kvp_t, (2, n_kv, HEAD_DIM)), kv_s


def _toeplitz(base_row, rows, shift):
    return pltpu.roll(jnp.broadcast_to(base_row, (rows, base_row.shape[1])), shift, 1, stride=1, stride_axis=0)


def _dil_prompt_body(q_ref, kvc_ref, kvp_ref, base_ref, o_ref, lse_ref, *, n_kv, grp):
    mi = pl.program_id(2)
    tq = Q_TILE
    i_idx = lax.broadcasted_iota(jnp.int32, (tq, 2 * tq), 0)
    j_idx = lax.broadcasted_iota(jnp.int32, (tq, 2 * tq), 1)
    steps = i_idx - j_idx + tq
    mask = jnp.where((steps >= 0) & (steps <= tq) & ((mi > 0) | (j_idx >= tq)), 0.0, NEG_INF)
    lane = lax.broadcasted_iota(jnp.int32, (tq, 128), 1)
    lse_tile = jnp.zeros((tq, 128), F32)
    width = n_kv * HEAD_DIM
    for g in range(n_kv):
        ksl = slice(g * HEAD_DIM, (g + 1) * HEAD_DIM)
        vsl = slice(width + g * HEAD_DIM, width + (g + 1) * HEAD_DIM)
        k = jnp.concatenate([kvp_ref[0, :, ksl], kvc_ref[0, :, ksl]], axis=0).astype(BF16)
        v = jnp.concatenate([kvp_ref[0, :, vsl], kvc_ref[0, :, vsl]], axis=0).astype(BF16)
        heads = range(g * grp, (g + 1) * grp)
        qg = jnp.concatenate([q_ref[0, :, h * HEAD_DIM:(h + 1) * HEAD_DIM] for h in heads], axis=0)
        bias = jnp.stack([_toeplitz(base_ref[h:h + 1, :], tq, 0) for h in heads])
        s = _dot_nt((qg * (HEAD_DIM ** -0.5)).astype(BF16), k)
        s = (s.reshape(grp, tq, 2 * tq) + bias + mask[None]).reshape(grp * tq, 2 * tq)
        m = jnp.maximum(jnp.max(s, axis=1, keepdims=True), MASKED_ROW_FLOOR)
        e = jnp.exp(s - m)
        den = jnp.maximum(jnp.sum(e, axis=1, keepdims=True), 1e-30)
        o = _dot((e / den).astype(BF16), v)
        lse = m + jnp.log(den)
        for u, h in enumerate(heads):
            o_ref[0, :, h * HEAD_DIM:(h + 1) * HEAD_DIM] = o[u * tq:(u + 1) * tq, :]
            lse_tile = jnp.where(lane == h, lse[u * tq:(u + 1) * tq, :], lse_tile)
    lse_ref[0] = lse_tile


def _dil_prompt(q, kv, base, dil, n_kv):
    b, tm, _ = q.shape
    d = N_HEADS * HEAD_DIM
    kvw = 2 * n_kv * HEAD_DIM
    nm = tm // Q_TILE
    return pl.pallas_call(
        functools.partial(_dil_prompt_body, n_kv=n_kv, grp=N_HEADS // n_kv),
        grid=(b, dil, nm),
        in_specs=[pl.BlockSpec((1, Q_TILE, d), lambda i, r, m: (i, m, r)),
                  pl.BlockSpec((1, Q_TILE, kvw), lambda i, r, m: (i, m, r)),
                  pl.BlockSpec((1, Q_TILE, kvw), lambda i, r, m: (i, jnp.maximum(m - 1, 0), r)),
                  _const_spec(base.shape)],
        out_specs=[pl.BlockSpec((1, Q_TILE, d), lambda i, r, m: (i, m, r)),
                   pl.BlockSpec((1, Q_TILE, 128), lambda i, r, m: (i, m, r))],
        out_shape=[jax.ShapeDtypeStruct((b, tm, dil * d), F32), jax.ShapeDtypeStruct((b, tm, dil * 128), F32)],
        compiler_params=_cparams("parallel", "parallel", "arbitrary"),
        name="dil_prompt",
    )(q, kv, kv, base)


def _dil_sample_body(q_ref, kvn_ref, kvnt_ref, b0_ref, bm0_ref, bm1_ref, bm2_ref, st0_ref, st1_ref, st2_ref,
                     o_ref, so0_ref, so1_ref, so2_ref, *, n_kv):
    b = pl.program_id(0)
    width = n_kv * HEAD_DIM
    nb = kvnt_ref.shape[2]
    pick = lax.broadcasted_iota(jnp.int32, (2 * width, nb), 1) == b
    outs, lses = [], []
    groups = zip((st0_ref, st1_ref, st2_ref), (so0_ref, so1_ref, so2_ref), (bm0_ref, bm1_ref, bm2_ref))
    for g, (st, so, bm) in enumerate(groups):
        w = st.shape[2]
        kn = kvn_ref[0, g:g + 1, 0:width].astype(BF16).astype(F32)
        vn = kvn_ref[0, g:g + 1, width:2 * width].astype(BF16).astype(F32)
        qbd = _block_diag_q(q_ref[0, g], n_kv, HEAD_DIM ** -0.5)
        s_old = _dot(qbd, st[0, 0:width, :].astype(BF16)) + bm[...]
        s_new = jnp.sum(qbd.astype(F32) * kn, axis=1, keepdims=True) + b0_ref[:, 0:1]
        m = jnp.maximum(jnp.max(s_old, axis=1, keepdims=True), s_new)
        e_old = jnp.exp(s_old - m)
        e_new = jnp.exp(s_new - m)
        den = jnp.sum(e_old, axis=1, keepdims=True) + e_new
        o_full = (_dot_nt((e_old / den).astype(BF16), st[0, width:2 * width, :].astype(BF16))
                  + (e_new / den).astype(BF16).astype(F32) * vn)
        outs.append(_block_diag_pick(o_full, n_kv))
        lses.append(m + jnp.log(den))
        new_col = jnp.sum(jnp.where(pick, kvnt_ref[g], 0.0), axis=1, keepdims=True)
        last = lax.broadcasted_iota(jnp.int32, (2 * width, w), 1) == w - 1
        so[0] = jnp.where(last, new_col, pltpu.roll(st[0], w - 1, 1))
    m = functools.reduce(jnp.maximum, lses)
    es = [jnp.exp(l - m) for l in lses]
    den = functools.reduce(lambda a, b: a + b, es)
    o = None
    for e, og in zip(es, outs):
        o = (e / den) * og if o is None else o + (e / den) * og
    o_ref[0] = o


def _dil_sample(q, kvn, kvn_t, b0, bias_masks, states, n_kv):
    nb = q.shape[0]
    st_specs = [pl.BlockSpec((1,) + s.shape[1:], lambda i: (i, 0, 0)) for s in states]
    outs = pl.pallas_call(
        functools.partial(_dil_sample_body, n_kv=n_kv),
        grid=(nb,),
        in_specs=[pl.BlockSpec((1,) + q.shape[1:], lambda i: (i, 0, 0, 0)),
                  pl.BlockSpec((1,) + kvn.shape[1:], lambda i: (i, 0, 0)),
                  _const_spec(kvn_t.shape), _const_spec(b0.shape)]
        + [_const_spec(bm.shape) for bm in bias_masks] + st_specs,
        out_specs=[pl.BlockSpec((1, N_HEADS, HEAD_DIM), lambda i: (i, 0, 0))] + st_specs,
        out_shape=[jax.ShapeDtypeStruct((nb, N_HEADS, HEAD_DIM), F32)]
        + [jax.ShapeDtypeStruct(s.shape, F32) for s in states],
        compiler_params=_cparams("parallel"),
        name="dil_sample",
    )(q, kvn, kvn_t, b0, *bias_masks, *states)
    return outs[0], outs[1:]


def _dil_layer(xp, xs, states, w_in, rel_bias):
    b, t, d = xp.shape
    n_g = len(DIL_PATTERNS)
    w3 = w_in.reshape(d, n_g, -1)
    kvw = w3.shape[2] - d
    n_kv = kvw // (2 * HEAD_DIM)
    ws, wts = [], []
    for g in range(n_g):
        ws += [w3[:, g, :d].astype(BF16), w3[:, g, d:].astype(BF16)]
        wts.append(w3[:, g, d:].T.astype(BF16))
    outs_p = _proj(xp.reshape(b * t, d), ws, wts, b)
    lane = np.arange(2 * Q_TILE)
    mix, st_p = [], []
    for g, (w, dil) in enumerate(DIL_PATTERNS):
        assert w // dil == Q_TILE and t % (dil * Q_TILE) == 0
        base = _bias_by_distance(rel_bias, np.where(lane <= Q_TILE, (Q_TILE - lane) * dil, -1))
        o_g, lse_g = _dil_prompt(outs_p[2 * g].reshape(b, t // dil, dil * d),
                                 outs_p[2 * g + 1].reshape(b, t // dil, dil * kvw), base, dil, n_kv)
        mix.append((o_g.reshape(b * t, d), lse_g.reshape(b * t, 128)))
        st_p.append(_token_major(outs_p[2 * n_g + g][:, :, t - min(w, t):], (2, n_kv, HEAD_DIM)))
    mix_p = [m[0] for m in mix] + [m[1] for m in mix]
    nb = xs.shape[0]
    outs_s = _proj(xs.reshape(nb, d), ws, wts, 1)
    q_s = jnp.stack([outs_s[2 * g].reshape(nb, N_HEADS, HEAD_DIM) for g in range(n_g)], axis=1)
    kvn = jnp.stack([outs_s[2 * g + 1] for g in range(n_g)], axis=1)
    kvn_t = jnp.concatenate(outs_s[2 * n_g:], axis=0)
    bias_masks = []
    for g, (w, dil) in enumerate(DIL_PATTERNS):
        pos = np.arange(states[g].shape[1])
        bias = _bias_by_distance(rel_bias, np.where(pos % dil == 0, w - pos, -1))
        bias_masks.append(jnp.where(jnp.asarray(pos % dil == 0), bias, NEG_INF))
    b0 = _bias_by_distance(rel_bias, np.zeros((128,), np.int64))
    st_t = [jnp.transpose(s, (0, 2, 3, 4, 1)).reshape(nb, kvw, s.shape[1]) for s in states]
    attn_s, st_s = _dil_sample(q_s, kvn, kvn_t, b0, bias_masks, st_t, n_kv)
    st_s = [_token_major(s, (2, n_kv, HEAD_DIM)) for s in st_s]
    return mix_p, attn_s.reshape(nb, d), st_p, st_s


HEAD_PAD = 128
ROPE_HALF = QK_ROPE // 2
MLA_SCALE = (HEAD_DIM + QK_ROPE) ** -0.5


def _rms(x, g):
    return x * lax.rsqrt(jnp.mean(x * x, axis=-1, keepdims=True) + LN_EPS) * g


def _mla_project_body(x_ref, c_ref, s_ref, wdq_ref, qn_ref, kvn_ref, wq_ref, wqs_ref, wk_ref, wv_ref, vone_ref,
                      q_ref, k_ref, v_ref, lat_ref, latt_ref):
    h = _dot(x_ref[...].astype(BF16), wdq_ref[...])
    cq = _rms(h[:, 0:Q_LORA], qn_ref[...]).astype(BF16)
    ckv = _rms(h[:, Q_LORA:Q_LORA + KV_LORA], kvn_ref[...])
    ckvb = ckv.astype(BF16)
    cos, sin = c_ref[...], s_ref[...]
    base = Q_LORA + KV_LORA
    kr = h[:, base:base + HEAD_PAD] * cos + h[:, base + HEAD_PAD:base + 2 * HEAD_PAD] * sin
    cos_all = jnp.concatenate([cos] * N_HEADS, axis=1)
    sin_all = jnp.concatenate([sin] * N_HEADS, axis=1)
    q_ref[...] = ((_dot(cq, wq_ref[...]) * cos_all + _dot(cq, wqs_ref[...]) * sin_all) * MLA_SCALE).astype(BF16)
    k_ref[...] = (_dot(ckvb, wk_ref[...]) + jnp.concatenate([kr] * N_HEADS, axis=1)).astype(BF16)
    v_ref[...] = (_dot(ckvb, wv_ref[...]) + vone_ref[...]).astype(BF16)
    lat_ref[:, 0:KV_LORA] = ckv
    lat_ref[:, KV_LORA:KV_LORA + QK_ROPE] = kr[:, HEAD_DIM:HEAD_DIM + QK_ROPE]
    latt_ref[0, 0:KV_LORA, :] = ckv.T
    latt_ref[0, KV_LORA:KV_LORA + QK_ROPE, :] = kr.T[HEAD_DIM:HEAD_DIM + QK_ROPE, :]


def _mla_project(x, cos, sin, consts, n_batch):
    m, d = x.shape
    tm = _row_tile(m)
    t = m // n_batch
    per_b = t // tm
    pos_blocks = cos.shape[0] // tm
    lat_w = KV_LORA + QK_ROPE
    wide = N_HEADS * HEAD_PAD
    rope_spec = pl.BlockSpec((tm, HEAD_PAD), lambda i: (i % pos_blocks, 0))
    return pl.pallas_call(
        _mla_project_body,
        grid=(m // tm,),
        in_specs=[pl.BlockSpec((tm, d), lambda i: (i, 0)), rope_spec, rope_spec] + [_const_spec(c.shape) for c in consts],
        out_specs=[pl.BlockSpec((tm, wide), lambda i: (i, 0)), pl.BlockSpec((tm, wide), lambda i: (i, 0)),
                   pl.BlockSpec((tm, wide), lambda i: (i, 0)), pl.BlockSpec((tm, lat_w), lambda i: (i, 0)),
                   pl.BlockSpec((1, lat_w, tm), lambda i: (i // per_b, 0, i % per_b))],
        out_shape=[jax.ShapeDtypeStruct((m, wide), BF16), jax.ShapeDtypeStruct((m, wide), BF16),
                   jax.ShapeDtypeStruct((m, wide), BF16), jax.ShapeDtypeStruct((m, lat_w), F32),
                   jax.ShapeDtypeStruct((n_batch, lat_w, t), F32)],
        compiler_params=_cparams("parallel"),
        name="mla_project",
    )(x, cos, sin, *consts)


MLA_TQ = 256
MLA_CHUNK = 1024
MLA_HEADS_PER_STEP = 4


def _mla_prompt_body(q_ref, k_ref, v_ref, o_ref, m_ref, acc_ref):
    qi = pl.program_id(2)
    tq, ch = MLA_TQ, MLA_CHUNK
    q0 = qi * tq
    m_ref[...] = jnp.full(m_ref.shape, NEG_INF, F32)
    acc_ref[...] = jnp.zeros(acc_ref.shape, F32)
    i_idx = lax.broadcasted_iota(jnp.int32, (tq, ch), 0)
    j_idx = lax.broadcasted_iota(jnp.int32, (tq, ch), 1)

    def chunk(c, carry):
        k0 = pl.multiple_of(c * ch, ch)
        mask = jnp.where(q0 + i_idx >= k0 + j_idx, 0.0, NEG_INF)
        heads = range(MLA_HEADS_PER_STEP)
        rows = [slice(hh * tq, (hh + 1) * tq) for hh in heads]
        lanes = [slice(hh * HEAD_PAD, (hh + 1) * HEAD_PAD) for hh in heads]
        m_old = [m_ref[r, :] for r in rows]
        acc_old = [acc_ref[r, :] for r in rows]
        m_new, acc_new = [], []
        for hh in heads:
            s = _dot_nt(q_ref[0, :, lanes[hh]], k_ref[0, pl.ds(k0, ch), lanes[hh]]) + mask
            mn = jnp.maximum(m_old[hh], jnp.max(s, axis=1, keepdims=True))
            p = jnp.exp(s - jnp.maximum(mn, MASKED_ROW_FLOOR))
            m_new.append(mn)
            acc_new.append(jnp.exp(m_old[hh] - mn) * acc_old[hh] + _dot(p.astype(BF16), v_ref[0, pl.ds(k0, ch), lanes[hh]]))
        for hh in heads:
            m_ref[rows[hh], :] = m_new[hh]
            acc_ref[rows[hh], :] = acc_new[hh]
        return carry

    lax.fori_loop(0, (q0 + tq + ch - 1) // ch, chunk, 0)
    for hh in range(MLA_HEADS_PER_STEP):
        acc = acc_ref[hh * tq:(hh + 1) * tq, :]
        o_ref[0, :, hh * HEAD_DIM:(hh + 1) * HEAD_DIM] = acc[:, 0:HEAD_DIM] / jnp.maximum(acc[:, HEAD_DIM:HEAD_DIM + 1], 1e-30)


def _mla_prompt(q, k, v):
    b, t, _ = q.shape
    tq = min(MLA_TQ, t)
    assert tq == MLA_TQ and t % MLA_CHUNK == 0
    nh = MLA_HEADS_PER_STEP
    pair = nh * HEAD_PAD
    return pl.pallas_call(
        _mla_prompt_body,
        grid=(b, N_HEADS // nh, t // tq),
        in_specs=[pl.BlockSpec((1, tq, pair), lambda i, h, j: (i, j, h)),
                  pl.BlockSpec((1, t, pair), lambda i, h, j: (i, 0, h)),
                  pl.BlockSpec((1, t, pair), lambda i, h, j: (i, 0, h))],
        out_specs=pl.BlockSpec((1, tq, nh * HEAD_DIM), lambda i, h, j: (i, j, h)),
        out_shape=jax.ShapeDtypeStruct((b, t, N_HEADS * HEAD_DIM), F32),
        scratch_shapes=[pltpu.VMEM((nh * tq, 1), F32), pltpu.VMEM((nh * tq, HEAD_PAD), F32)],
        compiler_params=_cparams("parallel", "parallel", "arbitrary"),
        name="mla_prompt",
    )(q, k, v)


def _mla_absorb_body(q_ref, wuk_ref, o_ref):
    for h in range(N_HEADS):
        qn = q_ref[:, h * HEAD_PAD:h * HEAD_PAD + HEAD_DIM]
        o_ref[:, h * KV_LORA:(h + 1) * KV_LORA] = _dot_nt(qn, wuk_ref[:, h * HEAD_DIM:(h + 1) * HEAD_DIM])


def _mla_unabsorb_body(o_ref, wuv_ref, y_ref):
    for h in range(N_HEADS):
        y_ref[:, h * HEAD_DIM:(h + 1) * HEAD_DIM] = _dot(
            o_ref[:, h * KV_LORA:(h + 1) * KV_LORA].astype(BF16), wuv_ref[:, h * HEAD_DIM:(h + 1) * HEAD_DIM])


def _whole_call(body, out_shape, name, *args):
    return pl.pallas_call(
        body, grid=(1,),
        in_specs=[_const_spec(a.shape) for a in args],
        out_specs=_const_spec(out_shape.shape),
        out_shape=out_shape, compiler_params=_cparams("arbitrary"), name=name)(*args)


def _mla_sample_body(pt_ref, ql_ref, q_ref, new_ref, pool_ref, o_ref, buf_ref, sem_ref, ckv_ref, kr_ref, *, n_pages):
    slot = _paged_prefetch(pool_ref, buf_ref, sem_ref, pt_ref, n_pages)
    ql = ql_ref[0].astype(BF16)
    qr = q_ref[0, :, HEAD_DIM:HEAD_DIM + QK_ROPE]
    new_c = new_ref[0, :, 0:KV_LORA].astype(BF16).astype(F32)
    new_r = new_ref[0, :, KV_LORA:KV_LORA + QK_ROPE].astype(BF16).astype(F32)
    s_new = (jnp.sum(ql.astype(F32) * new_c, axis=1, keepdims=True)
             + jnp.sum(qr.astype(F32) * new_r, axis=1, keepdims=True))
    for p in range(n_pages):
        ckv_ref[:, p * PAGE:(p + 1) * PAGE] = buf_ref[slot, p, 0:KV_LORA, :].astype(BF16)
        kr_ref[:, p * PAGE:(p + 1) * PAGE] = buf_ref[slot, p, KV_LORA:KV_LORA + QK_ROPE, :].astype(BF16)
    s = _dot(ql, ckv_ref[...]) + _dot(qr, kr_ref[...])
    m = jnp.maximum(jnp.max(s, axis=1, keepdims=True), s_new)
    p_old = jnp.exp(s - m)
    p_new = jnp.exp(s_new - m)
    den = jnp.sum(p_old, axis=1, keepdims=True) + p_new
    acc = _dot_nt(p_old.astype(BF16), ckv_ref[...]) + p_new * new_c
    o_ref[0] = acc / den


def _mla_sample(ql, q, lat_new, pool, page_table):
    nb = ql.shape[0]
    n_pages = page_table.shape[1]
    feat = pool.shape[1]
    grid_spec = pltpu.PrefetchScalarGridSpec(
        num_scalar_prefetch=1,
        grid=(nb,),
        in_specs=[pl.BlockSpec((1, N_HEADS, KV_LORA), lambda i, pt: (i, 0, 0)),
                  pl.BlockSpec((1, N_HEADS, HEAD_PAD), lambda i, pt: (i, 0, 0)),
                  pl.BlockSpec((1, 1, lat_new.shape[2]), lambda i, pt: (i, 0, 0)),
                  pl.BlockSpec(memory_space=pl.ANY)],
        out_specs=pl.BlockSpec((1, N_HEADS, KV_LORA), lambda i, pt: (i, 0, 0)),
        scratch_shapes=[pltpu.VMEM((2, n_pages, feat, PAGE), F32), pltpu.SemaphoreType.DMA((2,)),
                        pltpu.VMEM((KV_LORA, n_pages * PAGE), BF16), pltpu.VMEM((QK_ROPE, n_pages * PAGE), BF16)],
    )
    return pl.pallas_call(
        functools.partial(_mla_sample_body, n_pages=n_pages),
        grid_spec=grid_spec,
        out_shape=jax.ShapeDtypeStruct((nb, N_HEADS, KV_LORA), F32),
        compiler_params=_cparams("arbitrary"),
        name="mla_sample",
    )(page_table, ql, q, lat_new, pool)


def _pad_heads(w, parts):
    out = jnp.zeros((w.shape[0], N_HEADS, HEAD_PAD), w.dtype)
    for src, size, dst in parts:
        out = out.at[:, :, dst:dst + size].set(w[:, :, src:src + size])
    return out.reshape(w.shape[0], N_HEADS * HEAD_PAD)


def _mla_layer(xp, xs, cache, page_table, w_dq, q_norm, kv_norm, w_uq, w_uk, w_uv, past_len):
    b, t, d = xp.shape
    nb = xs.shape[0]
    r0, r1 = HEAD_DIM, HEAD_DIM + ROPE_HALF
    keep = [(0, HEAD_DIM, 0), (HEAD_DIM, ROPE_HALF, r0), (r1, ROPE_HALF, r1)]
    swap = [(r1, ROPE_HALF, r0), (HEAD_DIM, ROPE_HALF, r1)]
    base = Q_LORA + KV_LORA
    x1w, x2w = w_dq[:, base:base + ROPE_HALF], w_dq[:, base + ROPE_HALF:base + QK_ROPE]
    z_lo, z_hi = jnp.zeros((d, HEAD_DIM), F32), jnp.zeros((d, HEAD_PAD - HEAD_DIM - QK_ROPE), F32)
    kr_keep = jnp.concatenate([z_lo, x1w, x2w, z_hi], axis=1)
    kr_swap = jnp.concatenate([z_lo, x2w, x1w, z_hi], axis=1)
    wdq = jnp.concatenate([w_dq[:, :base], kr_keep, kr_swap], axis=1).astype(BF16)
    wq3 = w_uq.reshape(Q_LORA, N_HEADS, HEAD_DIM + QK_ROPE)
    wq = _pad_heads(wq3, keep).astype(BF16)
    wqs = _pad_heads(wq3, swap).astype(BF16)
    wk = _pad_heads(w_uk, [(0, HEAD_DIM, 0)]).astype(BF16)
    wv = w_uv.reshape(KV_LORA, N_HEADS * HEAD_DIM).astype(BF16)
    wv_pad = _pad_heads(w_uv, [(0, HEAD_DIM, 0)]).astype(BF16)
    vone = jnp.asarray((np.arange(N_HEADS * HEAD_PAD) % HEAD_PAD == HEAD_DIM).astype(np.float32)).reshape(1, -1)
    consts = [wdq, q_norm.reshape(1, -1), kv_norm.reshape(1, -1), wq, wqs, wk, wv_pad, vone]

    def rope_tables(pos):
        inv = ROPE_THETA ** (-jnp.arange(ROPE_HALF, dtype=F32) / ROPE_HALF)
        ang = pos.astype(F32)[:, None] * inv[None, :]
        cos, sin = jnp.cos(ang), jnp.sin(ang)
        n = pos.shape[0]
        c = jnp.concatenate([jnp.ones((n, HEAD_DIM), F32), cos, cos, jnp.zeros((n, HEAD_PAD - r1 - ROPE_HALF), F32)], axis=1)
        s = jnp.concatenate([jnp.zeros((n, HEAD_DIM), F32), -sin, sin, jnp.zeros((n, HEAD_PAD - r1 - ROPE_HALF), F32)], axis=1)
        return c, s

    cos_p, sin_p = rope_tables(jnp.arange(t))
    q, k, v, _, lat_p_t = _mla_project(xp.reshape(b * t, d), cos_p, sin_p, consts, b)
    wide = N_HEADS * HEAD_PAD
    attn_p = _mla_prompt(q.reshape(b, t, wide), k.reshape(b, t, wide), v.reshape(b, t, wide))
    cos_s, sin_s = rope_tables(jnp.full((nb,), past_len, jnp.int32))
    qs, _, _, lat_s, lat_s_t = _mla_project(xs.reshape(nb, d), cos_s, sin_s, consts, 1)
    wuk2 = w_uk.reshape(KV_LORA, N_HEADS * HEAD_DIM).astype(BF16)
    ql = _whole_call(_mla_absorb_body, jax.ShapeDtypeStruct((nb, N_HEADS * KV_LORA), F32), "mla_absorb", qs, wuk2)
    o_lat = _mla_sample(ql.reshape(nb, N_HEADS, KV_LORA), qs.reshape(nb, N_HEADS, HEAD_PAD),
                        lat_s.reshape(nb, 1, -1), _feature_major_pool(cache), page_table)
    attn_s = _whole_call(_mla_unabsorb_body, jax.ShapeDtypeStruct((nb, d), F32), "mla_unabsorb",
                         o_lat.reshape(nb, N_HEADS * KV_LORA), wv)
    return (attn_p.reshape(b * t, d), attn_s, jnp.transpose(lat_p_t, (0, 2, 1)),
            jnp.transpose(lat_s_t, (2, 0, 1)))


CMP_HIDDEN = 2 * HEAD_DIM
KV_PAIR = 2 * HEAD_DIM


def _nsa_pe_body(pe_ref, w1_ref, o_ref):
    for c in range(2):
        o_ref[:, c * CMP_HIDDEN:(c + 1) * CMP_HIDDEN] = _dot(pe_ref[c], w1_ref[c])


def _compress(load_rows, n_h, wblk_ref, peh_ref, w2k_ref, w2v_ref):
    hid = jnp.zeros((n_h, 4 * CMP_HIDDEN), F32)
    for s in range(0, CMP_STRIDE, 2):
        rows = jnp.concatenate([load_rows(s), load_rows(s + 1)], axis=1).astype(BF16)
        hid = hid + _dot(rows, wblk_ref[s // 2])
    up = lambda x: pltpu.roll(x, n_h - 1, 0)
    peh = peh_ref[0:1, :]
    hk = peh[:, 0:CMP_HIDDEN] + hid[:, 0:CMP_HIDDEN] + up(hid[:, CMP_HIDDEN:2 * CMP_HIDDEN])
    hv = peh[:, CMP_HIDDEN:] + hid[:, 2 * CMP_HIDDEN:3 * CMP_HIDDEN] + up(hid[:, 3 * CMP_HIDDEN:])
    return _dot(jax.nn.gelu(hk).astype(BF16), w2k_ref[...]) + _dot(jax.nn.gelu(hv).astype(BF16), w2v_ref[...])


def _nsa_compress_body(cmp_ref, wblk_ref, peh_ref, w2k_ref, w2v_ref, o_ref):
    n_h = o_ref.shape[1]
    o_ref[0] = _compress(lambda s: cmp_ref[0, pl.ds(s, n_h, stride=CMP_STRIDE), :], n_h,
                         wblk_ref, peh_ref, w2k_ref, w2v_ref)


def _nsa_compress_prompt(cmp, consts):
    b, t, _ = cmp.shape
    n_h = t // CMP_STRIDE
    return pl.pallas_call(
        _nsa_compress_body,
        grid=(b,),
        in_specs=[pl.BlockSpec((1, t, KV_PAIR), lambda i: (i, 0, 0))] + [_const_spec(c.shape) for c in consts],
        out_specs=pl.BlockSpec((1, n_h, KV_PAIR), lambda i: (i, 0, 0)),
        out_shape=jax.ShapeDtypeStruct((b, n_h, KV_PAIR), F32),
        compiler_params=_cparams("parallel"),
        name="nsa_compress",
    )(cmp, *consts)


def _intersect_matrix(n_c, n_s):
    n = lax.broadcasted_iota(jnp.int32, (n_c, n_s), 0) * CMP_STRIDE
    j = lax.broadcasted_iota(jnp.int32, (n_c, n_s), 1) * SLC_BLOCK
    return ((n < j + SLC_BLOCK) & (n + CMP_BLOCK > j)).astype(BF16)


def _split_dot(x, w):
    hi = x.astype(BF16)
    lo = (x - hi.astype(F32)).astype(BF16)
    return _dot(hi, w) + _dot(lo, w)


def _nsa_cmp_bias_body(base_ref, o_ref):
    n_c = o_ref.shape[1]
    for c in range(CMP_STRIDE):
        o_ref[0, :, c * n_c:(c + 1) * n_c] = _toeplitz(base_ref[0, c:c + 1, :], n_c, 1)[:, 0:n_c]


def _nsa_cmp_bias(rel_bias, t):
    n_c = t // CMP_STRIDE
    u = np.arange(2 * n_c)[None, :]
    c = np.arange(CMP_STRIDE)[:, None]
    dist = np.where(u >= n_c, CMP_STRIDE * (2 * n_c - 1 - u) + c - (CMP_BLOCK - 1), -1)
    base = _bias_by_distance(rel_bias, dist)
    out = pl.pallas_call(
        _nsa_cmp_bias_body,
        grid=(N_HEADS,),
        in_specs=[pl.BlockSpec((1, CMP_STRIDE, 2 * n_c), lambda h: (h, 0, 0))],
        out_specs=pl.BlockSpec((1, n_c, CMP_STRIDE * n_c), lambda h: (h, 0, 0)),
        out_shape=jax.ShapeDtypeStruct((N_HEADS, n_c, CMP_STRIDE * n_c), F32),
        compiler_params=_cparams("parallel"),
        name="nsa_cmp_bias",
    )(base)
    return out.reshape(N_HEADS, t, n_c)


NSA_PCHUNK = 1024
NSA_WKEYS = NSA_WINDOW + Q_TILE


def _value_ones(kv):
    lane = lax.broadcasted_iota(jnp.int32, kv.shape, 1)
    rolled = pltpu.roll(kv, HEAD_DIM, 1)
    return jnp.where(lane < HEAD_DIM, rolled, jnp.where(lane == HEAD_DIM, 1.0, 0.0)).astype(BF16)


def _value_ones_t(v_t):
    first = lax.broadcasted_iota(jnp.int32, v_t.shape, 0) == 0
    return jnp.concatenate([v_t, jnp.where(first, 1.0, 0.0)], axis=0).astype(BF16)


def _nsa_bias_tiles_body(rev_ref, o_ref, base_ref, *, nq):
    delta = pl.program_id(0)
    tq = Q_TILE
    width = o_ref.shape[3]
    for t in range(width // tq + 1):
        base_ref[:, t * tq:(t + 1) * tq] = rev_ref[nq - 1 - delta + t]
    for h in range(N_HEADS):
        o_ref[0, h] = _toeplitz(base_ref[h:h + 1, :], tq, width + 1)[:, 0:width]


def _nsa_bias_tiles(rev, nq, n_delta, width):
    return pl.pallas_call(
        functools.partial(_nsa_bias_tiles_body, nq=nq),
        grid=(n_delta,),
        in_specs=[_const_spec(rev.shape)],
        out_specs=pl.BlockSpec((1, N_HEADS, Q_TILE, width), lambda i: (i, 0, 0, 0)),
        out_shape=jax.ShapeDtypeStruct((n_delta, N_HEADS, Q_TILE, width), F32),
        scratch_shapes=[pltpu.VMEM((N_HEADS, width + Q_TILE), F32)],
        compiler_params=_cparams("parallel"),
        name="nsa_bias_tiles",
    )(rev)


def _nsa_prompt_body(q_ref, gate_ref, slc_ref, win_ref, kvc_ref, biasc_ref, tile_ref, wtile_ref, o_ref,
                     qst_ref, oc_ref, ms_ref, accs_ref, sel_ref, *, n_s):
    qi = pl.program_id(1)
    c = pl.program_id(2)
    tq, ch = Q_TILE, NSA_PCHUNK
    n_c = kvc_ref.shape[1]
    q0 = qi * tq
    rows_all = N_HEADS * tq

    @pl.when(c == 0)
    def _():
        for h in range(N_HEADS):
            qst_ref[h * tq:(h + 1) * tq, :] = (
                q_ref[0, :, h * HEAD_DIM:(h + 1) * HEAD_DIM] * (HEAD_DIM ** -0.5)).astype(BF16)

        kvc = kvc_ref[0]
        kcb = kvc[:, 0:HEAD_DIM].astype(BF16)
        vc = _value_ones(kvc)
        qpos_c = q0 + lax.broadcasted_iota(jnp.int32, (tq, n_c), 0)
        cend = lax.broadcasted_iota(jnp.int32, (tq, n_c), 1) * CMP_STRIDE + CMP_BLOCK
        mask_c = jnp.where(cend <= qpos_c + 1, 0.0, NEG_INF)
        s = (_dot_nt(qst_ref[...], kcb).reshape(N_HEADS, tq, n_c) + biasc_ref[...] + mask_c[None]).reshape(rows_all, n_c)
        m = jnp.maximum(jnp.max(s, axis=1, keepdims=True), MASKED_ROW_FLOOR)
        e = jnp.exp(s - m)
        p = e / jnp.maximum(jnp.sum(e, axis=1, keepdims=True), 1e-30)
        oc_ref[...] = _dot(p.astype(BF16), vc)[:, 0:HEAD_DIM]
        imp = _split_dot(jnp.sum(p.reshape(N_HEADS, tq, n_c), axis=0), _intersect_matrix(n_c, n_s))

        qblk = (q0 + lax.broadcasted_iota(jnp.int32, (tq, n_s), 0)) >> 6
        jb = lax.broadcasted_iota(jnp.int32, (tq, n_s), 1)
        forced = (jb == 0) | (jb == qblk) | (jb == qblk - 1)
        score = jnp.where(jb <= qblk, imp + jnp.where(forced, FORCE_BONUS, 0.0), NEG_INF)
        rank = jnp.zeros((tq, n_s), F32)
        for j in range(n_s):
            col = score[:, j:j + 1]
            rank = rank + ((col > score) | ((col == score) & (jb > j))).astype(F32)
        sel_ref[...] = (rank < N_SELECT).astype(BF16)
        ms_ref[...] = jnp.full(ms_ref.shape, NEG_INF, F32)
        accs_ref[...] = jnp.zeros(accs_ref.shape, F32)

    @pl.when(c * ch < q0 + tq)
    def _():
        k0 = c * ch
        i_idx = lax.broadcasted_iota(jnp.int32, (tq, ch), 0)
        j_idx = lax.broadcasted_iota(jnp.int32, (tq, ch), 1)
        e_row = lax.broadcasted_iota(jnp.int32, (n_s, ch), 0)
        e_lane = lax.broadcasted_iota(jnp.int32, (n_s, ch), 1)
        expand = (((k0 + e_lane) >> 6) == e_row).astype(BF16)
        valid = (_dot(sel_ref[...], expand) > 0.5) & (q0 + i_idx >= k0 + j_idx)
        mask = jnp.where(valid, 0.0, NEG_INF)

        @pl.when(jnp.max(mask) > -1.0)
        def _():
            kv = slc_ref[0, :, pl.ds(pl.multiple_of(k0, ch), ch)]
            kb = kv[0:HEAD_DIM, :].astype(BF16)
            va = _value_ones_t(kv[HEAD_DIM:KV_PAIR, :])
            s = (_dot(qst_ref[...], kb).reshape(N_HEADS, tq, ch) + tile_ref[0] + mask[None]).reshape(N_HEADS * tq, ch)
            m = ms_ref[...]
            mn = jnp.maximum(m, jnp.max(s, axis=1, keepdims=True))
            p = jnp.exp(s - jnp.maximum(mn, MASKED_ROW_FLOOR))
            ms_ref[...] = mn
            accs_ref[...] = jnp.exp(m - mn) * accs_ref[...] + _dot_nt(p.astype(BF16), va)

    @pl.when(c == pl.num_programs(2) - 1)
    def _():
        wk = NSA_WKEYS
        k0w = jnp.maximum(q0 - NSA_WINDOW, 0)
        dist = ((q0 - k0w) + lax.broadcasted_iota(jnp.int32, (tq, wk), 0)
                - lax.broadcasted_iota(jnp.int32, (tq, wk), 1))
        mask_w = jnp.where((dist >= 0) & (dist <= NSA_WINDOW), 0.0, NEG_INF)
        kvw = win_ref[0, :, pl.ds(pl.multiple_of(k0w, tq), wk)]
        kwb = kvw[0:HEAD_DIM, :].astype(BF16)
        vwa = _value_ones_t(kvw[HEAD_DIM:KV_PAIR, :])
        s = (_dot(qst_ref[...], kwb).reshape(N_HEADS, tq, wk) + wtile_ref[0] + mask_w[None]).reshape(rows_all, wk)
        m = jnp.maximum(jnp.max(s, axis=1, keepdims=True), MASKED_ROW_FLOOR)
        acc_w = _dot_nt(jnp.exp(s - m).astype(BF16), vwa)
        o_w = acc_w[:, 0:HEAD_DIM] / jnp.maximum(acc_w[:, HEAD_DIM:HEAD_DIM + 1], 1e-30)
        acc_s = accs_ref[...]
        o_s = acc_s[:, 0:HEAD_DIM] / jnp.maximum(acc_s[:, HEAD_DIM:HEAD_DIM + 1], 1e-30)
        gates = jax.nn.sigmoid(gate_ref[0])
        for h in range(N_HEADS):
            rows = slice(h * tq, (h + 1) * tq)
            o_ref[0, :, h * HEAD_DIM:(h + 1) * HEAD_DIM] = (
                gates[:, h:h + 1] * oc_ref[rows, :] + gates[:, N_HEADS + h:N_HEADS + h + 1] * o_s[rows, :]
                + gates[:, 2 * N_HEADS + h:2 * N_HEADS + h + 1] * o_w[rows, :])


def _nsa_prompt(q, gate, kv_t, win_t, kvc, bias_c, tiles, wtiles):
    b, t, d = q.shape
    tq, ch = Q_TILE, NSA_PCHUNK
    n_c = kvc.shape[1]
    n_s = t // SLC_BLOCK
    rows = N_HEADS * tq
    assert t % ch == 0 and t >= NSA_WKEYS
    step = ch // tq

    def tile_index(i, j, c):
        return (j - step * jnp.minimum(c, (j * tq + tq - 1) // ch), 0, 0, 0)

    return pl.pallas_call(
        functools.partial(_nsa_prompt_body, n_s=n_s),
        grid=(b, t // tq, t // ch),
        in_specs=[pl.BlockSpec((1, tq, d), lambda i, j, c: (i, j, 0)),
                  pl.BlockSpec((1, tq, gate.shape[2]), lambda i, j, c: (i, j, 0)),
                  pl.BlockSpec((1, KV_PAIR, t), lambda i, j, c: (i, 1, 0)),
                  pl.BlockSpec((1, KV_PAIR, t), lambda i, j, c: (i, 0, 0)),
                  pl.BlockSpec((1, n_c, KV_PAIR), lambda i, j, c: (i, 0, 0)),
                  pl.BlockSpec((N_HEADS, tq, n_c), lambda i, j, c: (0, j, 0)),
                  pl.BlockSpec((1, N_HEADS, tq, ch), tile_index),
                  pl.BlockSpec((1, N_HEADS, tq, NSA_WKEYS), lambda i, j, c: (jnp.minimum(j, NSA_WINDOW // tq), 0, 0, 0))],
        out_specs=pl.BlockSpec((1, tq, d), lambda i, j, c: (i, j, 0)),
        out_shape=jax.ShapeDtypeStruct((b, t, d), F32),
        scratch_shapes=[pltpu.VMEM((rows, HEAD_DIM), BF16), pltpu.VMEM((rows, HEAD_DIM), F32),
                        pltpu.VMEM((rows, 1), F32), pltpu.VMEM((rows, KV_PAIR), F32),
                        pltpu.VMEM((tq, n_s), BF16)],
        compiler_params=_cparams("parallel", "arbitrary", "arbitrary"),
        name="nsa_prompt",
    )(q, gate, kv_t, win_t, kvc, bias_c, tiles, wtiles)


def _nsa_sample_body(pt_ref, q_ref, gate_ref, new_ref, wnew_ref, cwin_ref, bc_ref, bs_ref, bw_ref, b0_ref,
                     wblk_ref, peh_ref, w2k_ref, w2v_ref, pool_ref, o_ref, wout_ref, buf_ref, sem_ref, cmp_ref,
                     ks_ref, vs_ref, expand_ref, *, n_pages, n_sp):
    slot = _paged_prefetch(pool_ref, buf_ref, sem_ref, pt_ref, n_pages)
    b = pl.program_id(0)
    past = n_pages * PAGE
    n_h = past // CMP_STRIDE
    q = (q_ref[0] * (HEAD_DIM ** -0.5)).astype(BF16)
    qf = q.astype(F32)

    def to_rows(p, carry):
        cmp_ref[pl.ds(pl.multiple_of(p * PAGE, PAGE), PAGE), :] = buf_ref[slot, p, 0:KV_PAIR, :].T
        return carry
    lax.fori_loop(0, n_pages, to_rows, 0)
    kvc = _compress(lambda s: cmp_ref[pl.ds(s, n_h, stride=CMP_STRIDE), :], n_h, wblk_ref, peh_ref, w2k_ref, w2v_ref)
    kc = kvc[:, 0:HEAD_DIM].astype(BF16)
    vc = kvc[:, HEAD_DIM:KV_PAIR].astype(BF16)
    cend = lax.broadcasted_iota(jnp.int32, (N_HEADS, n_h), 1) * CMP_STRIDE + CMP_BLOCK
    valid_c = cend <= past + 1
    s = jnp.where(valid_c, _dot_nt(q, kc) + bc_ref[...], NEG_INF)
    m = jnp.max(s, axis=1, keepdims=True)
    e = jnp.where(valid_c, jnp.exp(s - m), 0.0)
    p_c = e / jnp.maximum(jnp.sum(e, axis=1, keepdims=True), 1e-30)
    o_c = _dot(p_c.astype(BF16), vc)

    psum = jnp.broadcast_to(jnp.sum(p_c, axis=0, keepdims=True), (8, n_h))
    imp = _split_dot(psum, _intersect_matrix(n_h, n_sp))[0:1, :]
    qblk = past // SLC_BLOCK
    jb = lax.broadcasted_iota(jnp.int32, (1, n_sp), 1)
    forced = (jb == 0) | (jb == qblk) | (jb == qblk - 1)
    score = jnp.where(jb <= qblk, imp + jnp.where(forced, FORCE_BONUS, 0.0), NEG_INF)
    r_idx = lax.broadcasted_iota(jnp.int32, (n_sp, n_sp), 0)
    c_idx = lax.broadcasted_iota(jnp.int32, (n_sp, n_sp), 1)
    score_b = jnp.broadcast_to(score, (n_sp, n_sp))
    score_col = jnp.sum(jnp.where(r_idx == c_idx, score_b, 0.0), axis=1, keepdims=True)
    beats = (score_col > score_b) | ((score_col == score_b) & (r_idx < c_idx))
    rank = jnp.sum(beats.astype(F32), axis=0, keepdims=True)
    sel = jnp.broadcast_to((rank < N_SELECT).astype(BF16), (8, n_sp))
    sel_new = rank[:, qblk:qblk + 1] < N_SELECT

    def attend(s_old, v_old_t, kv_new, new_ok):
        k_new = kv_new[:, 0:HEAD_DIM].astype(BF16).astype(F32)
        v_new = kv_new[:, HEAD_DIM:KV_PAIR].astype(BF16).astype(F32)
        s_new = jnp.where(new_ok, jnp.sum(qf * k_new, axis=1, keepdims=True) + b0_ref[:, 0:1], NEG_INF)
        m = jnp.maximum(jnp.maximum(jnp.max(s_old, axis=1, keepdims=True), s_new), MASKED_ROW_FLOOR)
        p_old = jnp.exp(s_old - m)
        p_new = jnp.exp(s_new - m)
        den = jnp.sum(p_old, axis=1, keepdims=True) + p_new
        acc = _dot_nt(p_old.astype(BF16), v_old_t) + p_new.astype(BF16).astype(F32) * v_new
        return acc / jnp.maximum(den, 1e-30)

    @pl.when(b == 0)
    def _():
        blk = lax.broadcasted_iota(jnp.int32, expand_ref.shape, 1) >> 6
        expand_ref[...] = (blk == lax.broadcasted_iota(jnp.int32, expand_ref.shape, 0)).astype(BF16)

    for p in range(n_pages):
        lanes = slice(p * PAGE, (p + 1) * PAGE)
        ks_ref[:, lanes] = buf_ref[slot, p, KV_PAIR:KV_PAIR + HEAD_DIM, :].astype(BF16)
        vs_ref[:, lanes] = buf_ref[slot, p, KV_PAIR + HEAD_DIM:2 * KV_PAIR, :].astype(BF16)
    picked = _dot(sel, expand_ref[...])[0:1, :] > 0.5
    s = _dot(q, ks_ref[...]) + bs_ref[...] + jnp.where(picked, 0.0, NEG_INF)
    o_s = attend(s, vs_ref[...], new_ref[0, 0:1, :], jnp.broadcast_to(sel_new, (N_HEADS, 1)))

    cwin = cwin_ref[0]
    s = _dot(q, cwin[0:HEAD_DIM, :].astype(BF16)) + bw_ref[...]
    o_w = attend(s, cwin[HEAD_DIM:KV_PAIR, :].astype(BF16), new_ref[0, 1:2, :], jnp.full((N_HEADS, 1), True))

    gates = jax.nn.sigmoid(gate_ref[0])
    o_ref[0] = gates[:, 0:1] * o_c + gates[:, 1:2] * o_s + gates[:, 2:3] * o_w

    wb = cwin.shape[1]
    pick = lax.broadcasted_iota(jnp.int32, wnew_ref.shape, 1) == b
    new_col = jnp.sum(jnp.where(pick, wnew_ref[...], 0.0), axis=1, keepdims=True)
    last = lax.broadcasted_iota(jnp.int32, cwin.shape, 1) == wb - 1
    wout_ref[0] = jnp.where(last, new_col, pltpu.roll(cwin, wb - 1, 1))


def _nsa_sample(q, gate_t, new, wnew_t, cwin_t, biases, consts, pool, page_table):
    nb = q.shape[0]
    n_pages = page_table.shape[1]
    past = n_pages * PAGE
    assert cwin_t.shape[2] <= NSA_WINDOW
    n_s = past // SLC_BLOCK + 1
    n_sp = -(-n_s // 128) * 128
    per_b = lambda a: pl.BlockSpec((1,) + a.shape[1:], lambda i, pt: (i,) + (0,) * (a.ndim - 1))
    const = lambda a: pl.BlockSpec(a.shape, lambda i, pt: (0,) * a.ndim)
    grid_spec = pltpu.PrefetchScalarGridSpec(
        num_scalar_prefetch=1,
        grid=(nb,),
        in_specs=[per_b(q), per_b(gate_t), per_b(new), const(wnew_t), per_b(cwin_t)]
        + [const(a) for a in biases] + [const(a) for a in consts]
        + [pl.BlockSpec(memory_space=pl.ANY)],
        out_specs=[pl.BlockSpec((1, N_HEADS, HEAD_DIM), lambda i, pt: (i, 0, 0)), per_b(cwin_t)],
        scratch_shapes=[pltpu.VMEM((2, n_pages, pool.shape[1], PAGE), F32), pltpu.SemaphoreType.DMA((2,)),
                        pltpu.VMEM((past, KV_PAIR), F32), pltpu.VMEM((HEAD_DIM, past), BF16),
                        pltpu.VMEM((HEAD_DIM, past), BF16), pltpu.VMEM((n_sp, past), BF16)],
    )
    return pl.pallas_call(
        functools.partial(_nsa_sample_body, n_pages=n_pages, n_sp=n_sp),
        grid_spec=grid_spec,
        out_shape=[jax.ShapeDtypeStruct((nb, N_HEADS, HEAD_DIM), F32), jax.ShapeDtypeStruct(cwin_t.shape, F32)],
        compiler_params=_cparams("arbitrary"),
        name="nsa_sample",
    )(page_table, q, gate_t, new, wnew_t, cwin_t, *biases, *consts, pool)


def _nsa_layer(xp, xs, cache_kv, cache_win, page_table, w_in, pe, w1, w2, rel_bias):
    b, t, d = xp.shape
    nb = xs.shape[0]
    past = page_table.shape[1] * PAGE
    w_q, w_gate = w_in[:, :d].astype(BF16), w_in[:, d + 3 * KV_PAIR:].astype(BF16)
    w_cmp, w_slc, w_win = [w_in[:, d + i * KV_PAIR:d + (i + 1) * KV_PAIR].astype(BF16) for i in range(3)]
    wts = [w_in[:, d:d + 2 * KV_PAIR].T.astype(BF16), w_in[:, d + 2 * KV_PAIR:d + 3 * KV_PAIR].T.astype(BF16)]

    w1r = w1.reshape(2, 2, CMP_STRIDE, HEAD_DIM, CMP_HIDDEN)
    zero = jnp.zeros((CMP_STRIDE, HEAD_DIM, CMP_HIDDEN), F32)
    top = jnp.concatenate([w1r[0, 0], w1r[0, 1], zero, zero], axis=2)
    bot = jnp.concatenate([zero, zero, w1r[1, 0], w1r[1, 1]], axis=2)
    wblk = jnp.concatenate([top, bot], axis=1).astype(BF16).reshape(CMP_STRIDE // 2, 2 * KV_PAIR, -1)
    pe8 = jnp.broadcast_to(pe.reshape(2, 1, -1), (2, 8, CMP_BLOCK * HEAD_DIM)).astype(BF16)
    peh = _whole_call(_nsa_pe_body, jax.ShapeDtypeStruct((8, 2 * CMP_HIDDEN), F32), "nsa_pe", pe8, w1.astype(BF16))
    zpad = jnp.zeros((CMP_HIDDEN, HEAD_DIM), F32)
    w2k = jnp.concatenate([w2[0], zpad], axis=1).astype(BF16)
    w2v = jnp.concatenate([zpad, w2[1]], axis=1).astype(BF16)
    consts = [wblk, peh, w2k, w2v]

    qp, cmp_p, gate_p, kv_t, win_t = _proj(xp.reshape(b * t, d), [w_q, w_cmp, w_gate], wts, b)
    r3 = lambda a: a.reshape(b, t, -1)
    kvc = _nsa_compress_prompt(r3(cmp_p), consts)
    nq = t // Q_TILE
    bias_c = _nsa_cmp_bias(rel_bias, t)
    n_rev = nq + max(NSA_WKEYS, NSA_PCHUNK) // Q_TILE
    rev = _bias_by_distance(rel_bias, Q_TILE * nq - 1 - np.arange(n_rev * Q_TILE))
    rev = jnp.transpose(rev.reshape(N_HEADS, n_rev, Q_TILE), (1, 0, 2))
    tiles = _nsa_bias_tiles(rev, nq, nq, NSA_PCHUNK)
    wtiles = _nsa_bias_tiles(rev, nq, NSA_WINDOW // Q_TILE + 1, NSA_WKEYS)
    attn_p = _nsa_prompt(r3(qp), r3(gate_p), kv_t, win_t, kvc, bias_c, tiles, wtiles)
    kv_p = _token_major(kv_t, (4, 1, HEAD_DIM))
    win_out_p = _token_major(win_t[:, :, t - min(NSA_WINDOW, t):], (2, 1, HEAD_DIM))

    qs, slc_s, win_s, gate_s, kvs_t, wins_t = _proj(xs.reshape(nb, d), [w_q, w_slc, w_win, w_gate], wts, 1)
    wb = cache_win.shape[1]
    n_h = past // CMP_STRIDE
    bc = _bias_by_distance(rel_bias, past - (np.arange(n_h) * CMP_STRIDE + CMP_BLOCK - 1))
    bs = _bias_by_distance(rel_bias, past - np.arange(past))
    bw = _bias_by_distance(rel_bias, wb - np.arange(wb))
    b0 = _bias_by_distance(rel_bias, np.zeros((128,), np.int64))
    gate_t = jnp.transpose(gate_s.reshape(nb, 3, N_HEADS), (0, 2, 1))
    new = jnp.stack([slc_s, win_s], axis=1)
    cwin_t = jnp.transpose(cache_win, (0, 2, 3, 4, 1)).reshape(nb, KV_PAIR, wb)
    attn_s, wout_t = _nsa_sample(qs.reshape(nb, N_HEADS, HEAD_DIM), gate_t, new, wins_t[0], cwin_t,
                                 [bc, bs, bw, b0], consts, _feature_major_pool(cache_kv), page_table)
    kv_s = jnp.transpose(kvs_t[0], (1, 0)).reshape(nb, 1, 4, 1, HEAD_DIM)
    win_out_s = _token_major(wout_t, (2, 1, HEAD_DIM))
    return attn_p.reshape(b * t, d), attn_s.reshape(nb, d), kv_p, kv_s, win_out_p, win_out_s


def kernel(x_prompt, x_sample, cache_nsa_kv, cache_nsa_win, cache_mla, state_dil_w128, state_dil_w512,
           state_dil_w2048, cache_sb_kv, page_table, p_prompt, p_sample, rel_bias, ln1_g, ln1_b, ln2_g, ln2_b,
           ffn_wg, ffn_wu, ffn_wd, ple_wg, ple_wp, nsa_w_in, nsa_cmp_pe, nsa_cmp_w1, nsa_cmp_w2, nsa_w_out,
           mla_w_dq, mla_q_norm, mla_kv_norm, mla_w_uq, mla_w_uk, mla_w_uv, mla_w_out, dil_w_in, dil_w_out,
           sb_w_in, sb_w_out):
    b, t, d = x_prompt.shape
    nb = x_sample.shape[0]
    past_len = page_table.shape[1] * PAGE
    depth = p_prompt.shape[0]
    n_mixers = 4
    dil_states = (state_dil_w128, state_dil_w512, state_dil_w2048)
    xp = x_prompt.reshape(b * t, d)
    xs = x_sample.reshape(nb, d)
    pp_all = p_prompt.reshape(depth, b * t, -1)
    ps_all = p_sample.reshape(depth, nb, -1)
    outs = {k: [] for k in ("nsa_kv_p", "nsa_kv_s", "nsa_win_p", "nsa_win_s", "mla_p", "mla_s", "sb_p", "sb_s")}
    dil_p = [[] for _ in DIL_PATTERNS]
    dil_s = [[] for _ in DIL_PATTERNS]
    for i in range(depth):
        kind, j = i % n_mixers, i // n_mixers
        xp3, xs3 = xp.reshape(b, t, d), xs.reshape(nb, 1, d)
        if kind == 0:
            mp, ms, a, b_, c, e = _nsa_layer(xp3, xs3, cache_nsa_kv[j], cache_nsa_win[j], page_table, nsa_w_in[j],
                                            nsa_cmp_pe[j], nsa_cmp_w1[j], nsa_cmp_w2[j], rel_bias)
            mp, ms, w_out = [mp], [ms], nsa_w_out[j]
            outs["nsa_kv_p"].append(a)
            outs["nsa_kv_s"].append(b_)
            outs["nsa_win_p"].append(c)
            outs["nsa_win_s"].append(e)
        elif kind == 1:
            mp, ms, a, b_ = _mla_layer(xp3, xs3, cache_mla[j], page_table, mla_w_dq[j], mla_q_norm[j], mla_kv_norm[j],
                                       mla_w_uq[j], mla_w_uk[j], mla_w_uv[j], past_len)
            mp, ms, w_out = [mp], [ms], mla_w_out[j]
            outs["mla_p"].append(a)
            outs["mla_s"].append(b_)
        elif kind == 2:
            mp, ms, st_p, st_s = _dil_layer(xp3, xs3, [s[j] for s in dil_states], dil_w_in[j], rel_bias)
            ms, w_out = [ms], dil_w_out[j]
            for g in range(len(DIL_PATTERNS)):
                dil_p[g].append(st_p[g])
                dil_s[g].append(st_s[g])
        else:
            mp, ms, a, b_ = _sb_layer(xp3, xs3, cache_sb_kv[j], page_table, sb_w_in[j])
            mp, ms, w_out = [mp], [ms], sb_w_out[j]
            outs["sb_p"].append(a)
            outs["sb_s"].append(b_)
        row = lambda v: v.reshape(1, -1)
        consts = (w_out.astype(BF16), row(ln1_g[i]), row(ln1_b[i]), row(ln2_g[i]), row(ln2_b[i]),
                  ffn_wg[i].astype(BF16), ffn_wu[i].astype(BF16), ffn_wd[i].astype(BF16),
                  ple_wg[i].astype(BF16), ple_wp[i].astype(BF16))
        xp = _tail(mp, xp, pp_all, i, *consts)
        xs = _tail(ms, xs, ps_all, i, *consts)
    st = jnp.stack
    return (xp.reshape(b, t, d), xs.reshape(nb, 1, d),
            st(outs["nsa_kv_p"]), st(outs["nsa_kv_s"]), st(outs["nsa_win_p"]), st(outs["nsa_win_s"]),
            st(outs["mla_p"]), st(outs["mla_s"]),
            st(dil_p[0]), st(dil_s[0]), st(dil_p[1]), st(dil_s[1]), st(dil_p[2]), st(dil_s[2]),
            st(outs["sb_p"]), st(outs["sb_s"]))
```

```python
import functools
import math

import numpy as np
import jax
import jax.numpy as jnp
from jax import lax
from jax.experimental import pallas as pl
from jax.experimental.pallas import tpu as pltpu

F32 = jnp.float32
BF16 = jnp.bfloat16

HEAD_DIM = 64
N_HEADS = 16
PAGE = 128
Q_TILE = 128
LN_EPS = 1e-5
NEG_INF = -1e30
DEPTH = 4
ALPHA = (2 * DEPTH) ** 0.25
N_BUCKETS = 32
MAX_DISTANCE = 2048
CMP_BLOCK = 32
CMP_STRIDE = 16
SLC_BLOCK = 64
N_SELECT = 16
NSA_WINDOW = 512
FORCE_BONUS = 1e4
DIL_PATTERNS = ((128, 1), (512, 4), (2048, 16))
ROPE_THETA = 10000.0
QK_ROPE = 32
KV_LORA = 256
Q_LORA = 256
SB_DEAD = -104.0
MASKED_ROW_FLOOR = -1e29
VMEM_LIMIT_BYTES = 56 * 1024 * 1024


def _cparams(*sem):
    return pltpu.CompilerParams(dimension_semantics=sem, vmem_limit_bytes=VMEM_LIMIT_BYTES)


def _dot(a, b):
    return jnp.dot(a, b, preferred_element_type=F32)


def _dot_nt(a, b):
    return lax.dot_general(a, b, (((1,), (1,)), ((), ())), preferred_element_type=F32)


def _const_spec(shape):
    nd = len(shape)
    return pl.BlockSpec(shape, lambda *_: (0,) * nd)


def _bucket_of_distance(n_dist):
    n = np.arange(n_dist, dtype=np.int64)
    max_exact = N_BUCKETS // 2
    ratio = np.maximum(n, max_exact).astype(np.float32) / np.float32(max_exact)
    log_ratio = np.log(ratio).astype(np.float32) / np.float32(math.log(MAX_DISTANCE / max_exact))
    large = np.minimum(max_exact + (log_ratio * np.float32(N_BUCKETS - max_exact)).astype(np.int32), N_BUCKETS - 1)
    return np.where(n < max_exact, n, large).astype(np.int32)


def _bias_by_distance(rel_bias, dists):
    d = np.asarray(dists)
    bucket = _bucket_of_distance(int(d.max()) + 1)[np.maximum(d, 0)]
    vals = jnp.moveaxis(rel_bias.astype(F32)[bucket], -1, 0)
    return jnp.where(jnp.asarray(d >= 0), vals, 0.0)


def _proj_body(x_ref, *refs, n_row, n_col):
    xb = x_ref[...].astype(BF16)
    n = n_row + n_col
    for w_ref, o_ref in zip(refs[:n_row], refs[n:n + n_row]):
        o_ref[...] = _dot(xb, w_ref[...])
    for w_ref, o_ref in zip(refs[n_row:n], refs[n + n_row:]):
        o_ref[0] = _dot_nt(w_ref[...], xb)


def _proj(x, ws, wts, n_batch):
    m, k = x.shape
    tm = _row_tile(m)
    t = m // n_batch
    per_b = t // tm
    return pl.pallas_call(
        functools.partial(_proj_body, n_row=len(ws), n_col=len(wts)),
        grid=(m // tm,),
        in_specs=[pl.BlockSpec((tm, k), lambda i: (i, 0))] + [_const_spec(w.shape) for w in ws + wts],
        out_specs=[pl.BlockSpec((tm, w.shape[1]), lambda i: (i, 0)) for w in ws]
        + [pl.BlockSpec((1, w.shape[0], tm), lambda i: (i // per_b, 0, i % per_b)) for w in wts],
        out_shape=[jax.ShapeDtypeStruct((m, w.shape[1]), F32) for w in ws]
        + [jax.ShapeDtypeStruct((n_batch, w.shape[0], t), F32) for w in wts],
        compiler_params=_cparams("parallel"),
        name="proj",
    )(x, *ws, *wts)


def _row_tile(m):
    return 256 if m % 256 == 0 else m


def _layer_norm(x, g, b):
    mu = jnp.mean(x, axis=-1, keepdims=True)
    xc = x - mu
    var = jnp.mean(xc * xc, axis=-1, keepdims=True)
    return xc * lax.rsqrt(var + LN_EPS) * g + b


FF_CHUNK = 256


def _tail_body(*refs, n_mix):
    mix_refs = refs[:n_mix]
    (x_ref, p_ref, wo_ref, g1_ref, b1_ref, g2_ref, b2_ref, wg_ref, wu_ref, wd_ref,
     pwg_ref, pwp_ref, o_ref) = refs[n_mix:]
    if n_mix == 1:
        attn = mix_refs[0][...]
    else:
        n_g = n_mix // 2
        lses = [r[...] for r in mix_refs[n_g:]]
        m = functools.reduce(jnp.maximum, lses)
        es = [jnp.exp(l - m) for l in lses]
        den = functools.reduce(lambda a, b: a + b, es)
        ws = [e / den for e in es]
        cols = []
        for h in range(N_HEADS):
            sl = slice(h * HEAD_DIM, (h + 1) * HEAD_DIM)
            acc = None
            for g in range(n_g):
                term = ws[g][:, h:h + 1] * mix_refs[g][:, sl]
                acc = term if acc is None else acc + term
            cols.append(acc)
        attn = jnp.concatenate(cols, axis=1)
    x = x_ref[...]
    mix = _dot(attn.astype(BF16), wo_ref[...])
    h1 = _layer_norm(ALPHA * x + mix, g1_ref[...], b1_ref[...])
    h1b = h1.astype(BF16)
    d_ff = wg_ref.shape[1]
    acc = jnp.zeros(x.shape, F32)
    for c in range(d_ff // FF_CHUNK):
        sl = slice(c * FF_CHUNK, (c + 1) * FF_CHUNK)
        g = _dot(h1b, wg_ref[:, sl])
        u = _dot(h1b, wu_ref[:, sl])
        acc = acc + _dot((g * jax.nn.sigmoid(g) * u).astype(BF16), wd_ref[sl, :])
    h2 = _layer_norm(ALPHA * h1 + acc, g2_ref[...], b2_ref[...])
    gate = jax.nn.sigmoid(_dot(h2.astype(BF16), pwg_ref[...]))
    o_ref[...] = h2 + gate * _dot(p_ref[...].astype(BF16), pwp_ref[...])


def _tail(mix_list, x, p_all, layer, wo, g1, b1, g2, b2, wg, wu, wd, pwg, pwp):
    m, d = x.shape
    tm = _row_tile(m)
    consts = [wo, g1, b1, g2, b2, wg, wu, wd, pwg, pwp]
    row = lambda a: pl.BlockSpec((tm, a.shape[1]), lambda i: (i, 0))
    single = lambda a: pl.BlockSpec(a.shape, lambda i: (0, 0), pipeline_mode=pl.Buffered(1))
    return pl.pallas_call(
        functools.partial(_tail_body, n_mix=len(mix_list)),
        grid=(m // tm,),
        in_specs=[row(a) for a in mix_list]
        + [row(x), pl.BlockSpec((None, tm, p_all.shape[2]), lambda i: (layer, i, 0))] + [single(a) for a in consts],
        out_specs=pl.BlockSpec((tm, d), lambda i: (i, 0)),
        out_shape=jax.ShapeDtypeStruct((m, d), F32),
        compiler_params=_cparams("parallel"),
        name="tail",
    )(*mix_list, x, p_all, *consts)


def _page_copies(pool_ref, buf_ref, sem_ref, pt_ref, b, slot, n_pages, start):
    def body(p, carry):
        if len(buf_ref.shape) == 4:
            dst = buf_ref.at[slot, p]
        else:
            rows = pool_ref.shape[1]
            dst = buf_ref.at[slot, pl.ds(pl.multiple_of(p * rows, rows), rows)]
        cp = pltpu.make_async_copy(pool_ref.at[pt_ref[b, p]], dst, sem_ref.at[slot])
        if start:
            cp.start()
        else:
            cp.wait()
        return carry
    lax.fori_loop(0, n_pages, body, 0)


def _paged_prefetch(pool_ref, buf_ref, sem_ref, pt_ref, n_pages):
    b = pl.program_id(0)
    nb = pl.num_programs(0)
    slot = b % 2

    @pl.when(b == 0)
    def _():
        _page_copies(pool_ref, buf_ref, sem_ref, pt_ref, 0, 0, n_pages, True)

    @pl.when(b + 1 < nb)
    def _():
        _page_copies(pool_ref, buf_ref, sem_ref, pt_ref, b + 1, 1 - slot, n_pages, True)

    _page_copies(pool_ref, buf_ref, sem_ref, pt_ref, b, slot, n_pages, False)
    return slot


def _sb_terms(z, valid):
    t = jnp.log(1.0 + jnp.exp(-jnp.abs(z)))
    l1m = -jnp.maximum(z, 0.0) - t
    if valid is not None:
        l1m = jnp.where(valid, l1m, 0.0)
    ls = jnp.minimum(z, 0.0) - t
    return l1m, ls


def _strict_upper_sum_matrix(n):
    j = lax.broadcasted_iota(jnp.int32, (2 * n, n), 0) & (n - 1)
    s = lax.broadcasted_iota(jnp.int32, (2 * n, n), 1)
    return (j > s).astype(BF16)


def _tail_sums(l1m, uu):
    hi = l1m.astype(BF16)
    lo = (l1m - hi.astype(F32)).astype(BF16)
    return _dot(jnp.concatenate([hi, lo], axis=1), uu)


SB_GROUPS_PER_LOOP = 4


def _sb_prompt_body(q_ref, kv_ref, o_ref, r_ref, acc_ref, *, n_kv, grp):
    qi = pl.program_id(1)
    tq = q_ref.shape[1]
    rows = grp * tq
    uu = _strict_upper_sum_matrix(Q_TILE)
    qpos = qi * tq + (lax.broadcasted_iota(jnp.int32, (rows, Q_TILE), 0) & (tq - 1))
    lane = lax.broadcasted_iota(jnp.int32, (rows, Q_TILE), 1)
    for g0 in range(0, n_kv, SB_GROUPS_PER_LOOP):
        groups = range(g0, g0 + SB_GROUPS_PER_LOOP)
        qgs = []
        for g in groups:
            qg = jnp.concatenate(
                [q_ref[0, :, (g * grp + u) * HEAD_DIM:(g * grp + u + 1) * HEAD_DIM] for u in range(grp)], axis=0)
            qgs.append((qg * (HEAD_DIM ** -0.5)).astype(BF16))
        r_ref[...] = jnp.zeros(r_ref.shape, F32)
        acc_ref[...] = jnp.zeros(acc_ref.shape, F32)

        def cond(c):
            return jnp.logical_and(c[0] >= 0, c[1] > SB_DEAD)

        def body(c):
            kj = c[0]
            off = pl.multiple_of(kj * Q_TILE, Q_TILE)
            valid = (kj * Q_TILE + lane) < qpos
            alive = jnp.float32(-jnp.inf)
            for i, g in enumerate(groups):
                part = slice(i * rows, (i + 1) * rows)
                k = kv_ref[0, g * HEAD_DIM:(g + 1) * HEAD_DIM, pl.ds(off, Q_TILE)].astype(BF16)
                v = kv_ref[0, (n_kv + g) * HEAD_DIM:(n_kv + g + 1) * HEAD_DIM, pl.ds(off, Q_TILE)].astype(BF16)
                l1m, ls = _sb_terms(_dot(qgs[i], k), valid)
                r = r_ref[part, :]
                a = jnp.where(valid, jnp.exp(ls + _tail_sums(l1m, uu) + r), 0.0)
                acc_ref[part, :] += _dot_nt(a.astype(BF16), v)
                rn = r + jnp.sum(l1m, axis=1, keepdims=True)
                r_ref[part, :] = rn
                alive = jnp.maximum(alive, jnp.max(rn))
            return kj - 1, alive

        lax.while_loop(cond, body, (qi, jnp.float32(0.0)))
        for i, g in enumerate(groups):
            for u in range(grp):
                h = g * grp + u
                o_ref[0, :, h * HEAD_DIM:(h + 1) * HEAD_DIM] = acc_ref[i * rows + u * tq:i * rows + (u + 1) * tq, :]


def _sb_prompt(q, kv, n_kv):
    b, t, d = q.shape
    grp = N_HEADS // n_kv
    assert n_kv % SB_GROUPS_PER_LOOP == 0
    rows = SB_GROUPS_PER_LOOP * grp * Q_TILE
    return pl.pallas_call(
        functools.partial(_sb_prompt_body, n_kv=n_kv, grp=grp),
        grid=(b, t // Q_TILE),
        in_specs=[pl.BlockSpec((1, Q_TILE, d), lambda i, j: (i, j, 0)),
                  pl.BlockSpec((1, kv.shape[1], t), lambda i, j: (i, 0, 0))],
        out_specs=pl.BlockSpec((1, Q_TILE, d), lambda i, j: (i, j, 0)),
        out_shape=jax.ShapeDtypeStruct((b, t, d), F32),
        scratch_shapes=[pltpu.VMEM((rows, 1), F32), pltpu.VMEM((rows, HEAD_DIM), F32)],
        compiler_params=_cparams("parallel", "arbitrary"),
        name="sb_prompt",
    )(q, kv)


def _head_spread(n_kv):
    d = lax.broadcasted_iota(jnp.int32, (HEAD_DIM, n_kv * HEAD_DIM), 0)
    c = lax.broadcasted_iota(jnp.int32, (HEAD_DIM, n_kv * HEAD_DIM), 1)
    return ((c & (HEAD_DIM - 1)) == d).astype(BF16)


def _block_diag_q(q, n_kv, scale):
    grp = N_HEADS // n_kv
    width = n_kv * HEAD_DIM
    spread = _dot((q * scale).astype(BF16), _head_spread(n_kv))
    row = lax.broadcasted_iota(jnp.int32, (N_HEADS, width), 0)
    col = lax.broadcasted_iota(jnp.int32, (N_HEADS, width), 1)
    own = (col >> 6) == (row >> int(math.log2(grp)))
    return jnp.where(own, spread, 0.0).astype(BF16)


def _block_diag_pick(o_full, n_kv):
    grp = N_HEADS // n_kv
    row = lax.broadcasted_iota(jnp.int32, (N_HEADS, HEAD_DIM), 0)
    out = jnp.zeros((N_HEADS, HEAD_DIM), F32)
    for g in range(n_kv):
        out = out + jnp.where((row >> int(math.log2(grp))) == g, o_full[:, g * HEAD_DIM:(g + 1) * HEAD_DIM], 0.0)
    return out


SB_EAGER_PAGES = 4


def _sb_sample_body(pt_ref, q_ref, pool_ref, o_ref, buf_ref, sem_ref, old_ref, old_sem, *, n_kv, n_pages):
    b = pl.program_id(0)
    nb = pl.num_programs(0)
    slot = b % 2
    n_eager = buf_ref.shape[1]
    first = n_pages - n_eager

    def eager(bb, sl, start):
        for i in range(n_eager):
            cp = pltpu.make_async_copy(pool_ref.at[pt_ref[bb, first + i]], buf_ref.at[sl, i], sem_ref.at[sl])
            if start:
                cp.start()
            else:
                cp.wait()

    @pl.when(b == 0)
    def _():
        eager(0, 0, True)

    @pl.when(b + 1 < nb)
    def _():
        eager(b + 1, 1 - slot, True)

    eager(b, slot, False)
    width = n_kv * HEAD_DIM
    qbd = _block_diag_q(q_ref[0], n_kv, HEAD_DIM ** -0.5)
    uu = _strict_upper_sum_matrix(PAGE)

    def step(page, r, acc):
        l1m, ls = _sb_terms(_dot(qbd, page[0:width, :].astype(BF16)), None)
        a = jnp.exp(ls + _tail_sums(l1m, uu) + r)
        acc = acc + _dot_nt(a.astype(BF16), page[width:2 * width, :].astype(BF16))
        rn = r + jnp.sum(l1m, axis=1, keepdims=True)
        return jnp.max(rn), rn, acc

    def cond(c):
        return jnp.logical_and(c[0] >= 0, c[1] > SB_DEAD)

    def newest(c):
        i, _, r, acc = c
        return (i - 1,) + step(buf_ref[slot, i], r, acc)

    def older(c):
        p, _, r, acc = c
        cp = pltpu.make_async_copy(pool_ref.at[pt_ref[b, p]], old_ref, old_sem.at[0])
        cp.start()
        cp.wait()
        return (p - 1,) + step(old_ref[...], r, acc)

    init = (jnp.int32(n_eager - 1), jnp.float32(0.0), jnp.zeros((N_HEADS, 1), F32), jnp.zeros((N_HEADS, width), F32))
    _, alive, r, acc = lax.while_loop(cond, newest, init)
    acc = lax.while_loop(cond, older, (jnp.int32(first - 1), alive, r, acc))[3]
    o_ref[0] = _block_diag_pick(acc, n_kv)


def _sb_sample(q, pool, page_table, n_kv):
    nb = q.shape[0]
    n_pages = page_table.shape[1]
    feat = pool.shape[1]
    n_eager = min(SB_EAGER_PAGES, n_pages)
    grid_spec = pltpu.PrefetchScalarGridSpec(
        num_scalar_prefetch=1,
        grid=(nb,),
        in_specs=[pl.BlockSpec((1, N_HEADS, HEAD_DIM), lambda i, pt: (i, 0, 0)),
                  pl.BlockSpec(memory_space=pl.ANY)],
        out_specs=pl.BlockSpec((1, N_HEADS, HEAD_DIM), lambda i, pt: (i, 0, 0)),
        scratch_shapes=[pltpu.VMEM((2, n_eager, feat, PAGE), F32), pltpu.SemaphoreType.DMA((2,)),
                        pltpu.VMEM((feat, PAGE), F32), pltpu.SemaphoreType.DMA((1,))],
    )
    return pl.pallas_call(
        functools.partial(_sb_sample_body, n_kv=n_kv, n_pages=n_pages),
        grid_spec=grid_spec,
        out_shape=jax.ShapeDtypeStruct((nb, N_HEADS, HEAD_DIM), F32),
        compiler_params=_cparams("arbitrary"),
        name="sb_sample",
    )(page_table, q, pool)


def _feature_major_pool(cache):
    n = cache.ndim
    return jnp.transpose(cache, (0,) + tuple(range(2, n)) + (1,)).reshape(cache.shape[0], -1, cache.shape[1])


def _token_major(x_t, feat_shape):
    b, _, t = x_t.shape
    nf = len(feat_shape)
    return jnp.transpose(x_t.reshape((b,) + tuple(feat_shape) + (t,)), (0, nf + 1) + tuple(range(1, nf + 1)))


def _sb_layer(xp, xs, cache, page_table, w_in):
    b, t, d = xp.shape
    n_kv = (w_in.shape[1] - d) // (2 * HEAD_DIM)
    ws, wts = [w_in[:, :d].astype(BF16)], [w_in[:, d:].T.astype(BF16)]
    qp, kvp_t = _proj(xp.reshape(b * t, d), ws, wts, b)
    attn_p = _sb_prompt(qp.reshape(b, t, d), kvp_t, n_kv)
    nb = xs.shape[0]
    qs, kvs_t = _proj(xs.reshape(nb, d), ws, wts, 1)
    attn_s = _sb_sample(qs.reshape(nb, N_HEADS, HEAD_DIM), _feature_major_pool(cache), page_table, n_kv)
    kv_s = _token_major(kvs_t, (2, n_kv, HEAD_DIM)).reshape(nb, 1, 2, n_kv, HEAD_DIM)
    return attn_p.reshape(b * t, d), attn_s.reshape(nb, d), _token_major(kvp_t, (2, n_kv, HEAD_DIM)), kv_s


def _toeplitz(base_row, rows, shift):
    return pltpu.roll(jnp.broadcast_to(base_row, (rows, base_row.shape[1])), shift, 1, stride=1, stride_axis=0)


def _dil_prompt_body(q_ref, kvc_ref, kvp_ref, base_ref, o_ref, lse_ref, *, n_kv, grp):
    mi = pl.program_id(2)
    tq = Q_TILE
    i_idx = lax.broadcasted_iota(jnp.int32, (tq, 2 * tq), 0)
    j_idx = lax.broadcasted_iota(jnp.int32, (tq, 2 * tq), 1)
    steps = i_idx - j_idx + tq
    mask = jnp.where((steps >= 0) & (steps <= tq) & ((mi > 0) | (j_idx >= tq)), 0.0, NEG_INF)
    lane = lax.broadcasted_iota(jnp.int32, (tq, 128), 1)
    lse_tile = jnp.zeros((tq, 128), F32)
    width = n_kv * HEAD_DIM
    for g in range(n_kv):
        ksl = slice(g * HEAD_DIM, (g + 1) * HEAD_DIM)
        vsl = slice(width + g * HEAD_DIM, width + (g + 1) * HEAD_DIM)
        k = jnp.concatenate([kvp_ref[0, :, ksl], kvc_ref[0, :, ksl]], axis=0).astype(BF16)
        v = jnp.concatenate([kvp_ref[0, :, vsl], kvc_ref[0, :, vsl]], axis=0).astype(BF16)
        heads = range(g * grp, (g + 1) * grp)
        qg = jnp.concatenate([q_ref[0, :, h * HEAD_DIM:(h + 1) * HEAD_DIM] for h in heads], axis=0)
        bias = jnp.stack([_toeplitz(base_ref[h:h + 1, :], tq, 0) for h in heads])
        s = _dot_nt((qg * (HEAD_DIM ** -0.5)).astype(BF16), k)
        s = (s.reshape(grp, tq, 2 * tq) + bias + mask[None]).reshape(grp * tq, 2 * tq)
        m = jnp.maximum(jnp.max(s, axis=1, keepdims=True), MASKED_ROW_FLOOR)
        e = jnp.exp(s - m)
        den = jnp.maximum(jnp.sum(e, axis=1, keepdims=True), 1e-30)
        o = _dot((e / den).astype(BF16), v)
        lse = m + jnp.log(den)
        for u, h in enumerate(heads):
            o_ref[0, :, h * HEAD_DIM:(h + 1) * HEAD_DIM] = o[u * tq:(u + 1) * tq, :]
            lse_tile = jnp.where(lane == h, lse[u * tq:(u + 1) * tq, :], lse_tile)
    lse_ref[0] = lse_tile


def _dil_prompt(q, kv, base, dil, n_kv):
    b, tm, _ = q.shape
    d = N_HEADS * HEAD_DIM
    kvw = 2 * n_kv * HEAD_DIM
    nm = tm // Q_TILE
    return pl.pallas_call(
        functools.partial(_dil_prompt_body, n_kv=n_kv, grp=N_HEADS // n_kv),
        grid=(b, dil, nm),
        in_specs=[pl.BlockSpec((1, Q_TILE, d), lambda i, r, m: (i, m, r)),
                  pl.BlockSpec((1, Q_TILE, kvw), lambda i, r, m: (i, m, r)),
                  pl.BlockSpec((1, Q_TILE, kvw), lambda i, r, m: (i, jnp.maximum(m - 1, 0), r)),
                  _const_spec(base.shape)],
        out_specs=[pl.BlockSpec((1, Q_TILE, d), lambda i, r, m: (i, m, r)),
                   pl.BlockSpec((1, Q_TILE, 128), lambda i, r, m: (i, m, r))],
        out_shape=[jax.ShapeDtypeStruct((b, tm, dil * d), F32), jax.ShapeDtypeStruct((b, tm, dil * 128), F32)],
        compiler_params=_cparams("parallel", "parallel", "arbitrary"),
        name="dil_prompt",
    )(q, kv, kv, base)


def _dil_sample_body(q_ref, kvn_ref, kvnt_ref, b0_ref, bm0_ref, bm1_ref, bm2_ref, st0_ref, st1_ref, st2_ref,
                     o_ref, so0_ref, so1_ref, so2_ref, *, n_kv):
    b = pl.program_id(0)
    width = n_kv * HEAD_DIM
    nb = kvnt_ref.shape[2]
    pick = lax.broadcasted_iota(jnp.int32, (2 * width, nb), 1) == b
    outs, lses = [], []
    groups = zip((st0_ref, st1_ref, st2_ref), (so0_ref, so1_ref, so2_ref), (bm0_ref, bm1_ref, bm2_ref))
    for g, (st, so, bm) in enumerate(groups):
        w = st.shape[2]
        kn = kvn_ref[0, g:g + 1, 0:width].astype(BF16).astype(F32)
        vn = kvn_ref[0, g:g + 1, width:2 * width].astype(BF16).astype(F32)
        qbd = _block_diag_q(q_ref[0, g], n_kv, HEAD_DIM ** -0.5)
        s_old = _dot(qbd, st[0, 0:width, :].astype(BF16)) + bm[...]
        s_new = jnp.sum(qbd.astype(F32) * kn, axis=1, keepdims=True) + b0_ref[:, 0:1]
        m = jnp.maximum(jnp.max(s_old, axis=1, keepdims=True), s_new)
        e_old = jnp.exp(s_old - m)
        e_new = jnp.exp(s_new - m)
        den = jnp.sum(e_old, axis=1, keepdims=True) + e_new
        o_full = (_dot_nt((e_old / den).astype(BF16), st[0, width:2 * width, :].astype(BF16))
                  + (e_new / den).astype(BF16).astype(F32) * vn)
        outs.append(_block_diag_pick(o_full, n_kv))
        lses.append(m + jnp.log(den))
        new_col = jnp.sum(jnp.where(pick, kvnt_ref[g], 0.0), axis=1, keepdims=True)
        last = lax.broadcasted_iota(jnp.int32, (2 * width, w), 1) == w - 1
        so[0] = jnp.where(last, new_col, pltpu.roll(st[0], w - 1, 1))
    m = functools.reduce(jnp.maximum, lses)
    es = [jnp.exp(l - m) for l in lses]
    den = functools.reduce(lambda a, b: a + b, es)
    o = None
    for e, og in zip(es, outs):
        o = (e / den) * og if o is None else o + (e / den) * og
    o_ref[0] = o


def _dil_sample(q, kvn, kvn_t, b0, bias_masks, states, n_kv):
    nb = q.shape[0]
    st_specs = [pl.BlockSpec((1,) + s.shape[1:], lambda i: (i, 0, 0)) for s in states]
    outs = pl.pallas_call(
        functools.partial(_dil_sample_body, n_kv=n_kv),
        grid=(nb,),
        in_specs=[pl.BlockSpec((1,) + q.shape[1:], lambda i: (i, 0, 0, 0)),
                  pl.BlockSpec((1,) + kvn.shape[1:], lambda i: (i, 0, 0)),
                  _const_spec(kvn_t.shape), _const_spec(b0.shape)]
        + [_const_spec(bm.shape) for bm in bias_masks] + st_specs,
        out_specs=[pl.BlockSpec((1, N_HEADS, HEAD_DIM), lambda i: (i, 0, 0))] + st_specs,
        out_shape=[jax.ShapeDtypeStruct((nb, N_HEADS, HEAD_DIM), F32)]
        + [jax.ShapeDtypeStruct(s.shape, F32) for s in states],
        compiler_params=_cparams("parallel"),
        name="dil_sample",
    )(q, kvn, kvn_t, b0, *bias_masks, *states)
    return outs[0], outs[1:]


def _dil_layer(xp, xs, states, w_in, rel_bias):
    b, t, d = xp.shape
    n_g = len(DIL_PATTERNS)
    w3 = w_in.reshape(d, n_g, -1)
    kvw = w3.shape[2] - d
    n_kv = kvw // (2 * HEAD_DIM)
    ws, wts = [], []
    for g in range(n_g):
        ws += [w3[:, g, :d].astype(BF16), w3[:, g, d:].astype(BF16)]
        wts.append(w3[:, g, d:].T.astype(BF16))
    outs_p = _proj(xp.reshape(b * t, d), ws, wts, b)
    lane = np.arange(2 * Q_TILE)
    mix, st_p = [], []
    for g, (w, dil) in enumerate(DIL_PATTERNS):
        assert w // dil == Q_TILE and t % (dil * Q_TILE) == 0
        base = _bias_by_distance(rel_bias, np.where(lane <= Q_TILE, (Q_TILE - lane) * dil, -1))
        o_g, lse_g = _dil_prompt(outs_p[2 * g].reshape(b, t // dil, dil * d),
                                 outs_p[2 * g + 1].reshape(b, t // dil, dil * kvw), base, dil, n_kv)
        mix.append((o_g.reshape(b * t, d), lse_g.reshape(b * t, 128)))
        st_p.append(_token_major(outs_p[2 * n_g + g][:, :, t - min(w, t):], (2, n_kv, HEAD_DIM)))
    mix_p = [m[0] for m in mix] + [m[1] for m in mix]
    nb = xs.shape[0]
    outs_s = _proj(xs.reshape(nb, d), ws, wts, 1)
    q_s = jnp.stack([outs_s[2 * g].reshape(nb, N_HEADS, HEAD_DIM) for g in range(n_g)], axis=1)
    kvn = jnp.stack([outs_s[2 * g + 1] for g in range(n_g)], axis=1)
    kvn_t = jnp.concatenate(outs_s[2 * n_g:], axis=0)
    bias_masks = []
    for g, (w, dil) in enumerate(DIL_PATTERNS):
        pos = np.arange(states[g].shape[1])
        bias = _bias_by_distance(rel_bias, np.where(pos % dil == 0, w - pos, -1))
        bias_masks.append(jnp.where(jnp.asarray(pos % dil == 0), bias, NEG_INF))
    b0 = _bias_by_distance(rel_bias, np.zeros((128,), np.int64))
    st_t = [jnp.transpose(s, (0, 2, 3, 4, 1)).reshape(nb, kvw, s.shape[1]) for s in states]
    attn_s, st_s = _dil_sample(q_s, kvn, kvn_t, b0, bias_masks, st_t, n_kv)
    st_s = [_token_major(s, (2, n_kv, HEAD_DIM)) for s in st_s]
    return mix_p, attn_s.reshape(nb, d), st_p, st_s


HEAD_PAD = 128
ROPE_HALF = QK_ROPE // 2
MLA_SCALE = (HEAD_DIM + QK_ROPE) ** -0.5


def _rms(x, g):
    return x * lax.rsqrt(jnp.mean(x * x, axis=-1, keepdims=True) + LN_EPS) * g


def _mla_project_body(x_ref, c_ref, s_ref, wdq_ref, qn_ref, kvn_ref, wq_ref, wqs_ref, wk_ref, wv_ref, vone_ref,
                      q_ref, k_ref, v_ref, lat_ref, latt_ref):
    h = _dot(x_ref[...].astype(BF16), wdq_ref[...])
    cq = _rms(h[:, 0:Q_LORA], qn_ref[...]).astype(BF16)
    ckv = _rms(h[:, Q_LORA:Q_LORA + KV_LORA], kvn_ref[...])
    ckvb = ckv.astype(BF16)
    cos, sin = c_ref[...], s_ref[...]
    base = Q_LORA + KV_LORA
    kr = h[:, base:base + HEAD_PAD] * cos + h[:, base + HEAD_PAD:base + 2 * HEAD_PAD] * sin
    cos_all = jnp.concatenate([cos] * N_HEADS, axis=1)
    sin_all = jnp.concatenate([sin] * N_HEADS, axis=1)
    q_ref[...] = ((_dot(cq, wq_ref[...]) * cos_all + _dot(cq, wqs_ref[...]) * sin_all) * MLA_SCALE).astype(BF16)
    k_ref[...] = (_dot(ckvb, wk_ref[...]) + jnp.concatenate([kr] * N_HEADS, axis=1)).astype(BF16)
    v_ref[...] = (_dot(ckvb, wv_ref[...]) + vone_ref[...]).astype(BF16)
    lat_ref[:, 0:KV_LORA] = ckv
    lat_ref[:, KV_LORA:KV_LORA + QK_ROPE] = kr[:, HEAD_DIM:HEAD_DIM + QK_ROPE]
    latt_ref[0, 0:KV_LORA, :] = ckv.T
    latt_ref[0, KV_LORA:KV_LORA + QK_ROPE, :] = kr.T[HEAD_DIM:HEAD_DIM + QK_ROPE, :]


def _mla_project(x, cos, sin, consts, n_batch):
    m, d = x.shape
    tm = _row_tile(m)
    t = m // n_batch
    per_b = t // tm
    pos_blocks = cos.shape[0] // tm
    lat_w = KV_LORA + QK_ROPE
    wide = N_HEADS * HEAD_PAD
    rope_spec = pl.BlockSpec((tm, HEAD_PAD), lambda i: (i % pos_blocks, 0))
    return pl.pallas_call(
        _mla_project_body,
        grid=(m // tm,),
        in_specs=[pl.BlockSpec((tm, d), lambda i: (i, 0)), rope_spec, rope_spec] + [_const_spec(c.shape) for c in consts],
        out_specs=[pl.BlockSpec((tm, wide), lambda i: (i, 0)), pl.BlockSpec((tm, wide), lambda i: (i, 0)),
                   pl.BlockSpec((tm, wide), lambda i: (i, 0)), pl.BlockSpec((tm, lat_w), lambda i: (i, 0)),
                   pl.BlockSpec((1, lat_w, tm), lambda i: (i // per_b, 0, i % per_b))],
        out_shape=[jax.ShapeDtypeStruct((m, wide), BF16), jax.ShapeDtypeStruct((m, wide), BF16),
                   jax.ShapeDtypeStruct((m, wide), BF16), jax.ShapeDtypeStruct((m, lat_w), F32),
                   jax.ShapeDtypeStruct((n_batch, lat_w, t), F32)],
        compiler_params=_cparams("parallel"),
        name="mla_project",
    )(x, cos, sin, *consts)


MLA_TQ = 256
MLA_CHUNK = 1024
MLA_HEADS_PER_STEP = 4


def _mla_prompt_body(q_ref, k_ref, v_ref, o_ref, m_ref, acc_ref):
    qi = pl.program_id(2)
    tq, ch = MLA_TQ, MLA_CHUNK
    q0 = qi * tq
    m_ref[...] = jnp.full(m_ref.shape, NEG_INF, F32)
    acc_ref[...] = jnp.zeros(acc_ref.shape, F32)
    i_idx = lax.broadcasted_iota(jnp.int32, (tq, ch), 0)
    j_idx = lax.broadcasted_iota(jnp.int32, (tq, ch), 1)

    def chunk(c, carry):
        k0 = pl.multiple_of(c * ch, ch)
        mask = jnp.where(q0 + i_idx >= k0 + j_idx, 0.0, NEG_INF)
        heads = range(MLA_HEADS_PER_STEP)
        rows = [slice(hh * tq, (hh + 1) * tq) for hh in heads]
        lanes = [slice(hh * HEAD_PAD, (hh + 1) * HEAD_PAD) for hh in heads]
        m_old = [m_ref[r, :] for r in rows]
        acc_old = [acc_ref[r, :] for r in rows]
        m_new, acc_new = [], []
        for hh in heads:
            s = _dot_nt(q_ref[0, :, lanes[hh]], k_ref[0, pl.ds(k0, ch), lanes[hh]]) + mask
            mn = jnp.maximum(m_old[hh], jnp.max(s, axis=1, keepdims=True))
            p = jnp.exp(s - jnp.maximum(mn, MASKED_ROW_FLOOR))
            m_new.append(mn)
            acc_new.append(jnp.exp(m_old[hh] - mn) * acc_old[hh] + _dot(p.astype(BF16), v_ref[0, pl.ds(k0, ch), lanes[hh]]))
        for hh in heads:
            m_ref[rows[hh], :] = m_new[hh]
            acc_ref[rows[hh], :] = acc_new[hh]
        return carry

    lax.fori_loop(0, (q0 + tq + ch - 1) // ch, chunk, 0)
    for hh in range(MLA_HEADS_PER_STEP):
        acc = acc_ref[hh * tq:(hh + 1) * tq, :]
        o_ref[0, :, hh * HEAD_DIM:(hh + 1) * HEAD_DIM] = acc[:, 0:HEAD_DIM] / jnp.maximum(acc[:, HEAD_DIM:HEAD_DIM + 1], 1e-30)


def _mla_prompt(q, k, v):
    b, t, _ = q.shape
    tq = min(MLA_TQ, t)
    assert tq == MLA_TQ and t % MLA_CHUNK == 0
    nh = MLA_HEADS_PER_STEP
    pair = nh * HEAD_PAD
    return pl.pallas_call(
        _mla_prompt_body,
        grid=(b, N_HEADS // nh, t // tq),
        in_specs=[pl.BlockSpec((1, tq, pair), lambda i, h, j: (i, j, h)),
                  pl.BlockSpec((1, t, pair), lambda i, h, j: (i, 0, h)),
                  pl.BlockSpec((1, t, pair), lambda i, h, j: (i, 0, h))],
        out_specs=pl.BlockSpec((1, tq, nh * HEAD_DIM), lambda i, h, j: (i, j, h)),
        out_shape=jax.ShapeDtypeStruct((b, t, N_HEADS * HEAD_DIM), F32),
        scratch_shapes=[pltpu.VMEM((nh * tq, 1), F32), pltpu.VMEM((nh * tq, HEAD_PAD), F32)],
        compiler_params=_cparams("parallel", "parallel", "arbitrary"),
        name="mla_prompt",
    )(q, k, v)


def _mla_absorb_body(q_ref, wuk_ref, o_ref):
    for h in range(N_HEADS):
        qn = q_ref[:, h * HEAD_PAD:h * HEAD_PAD + HEAD_DIM]
        o_ref[:, h * KV_LORA:(h + 1) * KV_LORA] = _dot_nt(qn, wuk_ref[:, h * HEAD_DIM:(h + 1) * HEAD_DIM])


def _mla_unabsorb_body(o_ref, wuv_ref, y_ref):
    for h in range(N_HEADS):
        y_ref[:, h * HEAD_DIM:(h + 1) * HEAD_DIM] = _dot(
            o_ref[:, h * KV_LORA:(h + 1) * KV_LORA].astype(BF16), wuv_ref[:, h * HEAD_DIM:(h + 1) * HEAD_DIM])


def _whole_call(body, out_shape, name, *args):
    return pl.pallas_call(
        body, grid=(1,),
        in_specs=[_const_spec(a.shape) for a in args],
        out_specs=_const_spec(out_shape.shape),
        out_shape=out_shape, compiler_params=_cparams("arbitrary"), name=name)(*args)


def _mla_sample_body(pt_ref, ql_ref, q_ref, new_ref, pool_ref, o_ref, buf_ref, sem_ref, ckv_ref, kr_ref, *, n_pages):
    slot = _paged_prefetch(pool_ref, buf_ref, sem_ref, pt_ref, n_pages)
    ql = ql_ref[0].astype(BF16)
    qr = q_ref[0, :, HEAD_DIM:HEAD_DIM + QK_ROPE]
    new_c = new_ref[0, :, 0:KV_LORA].astype(BF16).astype(F32)
    new_r = new_ref[0, :, KV_LORA:KV_LORA + QK_ROPE].astype(BF16).astype(F32)
    s_new = (jnp.sum(ql.astype(F32) * new_c, axis=1, keepdims=True)
             + jnp.sum(qr.astype(F32) * new_r, axis=1, keepdims=True))
    for p in range(n_pages):
        ckv_ref[:, p * PAGE:(p + 1) * PAGE] = buf_ref[slot, p, 0:KV_LORA, :].astype(BF16)
        kr_ref[:, p * PAGE:(p + 1) * PAGE] = buf_ref[slot, p, KV_LORA:KV_LORA + QK_ROPE, :].astype(BF16)
    s = _dot(ql, ckv_ref[...]) + _dot(qr, kr_ref[...])
    m = jnp.maximum(jnp.max(s, axis=1, keepdims=True), s_new)
    p_old = jnp.exp(s - m)
    p_new = jnp.exp(s_new - m)
    den = jnp.sum(p_old, axis=1, keepdims=True) + p_new
    acc = _dot_nt(p_old.astype(BF16), ckv_ref[...]) + p_new * new_c
    o_ref[0] = acc / den


def _mla_sample(ql, q, lat_new, pool, page_table):
    nb = ql.shape[0]
    n_pages = page_table.shape[1]
    feat = pool.shape[1]
    grid_spec = pltpu.PrefetchScalarGridSpec(
        num_scalar_prefetch=1,
        grid=(nb,),
        in_specs=[pl.BlockSpec((1, N_HEADS, KV_LORA), lambda i, pt: (i, 0, 0)),
                  pl.BlockSpec((1, N_HEADS, HEAD_PAD), lambda i, pt: (i, 0, 0)),
                  pl.BlockSpec((1, 1, lat_new.shape[2]), lambda i, pt: (i, 0, 0)),
                  pl.BlockSpec(memory_space=pl.ANY)],
        out_specs=pl.BlockSpec((1, N_HEADS, KV_LORA), lambda i, pt: (i, 0, 0)),
        scratch_shapes=[pltpu.VMEM((2, n_pages, feat, PAGE), F32), pltpu.SemaphoreType.DMA((2,)),
                        pltpu.VMEM((KV_LORA, n_pages * PAGE), BF16), pltpu.VMEM((QK_ROPE, n_pages * PAGE), BF16)],
    )
    return pl.pallas_call(
        functools.partial(_mla_sample_body, n_pages=n_pages),
        grid_spec=grid_spec,
        out_shape=jax.ShapeDtypeStruct((nb, N_HEADS, KV_LORA), F32),
        compiler_params=_cparams("arbitrary"),
        name="mla_sample",
    )(page_table, ql, q, lat_new, pool)


def _pad_heads(w, parts):
    out = jnp.zeros((w.shape[0], N_HEADS, HEAD_PAD), w.dtype)
    for src, size, dst in parts:
        out = out.at[:, :, dst:dst + size].set(w[:, :, src:src + size])
    return out.reshape(w.shape[0], N_HEADS * HEAD_PAD)


def _mla_layer(xp, xs, cache, page_table, w_dq, q_norm, kv_norm, w_uq, w_uk, w_uv, past_len):
    b, t, d = xp.shape
    nb = xs.shape[0]
    r0, r1 = HEAD_DIM, HEAD_DIM + ROPE_HALF
    keep = [(0, HEAD_DIM, 0), (HEAD_DIM, ROPE_HALF, r0), (r1, ROPE_HALF, r1)]
    swap = [(r1, ROPE_HALF, r0), (HEAD_DIM, ROPE_HALF, r1)]
    base = Q_LORA + KV_LORA
    x1w, x2w = w_dq[:, base:base + ROPE_HALF], w_dq[:, base + ROPE_HALF:base + QK_ROPE]
    z_lo, z_hi = jnp.zeros((d, HEAD_DIM), F32), jnp.zeros((d, HEAD_PAD - HEAD_DIM - QK_ROPE), F32)
    kr_keep = jnp.concatenate([z_lo, x1w, x2w, z_hi], axis=1)
    kr_swap = jnp.concatenate([z_lo, x2w, x1w, z_hi], axis=1)
    wdq = jnp.concatenate([w_dq[:, :base], kr_keep, kr_swap], axis=1).astype(BF16)
    wq3 = w_uq.reshape(Q_LORA, N_HEADS, HEAD_DIM + QK_ROPE)
    wq = _pad_heads(wq3, keep).astype(BF16)
    wqs = _pad_heads(wq3, swap).astype(BF16)
    wk = _pad_heads(w_uk, [(0, HEAD_DIM, 0)]).astype(BF16)
    wv = w_uv.reshape(KV_LORA, N_HEADS * HEAD_DIM).astype(BF16)
    wv_pad = _pad_heads(w_uv, [(0, HEAD_DIM, 0)]).astype(BF16)
    vone = jnp.asarray((np.arange(N_HEADS * HEAD_PAD) % HEAD_PAD == HEAD_DIM).astype(np.float32)).reshape(1, -1)
    consts = [wdq, q_norm.reshape(1, -1), kv_norm.reshape(1, -1), wq, wqs, wk, wv_pad, vone]

    def rope_tables(pos):
        inv = ROPE_THETA ** (-jnp.arange(ROPE_HALF, dtype=F32) / ROPE_HALF)
        ang = pos.astype(F32)[:, None] * inv[None, :]
        cos, sin = jnp.cos(ang), jnp.sin(ang)
        n = pos.shape[0]
        c = jnp.concatenate([jnp.ones((n, HEAD_DIM), F32), cos, cos, jnp.zeros((n, HEAD_PAD - r1 - ROPE_HALF), F32)], axis=1)
        s = jnp.concatenate([jnp.zeros((n, HEAD_DIM), F32), -sin, sin, jnp.zeros((n, HEAD_PAD - r1 - ROPE_HALF), F32)], axis=1)
        return c, s

    cos_p, sin_p = rope_tables(jnp.arange(t))
    q, k, v, _, lat_p_t = _mla_project(xp.reshape(b * t, d), cos_p, sin_p, consts, b)
    wide = N_HEADS * HEAD_PAD
    attn_p = _mla_prompt(q.reshape(b, t, wide), k.reshape(b, t, wide), v.reshape(b, t, wide))
    cos_s, sin_s = rope_tables(jnp.full((nb,), past_len, jnp.int32))
    qs, _, _, lat_s, lat_s_t = _mla_project(xs.reshape(nb, d), cos_s, sin_s, consts, 1)
    wuk2 = w_uk.reshape(KV_LORA, N_HEADS * HEAD_DIM).astype(BF16)
    ql = _whole_call(_mla_absorb_body, jax.ShapeDtypeStruct((nb, N_HEADS * KV_LORA), F32), "mla_absorb", qs, wuk2)
    o_lat = _mla_sample(ql.reshape(nb, N_HEADS, KV_LORA), qs.reshape(nb, N_HEADS, HEAD_PAD),
                        lat_s.reshape(nb, 1, -1), _feature_major_pool(cache), page_table)
    attn_s = _whole_call(_mla_unabsorb_body, jax.ShapeDtypeStruct((nb, d), F32), "mla_unabsorb",
                         o_lat.reshape(nb, N_HEADS * KV_LORA), wv)
    return (attn_p.reshape(b * t, d), attn_s, jnp.transpose(lat_p_t, (0, 2, 1)),
            jnp.transpose(lat_s_t, (2, 0, 1)))


CMP_HIDDEN = 2 * HEAD_DIM
KV_PAIR = 2 * HEAD_DIM


def _nsa_pe_body(pe_ref, w1_ref, o_ref):
    for c in range(2):
        o_ref[:, c * CMP_HIDDEN:(c + 1) * CMP_HIDDEN] = _dot(pe_ref[c], w1_ref[c])


def _compress(load_rows, n_h, wblk_ref, peh_ref, w2k_ref, w2v_ref):
    hid = jnp.zeros((n_h, 4 * CMP_HIDDEN), F32)
    for s in range(0, CMP_STRIDE, 2):
        rows = jnp.concatenate([load_rows(s), load_rows(s + 1)], axis=1).astype(BF16)
        hid = hid + _dot(rows, wblk_ref[s // 2])
    up = lambda x: pltpu.roll(x, n_h - 1, 0)
    peh = peh_ref[0:1, :]
    hk = peh[:, 0:CMP_HIDDEN] + hid[:, 0:CMP_HIDDEN] + up(hid[:, CMP_HIDDEN:2 * CMP_HIDDEN])
    hv = peh[:, CMP_HIDDEN:] + hid[:, 2 * CMP_HIDDEN:3 * CMP_HIDDEN] + up(hid[:, 3 * CMP_HIDDEN:])
    return _dot(jax.nn.gelu(hk).astype(BF16), w2k_ref[...]) + _dot(jax.nn.gelu(hv).astype(BF16), w2v_ref[...])


def _nsa_compress_body(cmp_ref, wblk_ref, peh_ref, w2k_ref, w2v_ref, o_ref):
    n_h = o_ref.shape[1]
    o_ref[0] = _compress(lambda s: cmp_ref[0, pl.ds(s, n_h, stride=CMP_STRIDE), :], n_h,
                         wblk_ref, peh_ref, w2k_ref, w2v_ref)


def _nsa_compress_prompt(cmp, consts):
    b, t, _ = cmp.shape
    n_h = t // CMP_STRIDE
    return pl.pallas_call(
        _nsa_compress_body,
        grid=(b,),
        in_specs=[pl.BlockSpec((1, t, KV_PAIR), lambda i: (i, 0, 0))] + [_const_spec(c.shape) for c in consts],
        out_specs=pl.BlockSpec((1, n_h, KV_PAIR), lambda i: (i, 0, 0)),
        out_shape=jax.ShapeDtypeStruct((b, n_h, KV_PAIR), F32),
        compiler_params=_cparams("parallel"),
        name="nsa_compress",
    )(cmp, *consts)


def _intersect_matrix(n_c, n_s):
    n = lax.broadcasted_iota(jnp.int32, (n_c, n_s), 0) * CMP_STRIDE
    j = lax.broadcasted_iota(jnp.int32, (n_c, n_s), 1) * SLC_BLOCK
    return ((n < j + SLC_BLOCK) & (n + CMP_BLOCK > j)).astype(BF16)


def _split_dot(x, w):
    hi = x.astype(BF16)
    lo = (x - hi.astype(F32)).astype(BF16)
    return _dot(hi, w) + _dot(lo, w)


def _nsa_cmp_bias_body(base_ref, o_ref):
    n_c = o_ref.shape[1]
    for c in range(CMP_STRIDE):
        o_ref[0, :, c * n_c:(c + 1) * n_c] = _toeplitz(base_ref[0, c:c + 1, :], n_c, 1)[:, 0:n_c]


def _nsa_cmp_bias(rel_bias, t):
    n_c = t // CMP_STRIDE
    u = np.arange(2 * n_c)[None, :]
    c = np.arange(CMP_STRIDE)[:, None]
    dist = np.where(u >= n_c, CMP_STRIDE * (2 * n_c - 1 - u) + c - (CMP_BLOCK - 1), -1)
    base = _bias_by_distance(rel_bias, dist)
    out = pl.pallas_call(
        _nsa_cmp_bias_body,
        grid=(N_HEADS,),
        in_specs=[pl.BlockSpec((1, CMP_STRIDE, 2 * n_c), lambda h: (h, 0, 0))],
        out_specs=pl.BlockSpec((1, n_c, CMP_STRIDE * n_c), lambda h: (h, 0, 0)),
        out_shape=jax.ShapeDtypeStruct((N_HEADS, n_c, CMP_STRIDE * n_c), F32),
        compiler_params=_cparams("parallel"),
        name="nsa_cmp_bias",
    )(base)
    return out.reshape(N_HEADS, t, n_c)


NSA_PCHUNK = 1024
NSA_WKEYS = NSA_WINDOW + Q_TILE


def _value_ones(kv):
    lane = lax.broadcasted_iota(jnp.int32, kv.shape, 1)
    rolled = pltpu.roll(kv, HEAD_DIM, 1)
    return jnp.where(lane < HEAD_DIM, rolled, jnp.where(lane == HEAD_DIM, 1.0, 0.0)).astype(BF16)


def _value_ones_t(v_t):
    first = lax.broadcasted_iota(jnp.int32, v_t.shape, 0) == 0
    return jnp.concatenate([v_t, jnp.where(first, 1.0, 0.0)], axis=0).astype(BF16)


def _nsa_bias_tiles_body(rev_ref, o_ref, base_ref, *, nq):
    delta = pl.program_id(0)
    tq = Q_TILE
    width = o_ref.shape[3]
    for t in range(width // tq + 1):
        base_ref[:, t * tq:(t + 1) * tq] = rev_ref[nq - 1 - delta + t]
    for h in range(N_HEADS):
        o_ref[0, h] = _toeplitz(base_ref[h:h + 1, :], tq, width + 1)[:, 0:width]


def _nsa_bias_tiles(rev, nq, n_delta, width):
    return pl.pallas_call(
        functools.partial(_nsa_bias_tiles_body, nq=nq),
        grid=(n_delta,),
        in_specs=[_const_spec(rev.shape)],
        out_specs=pl.BlockSpec((1, N_HEADS, Q_TILE, width), lambda i: (i, 0, 0, 0)),
        out_shape=jax.ShapeDtypeStruct((n_delta, N_HEADS, Q_TILE, width), F32),
        scratch_shapes=[pltpu.VMEM((N_HEADS, width + Q_TILE), F32)],
        compiler_params=_cparams("parallel"),
        name="nsa_bias_tiles",
    )(rev)


def _nsa_prompt_body(q_ref, gate_ref, slc_ref, win_ref, kvc_ref, biasc_ref, tile_ref, wtile_ref, o_ref,
                     qst_ref, oc_ref, ms_ref, accs_ref, sel_ref, *, n_s):
    qi = pl.program_id(1)
    c = pl.program_id(2)
    tq, ch = Q_TILE, NSA_PCHUNK
    n_c = kvc_ref.shape[1]
    q0 = qi * tq
    rows_all = N_HEADS * tq

    @pl.when(c == 0)
    def _():
        for h in range(N_HEADS):
            qst_ref[h * tq:(h + 1) * tq, :] = (
                q_ref[0, :, h * HEAD_DIM:(h + 1) * HEAD_DIM] * (HEAD_DIM ** -0.5)).astype(BF16)

        kvc = kvc_ref[0]
        kcb = kvc[:, 0:HEAD_DIM].astype(BF16)
        vc = _value_ones(kvc)
        qpos_c = q0 + lax.broadcasted_iota(jnp.int32, (tq, n_c), 0)
        cend = lax.broadcasted_iota(jnp.int32, (tq, n_c), 1) * CMP_STRIDE + CMP_BLOCK
        mask_c = jnp.where(cend <= qpos_c + 1, 0.0, NEG_INF)
        s = (_dot_nt(qst_ref[...], kcb).reshape(N_HEADS, tq, n_c) + biasc_ref[...] + mask_c[None]).reshape(rows_all, n_c)
        m = jnp.maximum(jnp.max(s, axis=1, keepdims=True), MASKED_ROW_FLOOR)
        e = jnp.exp(s - m)
        p = e / jnp.maximum(jnp.sum(e, axis=1, keepdims=True), 1e-30)
        oc_ref[...] = _dot(p.astype(BF16), vc)[:, 0:HEAD_DIM]
        n_sp = sel_ref.shape[1]
        imp = _split_dot(jnp.sum(p.reshape(N_HEADS, tq, n_c), axis=0), _intersect_matrix(n_c, n_sp))

        qblk = (q0 + lax.broadcasted_iota(jnp.int32, (tq, n_sp), 0)) >> 6
        jb = lax.broadcasted_iota(jnp.int32, (tq, n_sp), 1)
        forced = (jb == 0) | (jb == qblk) | (jb == qblk - 1)
        score = jnp.where(jb <= qblk, imp + jnp.where(forced, FORCE_BONUS, 0.0), NEG_INF)
        score_t = score.T[0:n_s, :]
        blk_t = lax.broadcasted_iota(jnp.int32, (n_s, tq), 0)
        rank_t = jnp.zeros((n_s, tq), F32)
        for j in range(n_s):
            row = score_t[j:j + 1, :]
            rank_t = rank_t + ((row > score_t) | ((row == score_t) & (blk_t > j))).astype(F32)
        sel_t = jnp.concatenate([(rank_t < N_SELECT).astype(F32), jnp.zeros((n_sp - n_s, tq), F32)], axis=0)
        sel_ref[...] = sel_t.T.astype(BF16)
        ms_ref[...] = jnp.full(ms_ref.shape, NEG_INF, F32)
        accs_ref[...] = jnp.zeros(accs_ref.shape, F32)

    @pl.when(c * ch < q0 + tq)
    def _():
        k0 = c * ch
        i_idx = lax.broadcasted_iota(jnp.int32, (tq, ch), 0)
        j_idx = lax.broadcasted_iota(jnp.int32, (tq, ch), 1)
        e_row = lax.broadcasted_iota(jnp.int32, (sel_ref.shape[1], ch), 0)
        e_lane = lax.broadcasted_iota(jnp.int32, (sel_ref.shape[1], ch), 1)
        expand = (((k0 + e_lane) >> 6) == e_row).astype(BF16)
        valid = (_dot(sel_ref[...], expand) > 0.5) & (q0 + i_idx >= k0 + j_idx)
        mask = jnp.where(valid, 0.0, NEG_INF)

        @pl.when(jnp.max(mask) > -1.0)
        def _():
            kv = slc_ref[0, :, pl.ds(pl.multiple_of(k0, ch), ch)]
            kb = kv[0:HEAD_DIM, :].astype(BF16)
            va = _value_ones_t(kv[HEAD_DIM:KV_PAIR, :])
            s = (_dot(qst_ref[...], kb).reshape(N_HEADS, tq, ch) + tile_ref[0] + mask[None]).reshape(N_HEADS * tq, ch)
            m = ms_ref[...]
            mn = jnp.maximum(m, jnp.max(s, axis=1, keepdims=True))
            p = jnp.exp(s - jnp.maximum(mn, MASKED_ROW_FLOOR))
            ms_ref[...] = mn
            accs_ref[...] = jnp.exp(m - mn) * accs_ref[...] + _dot_nt(p.astype(BF16), va)

    @pl.when(c == pl.num_programs(2) - 1)
    def _():
        wk = NSA_WKEYS
        k0w = jnp.maximum(q0 - NSA_WINDOW, 0)
        dist = ((q0 - k0w) + lax.broadcasted_iota(jnp.int32, (tq, wk), 0)
                - lax.broadcasted_iota(jnp.int32, (tq, wk), 1))
        mask_w = jnp.where((dist >= 0) & (dist <= NSA_WINDOW), 0.0, NEG_INF)
        kvw = win_ref[0, :, pl.ds(pl.multiple_of(k0w, tq), wk)]
        kwb = kvw[0:HEAD_DIM, :].astype(BF16)
        vwa = _value_ones_t(kvw[HEAD_DIM:KV_PAIR, :])
        s = (_dot(qst_ref[...], kwb).reshape(N_HEADS, tq, wk) + wtile_ref[0] + mask_w[None]).reshape(rows_all, wk)
        m = jnp.maximum(jnp.max(s, axis=1, keepdims=True), MASKED_ROW_FLOOR)
        acc_w = _dot_nt(jnp.exp(s - m).astype(BF16), vwa)
        o_w = acc_w[:, 0:HEAD_DIM] / jnp.maximum(acc_w[:, HEAD_DIM:HEAD_DIM + 1], 1e-30)
        acc_s = accs_ref[...]
        o_s = acc_s[:, 0:HEAD_DIM] / jnp.maximum(acc_s[:, HEAD_DIM:HEAD_DIM + 1], 1e-30)
        gates = jax.nn.sigmoid(gate_ref[0])
        for h in range(N_HEADS):
            rows = slice(h * tq, (h + 1) * tq)
            o_ref[0, :, h * HEAD_DIM:(h + 1) * HEAD_DIM] = (
                gates[:, h:h + 1] * oc_ref[rows, :] + gates[:, N_HEADS + h:N_HEADS + h + 1] * o_s[rows, :]
                + gates[:, 2 * N_HEADS + h:2 * N_HEADS + h + 1] * o_w[rows, :])


def _nsa_prompt(q, gate, kv_t, win_t, kvc, bias_c, tiles, wtiles):
    b, t, d = q.shape
    tq, ch = Q_TILE, NSA_PCHUNK
    n_c = kvc.shape[1]
    n_s = t // SLC_BLOCK
    rows = N_HEADS * tq
    assert t % ch == 0 and t >= NSA_WKEYS
    step = ch // tq

    def tile_index(i, j, c):
        return (j - step * jnp.minimum(c, (j * tq + tq - 1) // ch), 0, 0, 0)

    return pl.pallas_call(
        functools.partial(_nsa_prompt_body, n_s=n_s),
        grid=(b, t // tq, t // ch),
        in_specs=[pl.BlockSpec((1, tq, d), lambda i, j, c: (i, j, 0)),
                  pl.BlockSpec((1, tq, gate.shape[2]), lambda i, j, c: (i, j, 0)),
                  pl.BlockSpec((1, KV_PAIR, t), lambda i, j, c: (i, 1, 0)),
                  pl.BlockSpec((1, KV_PAIR, t), lambda i, j, c: (i, 0, 0)),
                  pl.BlockSpec((1, n_c, KV_PAIR), lambda i, j, c: (i, 0, 0)),
                  pl.BlockSpec((N_HEADS, tq, n_c), lambda i, j, c: (0, j, 0)),
                  pl.BlockSpec((1, N_HEADS, tq, ch), tile_index),
                  pl.BlockSpec((1, N_HEADS, tq, NSA_WKEYS), lambda i, j, c: (jnp.minimum(j, NSA_WINDOW // tq), 0, 0, 0))],
        out_specs=pl.BlockSpec((1, tq, d), lambda i, j, c: (i, j, 0)),
        out_shape=jax.ShapeDtypeStruct((b, t, d), F32),
        scratch_shapes=[pltpu.VMEM((rows, HEAD_DIM), BF16), pltpu.VMEM((rows, HEAD_DIM), F32),
                        pltpu.VMEM((rows, 1), F32), pltpu.VMEM((rows, KV_PAIR), F32),
                        pltpu.VMEM((tq, -(-n_s // 128) * 128), BF16)],
        compiler_params=_cparams("parallel", "arbitrary", "arbitrary"),
        name="nsa_prompt",
    )(q, gate, kv_t, win_t, kvc, bias_c, tiles, wtiles)


def _nsa_sample_body(pt_ref, q_ref, gate_ref, new_ref, wnew_ref, cwin_ref, bc_ref, bs_ref, bw_ref, b0_ref,
                     wblk_ref, peh_ref, w2k_ref, w2v_ref, pool_ref, o_ref, wout_ref, buf_ref, sem_ref, cmp_ref,
                     ks_ref, vs_ref, expand_ref, *, n_pages, n_sp):
    slot = _paged_prefetch(pool_ref, buf_ref, sem_ref, pt_ref, n_pages)
    b = pl.program_id(0)
    past = n_pages * PAGE
    n_h = past // CMP_STRIDE
    q = (q_ref[0] * (HEAD_DIM ** -0.5)).astype(BF16)
    qf = q.astype(F32)

    for p in range(n_pages):
        cmp_ref[p * PAGE:(p + 1) * PAGE, :] = buf_ref[slot, p, 0:KV_PAIR, :].T
    kvc = _compress(lambda s: cmp_ref[pl.ds(s, n_h, stride=CMP_STRIDE), :], n_h, wblk_ref, peh_ref, w2k_ref, w2v_ref)
    kc = kvc[:, 0:HEAD_DIM].astype(BF16)
    vc = kvc[:, HEAD_DIM:KV_PAIR].astype(BF16)
    cend = lax.broadcasted_iota(jnp.int32, (N_HEADS, n_h), 1) * CMP_STRIDE + CMP_BLOCK
    valid_c = cend <= past + 1
    s = jnp.where(valid_c, _dot_nt(q, kc) + bc_ref[...], NEG_INF)
    m = jnp.max(s, axis=1, keepdims=True)
    e = jnp.where(valid_c, jnp.exp(s - m), 0.0)
    p_c = e / jnp.maximum(jnp.sum(e, axis=1, keepdims=True), 1e-30)
    o_c = _dot(p_c.astype(BF16), vc)

    psum = jnp.broadcast_to(jnp.sum(p_c, axis=0, keepdims=True), (8, n_h))
    imp = _split_dot(psum, _intersect_matrix(n_h, n_sp))[0:1, :]
    qblk = past // SLC_BLOCK
    jb = lax.broadcasted_iota(jnp.int32, (1, n_sp), 1)
    forced = (jb == 0) | (jb == qblk) | (jb == qblk - 1)
    score = jnp.where(jb <= qblk, imp + jnp.where(forced, FORCE_BONUS, 0.0), NEG_INF)
    r_idx = lax.broadcasted_iota(jnp.int32, (n_sp, n_sp), 0)
    c_idx = lax.broadcasted_iota(jnp.int32, (n_sp, n_sp), 1)
    score_b = jnp.broadcast_to(score, (n_sp, n_sp))
    score_col = jnp.sum(jnp.where(r_idx == c_idx, score_b, 0.0), axis=1, keepdims=True)
    beats = (score_col > score_b) | ((score_col == score_b) & (r_idx < c_idx))
    rank = jnp.sum(beats.astype(F32), axis=0, keepdims=True)
    sel = jnp.broadcast_to((rank < N_SELECT).astype(BF16), (8, n_sp))
    sel_new = rank[:, qblk:qblk + 1] < N_SELECT

    def attend(s_old, v_old_t, kv_new, new_ok):
        k_new = kv_new[:, 0:HEAD_DIM].astype(BF16).astype(F32)
        v_new = kv_new[:, HEAD_DIM:KV_PAIR].astype(BF16).astype(F32)
        s_new = jnp.where(new_ok, jnp.sum(qf * k_new, axis=1, keepdims=True) + b0_ref[:, 0:1], NEG_INF)
        m = jnp.maximum(jnp.maximum(jnp.max(s_old, axis=1, keepdims=True), s_new), MASKED_ROW_FLOOR)
        p_old = jnp.exp(s_old - m)
        p_new = jnp.exp(s_new - m)
        den = jnp.sum(p_old, axis=1, keepdims=True) + p_new
        acc = _dot_nt(p_old.astype(BF16), v_old_t) + p_new.astype(BF16).astype(F32) * v_new
        return acc / jnp.maximum(den, 1e-30)

    @pl.when(b == 0)
    def _():
        blk = lax.broadcasted_iota(jnp.int32, expand_ref.shape, 1) >> 6
        expand_ref[...] = (blk == lax.broadcasted_iota(jnp.int32, expand_ref.shape, 0)).astype(BF16)

    for p in range(n_pages):
        lanes = slice(p * PAGE, (p + 1) * PAGE)
        ks_ref[:, lanes] = buf_ref[slot, p, KV_PAIR:KV_PAIR + HEAD_DIM, :].astype(BF16)
        vs_ref[:, lanes] = buf_ref[slot, p, KV_PAIR + HEAD_DIM:2 * KV_PAIR, :].astype(BF16)
    picked = _dot(sel, expand_ref[...])[0:1, :] > 0.5
    s = _dot(q, ks_ref[...]) + bs_ref[...] + jnp.where(picked, 0.0, NEG_INF)
    o_s = attend(s, vs_ref[...], new_ref[0, 0:1, :], jnp.broadcast_to(sel_new, (N_HEADS, 1)))

    cwin = cwin_ref[0]
    s = _dot(q, cwin[0:HEAD_DIM, :].astype(BF16)) + bw_ref[...]
    o_w = attend(s, cwin[HEAD_DIM:KV_PAIR, :].astype(BF16), new_ref[0, 1:2, :], jnp.full((N_HEADS, 1), True))

    gates = jax.nn.sigmoid(gate_ref[0])
    o_ref[0] = gates[:, 0:1] * o_c + gates[:, 1:2] * o_s + gates[:, 2:3] * o_w

    wb = cwin.shape[1]
    pick = lax.broadcasted_iota(jnp.int32, wnew_ref.shape, 1) == b
    new_col = jnp.sum(jnp.where(pick, wnew_ref[...], 0.0), axis=1, keepdims=True)
    last = lax.broadcasted_iota(jnp.int32, cwin.shape, 1) == wb - 1
    wout_ref[0] = jnp.where(last, new_col, pltpu.roll(cwin, wb - 1, 1))


def _nsa_sample(q, gate_t, new, wnew_t, cwin_t, biases, consts, pool, page_table):
    nb = q.shape[0]
    n_pages = page_table.shape[1]
    past = n_pages * PAGE
    assert cwin_t.shape[2] <= NSA_WINDOW
    n_s = past // SLC_BLOCK + 1
    n_sp = -(-n_s // 128) * 128
    per_b = lambda a: pl.BlockSpec((1,) + a.shape[1:], lambda i, pt: (i,) + (0,) * (a.ndim - 1))
    const = lambda a: pl.BlockSpec(a.shape, lambda i, pt: (0,) * a.ndim)
    grid_spec = pltpu.PrefetchScalarGridSpec(
        num_scalar_prefetch=1,
        grid=(nb,),
        in_specs=[per_b(q), per_b(gate_t), per_b(new), const(wnew_t), per_b(cwin_t)]
        + [const(a) for a in biases] + [const(a) for a in consts]
        + [pl.BlockSpec(memory_space=pl.ANY)],
        out_specs=[pl.BlockSpec((1, N_HEADS, HEAD_DIM), lambda i, pt: (i, 0, 0)), per_b(cwin_t)],
        scratch_shapes=[pltpu.VMEM((2, n_pages, pool.shape[1], PAGE), F32), pltpu.SemaphoreType.DMA((2,)),
                        pltpu.VMEM((past, KV_PAIR), F32), pltpu.VMEM((HEAD_DIM, past), BF16),
                        pltpu.VMEM((HEAD_DIM, past), BF16), pltpu.VMEM((n_sp, past), BF16)],
    )
    return pl.pallas_call(
        functools.partial(_nsa_sample_body, n_pages=n_pages, n_sp=n_sp),
        grid_spec=grid_spec,
        out_shape=[jax.ShapeDtypeStruct((nb, N_HEADS, HEAD_DIM), F32), jax.ShapeDtypeStruct(cwin_t.shape, F32)],
        compiler_params=_cparams("arbitrary"),
        name="nsa_sample",
    )(page_table, q, gate_t, new, wnew_t, cwin_t, *biases, *consts, pool)


def _nsa_layer(xp, xs, cache_kv, cache_win, page_table, w_in, pe, w1, w2, rel_bias):
    b, t, d = xp.shape
    nb = xs.shape[0]
    past = page_table.shape[1] * PAGE
    w_q, w_gate = w_in[:, :d].astype(BF16), w_in[:, d + 3 * KV_PAIR:].astype(BF16)
    w_cmp, w_slc, w_win = [w_in[:, d + i * KV_PAIR:d + (i + 1) * KV_PAIR].astype(BF16) for i in range(3)]
    wts = [w_in[:, d:d + 2 * KV_PAIR].T.astype(BF16), w_in[:, d + 2 * KV_PAIR:d + 3 * KV_PAIR].T.astype(BF16)]

    w1r = w1.reshape(2, 2, CMP_STRIDE, HEAD_DIM, CMP_HIDDEN)
    zero = jnp.zeros((CMP_STRIDE, HEAD_DIM, CMP_HIDDEN), F32)
    top = jnp.concatenate([w1r[0, 0], w1r[0, 1], zero, zero], axis=2)
    bot = jnp.concatenate([zero, zero, w1r[1, 0], w1r[1, 1]], axis=2)
    wblk = jnp.concatenate([top, bot], axis=1).astype(BF16).reshape(CMP_STRIDE // 2, 2 * KV_PAIR, -1)
    pe8 = jnp.broadcast_to(pe.reshape(2, 1, -1), (2, 8, CMP_BLOCK * HEAD_DIM)).astype(BF16)
    peh = _whole_call(_nsa_pe_body, jax.ShapeDtypeStruct((8, 2 * CMP_HIDDEN), F32), "nsa_pe", pe8, w1.astype(BF16))
    zpad = jnp.zeros((CMP_HIDDEN, HEAD_DIM), F32)
    w2k = jnp.concatenate([w2[0], zpad], axis=1).astype(BF16)
    w2v = jnp.concatenate([zpad, w2[1]], axis=1).astype(BF16)
    consts = [wblk, peh, w2k, w2v]

    qp, cmp_p, gate_p, kv_t, win_t = _proj(xp.reshape(b * t, d), [w_q, w_cmp, w_gate], wts, b)
    r3 = lambda a: a.reshape(b, t, -1)
    kvc = _nsa_compress_prompt(r3(cmp_p), consts)
    nq = t // Q_TILE
    bias_c = _nsa_cmp_bias(rel_bias, t)
    n_rev = nq + max(NSA_WKEYS, NSA_PCHUNK) // Q_TILE
    rev = _bias_by_distance(rel_bias, Q_TILE * nq - 1 - np.arange(n_rev * Q_TILE))
    rev = jnp.transpose(rev.reshape(N_HEADS, n_rev, Q_TILE), (1, 0, 2))
    tiles = _nsa_bias_tiles(rev, nq, nq, NSA_PCHUNK)
    wtiles = _nsa_bias_tiles(rev, nq, NSA_WINDOW // Q_TILE + 1, NSA_WKEYS)
    attn_p = _nsa_prompt(r3(qp), r3(gate_p), kv_t, win_t, kvc, bias_c, tiles, wtiles)
    kv_p = _token_major(kv_t, (4, 1, HEAD_DIM))
    win_out_p = _token_major(win_t[:, :, t - min(NSA_WINDOW, t):], (2, 1, HEAD_DIM))

    qs, slc_s, win_s, gate_s, kvs_t, wins_t = _proj(xs.reshape(nb, d), [w_q, w_slc, w_win, w_gate], wts, 1)
    wb = cache_win.shape[1]
    n_h = past // CMP_STRIDE
    bc = _bias_by_distance(rel_bias, past - (np.arange(n_h) * CMP_STRIDE + CMP_BLOCK - 1))
    bs = _bias_by_distance(rel_bias, past - np.arange(past))
    bw = _bias_by_distance(rel_bias, wb - np.arange(wb))
    b0 = _bias_by_distance(rel_bias, np.zeros((128,), np.int64))
    gate_t = jnp.transpose(gate_s.reshape(nb, 3, N_HEADS), (0, 2, 1))
    new = jnp.stack([slc_s, win_s], axis=1)
    cwin_t = jnp.transpose(cache_win, (0, 2, 3, 4, 1)).reshape(nb, KV_PAIR, wb)
    attn_s, wout_t = _nsa_sample(qs.reshape(nb, N_HEADS, HEAD_DIM), gate_t, new, wins_t[0], cwin_t,
                                 [bc, bs, bw, b0], consts, _feature_major_pool(cache_kv), page_table)
    kv_s = jnp.transpose(kvs_t[0], (1, 0)).reshape(nb, 1, 4, 1, HEAD_DIM)
    win_out_s = _token_major(wout_t, (2, 1, HEAD_DIM))
    return attn_p.reshape(b * t, d), attn_s.reshape(nb, d), kv_p, kv_s, win_out_p, win_out_s


def kernel(x_prompt, x_sample, cache_nsa_kv, cache_nsa_win, cache_mla, state_dil_w128, state_dil_w512,
           state_dil_w2048, cache_sb_kv, page_table, p_prompt, p_sample, rel_bias, ln1_g, ln1_b, ln2_g, ln2_b,
           ffn_wg, ffn_wu, ffn_wd, ple_wg, ple_wp, nsa_w_in, nsa_cmp_pe, nsa_cmp_w1, nsa_cmp_w2, nsa_w_out,
           mla_w_dq, mla_q_norm, mla_kv_norm, mla_w_uq, mla_w_uk, mla_w_uv, mla_w_out, dil_w_in, dil_w_out,
           sb_w_in, sb_w_out):
    b, t, d = x_prompt.shape
    nb = x_sample.shape[0]
    past_len = page_table.shape[1] * PAGE
    depth = p_prompt.shape[0]
    n_mixers = 4
    dil_states = (state_dil_w128, state_dil_w512, state_dil_w2048)
    xp = x_prompt.reshape(b * t, d)
    xs = x_sample.reshape(nb, d)
    pp_all = p_prompt.reshape(depth, b * t, -1)
    ps_all = p_sample.reshape(depth, nb, -1)
    outs = {k: [] for k in ("nsa_kv_p", "nsa_kv_s", "nsa_win_p", "nsa_win_s", "mla_p", "mla_s", "sb_p", "sb_s")}
    dil_p = [[] for _ in DIL_PATTERNS]
    dil_s = [[] for _ in DIL_PATTERNS]
    for i in range(depth):
        kind, j = i % n_mixers, i // n_mixers
        xp3, xs3 = xp.reshape(b, t, d), xs.reshape(nb, 1, d)
        if kind == 0:
            mp, ms, a, b_, c, e = _nsa_layer(xp3, xs3, cache_nsa_kv[j], cache_nsa_win[j], page_table, nsa_w_in[j],
                                            nsa_cmp_pe[j], nsa_cmp_w1[j], nsa_cmp_w2[j], rel_bias)
            mp, ms, w_out = [mp], [ms], nsa_w_out[j]
            outs["nsa_kv_p"].append(a)
            outs["nsa_kv_s"].append(b_)
            outs["nsa_win_p"].append(c)
            outs["nsa_win_s"].append(e)
        elif kind == 1:
            mp, ms, a, b_ = _mla_layer(xp3, xs3, cache_mla[j], page_table, mla_w_dq[j], mla_q_norm[j], mla_kv_norm[j],
                                       mla_w_uq[j], mla_w_uk[j], mla_w_uv[j], past_len)
            mp, ms, w_out = [mp], [ms], mla_w_out[j]
            outs["mla_p"].append(a)
            outs["mla_s"].append(b_)
        elif kind == 2:
            mp, ms, st_p, st_s = _dil_layer(xp3, xs3, [s[j] for s in dil_states], dil_w_in[j], rel_bias)
            ms, w_out = [ms], dil_w_out[j]
            for g in range(len(DIL_PATTERNS)):
                dil_p[g].append(st_p[g])
                dil_s[g].append(st_s[g])
        else:
            mp, ms, a, b_ = _sb_layer(xp3, xs3, cache_sb_kv[j], page_table, sb_w_in[j])
            mp, ms, w_out = [mp], [ms], sb_w_out[j]
            outs["sb_p"].append(a)
            outs["sb_s"].append(b_)
        row = lambda v: v.reshape(1, -1)
        consts = (w_out.astype(BF16), row(ln1_g[i]), row(ln1_b[i]), row(ln2_g[i]), row(ln2_b[i]),
                  ffn_wg[i].astype(BF16), ffn_wu[i].astype(BF16), ffn_wd[i].astype(BF16),
                  ple_wg[i].astype(BF16), ple_wp[i].astype(BF16))
        xp = _tail(mp, xp, pp_all, i, *consts)
        xs = _tail(ms, xs, ps_all, i, *consts)
    st = jnp.stack
    return (xp.reshape(b, t, d), xs.reshape(nb, 1, d),
            st(outs["nsa_kv_p"]), st(outs["nsa_kv_s"]), st(outs["nsa_win_p"]), st(outs["nsa_win_s"]),
            st(outs["mla_p"]), st(outs["mla_s"]),
            st(dil_p[0]), st(dil_s[0]), st(dil_p[1]), st(dil_s[1]), st(dil_p[2]), st(dil_s[2]),
            st(outs["sb_p"]), st(outs["sb_s"]))
```

```python
import functools
import math

import numpy as np
import jax
import jax.numpy as jnp
from jax import lax
from jax.experimental import pallas as pl
from jax.experimental.pallas import tpu as pltpu

F32 = jnp.float32
BF16 = jnp.bfloat16

HEAD_DIM = 64
N_HEADS = 16
PAGE = 128
Q_TILE = 128
LN_EPS = 1e-5
NEG_INF = -1e30
DEPTH = 4
ALPHA = (2 * DEPTH) ** 0.25
N_BUCKETS = 32
MAX_DISTANCE = 2048
CMP_BLOCK = 32
CMP_STRIDE = 16
SLC_BLOCK = 64
N_SELECT = 16
NSA_WINDOW = 512
FORCE_BONUS = 1e4
DIL_PATTERNS = ((128, 1), (512, 4), (2048, 16))
ROPE_THETA = 10000.0
QK_ROPE = 32
KV_LORA = 256
Q_LORA = 256
SB_DEAD = -104.0
MASKED_ROW_FLOOR = -1e29
VMEM_LIMIT_BYTES = 56 * 1024 * 1024


def _cparams(*sem):
    return pltpu.CompilerParams(dimension_semantics=sem, vmem_limit_bytes=VMEM_LIMIT_BYTES)


def _dot(a, b):
    return jnp.dot(a, b, preferred_element_type=F32)


def _dot_nt(a, b):
    return lax.dot_general(a, b, (((1,), (1,)), ((), ())), preferred_element_type=F32)


def _const_spec(shape):
    nd = len(shape)
    return pl.BlockSpec(shape, lambda *_: (0,) * nd)


def _bucket_of_distance(n_dist):
    n = np.arange(n_dist, dtype=np.int64)
    max_exact = N_BUCKETS // 2
    ratio = np.maximum(n, max_exact).astype(np.float32) / np.float32(max_exact)
    log_ratio = np.log(ratio).astype(np.float32) / np.float32(math.log(MAX_DISTANCE / max_exact))
    large = np.minimum(max_exact + (log_ratio * np.float32(N_BUCKETS - max_exact)).astype(np.int32), N_BUCKETS - 1)
    return np.where(n < max_exact, n, large).astype(np.int32)


def _bias_by_distance(rel_bias, dists):
    d = np.asarray(dists)
    bucket = _bucket_of_distance(int(d.max()) + 1)[np.maximum(d, 0)]
    vals = jnp.moveaxis(rel_bias.astype(F32)[bucket], -1, 0)
    return jnp.where(jnp.asarray(d >= 0), vals, 0.0)


def _proj_body(x_ref, *refs, n_row, n_col):
    xb = x_ref[...].astype(BF16)
    n = n_row + n_col
    for w_ref, o_ref in zip(refs[:n_row], refs[n:n + n_row]):
        o_ref[...] = _dot(xb, w_ref[...])
    for w_ref, o_ref in zip(refs[n_row:n], refs[n + n_row:]):
        o_ref[0] = _dot_nt(w_ref[...], xb)


def _proj(x, ws, wts, n_batch):
    m, k = x.shape
    tm = _row_tile(m)
    t = m // n_batch
    per_b = t // tm
    return pl.pallas_call(
        functools.partial(_proj_body, n_row=len(ws), n_col=len(wts)),
        grid=(m // tm,),
        in_specs=[pl.BlockSpec((tm, k), lambda i: (i, 0))] + [_const_spec(w.shape) for w in ws + wts],
        out_specs=[pl.BlockSpec((tm, w.shape[1]), lambda i: (i, 0)) for w in ws]
        + [pl.BlockSpec((1, w.shape[0], tm), lambda i: (i // per_b, 0, i % per_b)) for w in wts],
        out_shape=[jax.ShapeDtypeStruct((m, w.shape[1]), F32) for w in ws]
        + [jax.ShapeDtypeStruct((n_batch, w.shape[0], t), F32) for w in wts],
        compiler_params=_cparams("parallel"),
        name="proj",
    )(x, *ws, *wts)


def _row_tile(m):
    return 256 if m % 256 == 0 else m


def _layer_norm(x, g, b):
    mu = jnp.mean(x, axis=-1, keepdims=True)
    xc = x - mu
    var = jnp.mean(xc * xc, axis=-1, keepdims=True)
    return xc * lax.rsqrt(var + LN_EPS) * g + b


FF_CHUNK = 256
TAIL_ROWS = 512


def _tail_body(*refs, n_mix):
    mix_refs = refs[:n_mix]
    (x_ref, p_ref, wo_ref, g1_ref, b1_ref, g2_ref, b2_ref, wg_ref, wu_ref, wd_ref,
     pwg_ref, pwp_ref, o_ref) = refs[n_mix:]
    if n_mix == 1:
        attn = mix_refs[0][...]
    else:
        n_g = n_mix // 2
        lses = [r[...] for r in mix_refs[n_g:]]
        m = functools.reduce(jnp.maximum, lses)
        es = [jnp.exp(l - m) for l in lses]
        den = functools.reduce(lambda a, b: a + b, es)
        ws = [e / den for e in es]
        cols = []
        for h in range(N_HEADS):
            sl = slice(h * HEAD_DIM, (h + 1) * HEAD_DIM)
            acc = None
            for g in range(n_g):
                term = ws[g][:, h:h + 1] * mix_refs[g][:, sl]
                acc = term if acc is None else acc + term
            cols.append(acc)
        attn = jnp.concatenate(cols, axis=1)
    x = x_ref[...]
    mix = _dot(attn.astype(BF16), wo_ref[...])
    h1 = _layer_norm(ALPHA * x + mix, g1_ref[...], b1_ref[...])
    h1b = h1.astype(BF16)
    d_ff = wg_ref.shape[1]
    acc = jnp.zeros(x.shape, F32)
    for c in range(d_ff // FF_CHUNK):
        sl = slice(c * FF_CHUNK, (c + 1) * FF_CHUNK)
        g = _dot(h1b, wg_ref[:, sl])
        u = _dot(h1b, wu_ref[:, sl])
        acc = acc + _dot((g * jax.nn.sigmoid(g) * u).astype(BF16), wd_ref[sl, :])
    h2 = _layer_norm(ALPHA * h1 + acc, g2_ref[...], b2_ref[...])
    gate = jax.nn.sigmoid(_dot(h2.astype(BF16), pwg_ref[...]))
    o_ref[...] = h2 + gate * _dot(p_ref[...].astype(BF16), pwp_ref[...])


def _tail(mix_list, x, p_all, layer, wo, g1, b1, g2, b2, wg, wu, wd, pwg, pwp):
    m, d = x.shape
    tm = TAIL_ROWS if m % TAIL_ROWS == 0 else _row_tile(m)
    consts = [wo, g1, b1, g2, b2, wg, wu, wd, pwg, pwp]
    row = lambda a: pl.BlockSpec((tm, a.shape[1]), lambda i: (i, 0))
    single = lambda a: pl.BlockSpec(a.shape, lambda i: (0, 0), pipeline_mode=pl.Buffered(1))
    return pl.pallas_call(
        functools.partial(_tail_body, n_mix=len(mix_list)),
        grid=(m // tm,),
        in_specs=[row(a) for a in mix_list]
        + [row(x), pl.BlockSpec((None, tm, p_all.shape[2]), lambda i: (layer, i, 0))] + [single(a) for a in consts],
        out_specs=pl.BlockSpec((tm, d), lambda i: (i, 0)),
        out_shape=jax.ShapeDtypeStruct((m, d), F32),
        compiler_params=_cparams("parallel"),
        name="tail",
    )(*mix_list, x, p_all, *consts)


def _page_copies(pool_ref, buf_ref, sem_ref, pt_ref, b, slot, n_pages, start):
    def body(p, carry):
        if len(buf_ref.shape) == 4:
            dst = buf_ref.at[slot, p]
        else:
            rows = pool_ref.shape[1]
            dst = buf_ref.at[slot, pl.ds(pl.multiple_of(p * rows, rows), rows)]
        cp = pltpu.make_async_copy(pool_ref.at[pt_ref[b, p]], dst, sem_ref.at[slot])
        if start:
            cp.start()
        else:
            cp.wait()
        return carry
    lax.fori_loop(0, n_pages, body, 0)


def _paged_prefetch(pool_ref, buf_ref, sem_ref, pt_ref, n_pages):
    b = pl.program_id(0)
    nb = pl.num_programs(0)
    slot = b % 2

    @pl.when(b == 0)
    def _():
        _page_copies(pool_ref, buf_ref, sem_ref, pt_ref, 0, 0, n_pages, True)

    @pl.when(b + 1 < nb)
    def _():
        _page_copies(pool_ref, buf_ref, sem_ref, pt_ref, b + 1, 1 - slot, n_pages, True)

    _page_copies(pool_ref, buf_ref, sem_ref, pt_ref, b, slot, n_pages, False)
    return slot


def _sb_terms(z, valid):
    t = jnp.log(1.0 + jnp.exp(-jnp.abs(z)))
    l1m = -jnp.maximum(z, 0.0) - t
    if valid is not None:
        l1m = jnp.where(valid, l1m, 0.0)
    ls = jnp.minimum(z, 0.0) - t
    return l1m, ls


def _strict_upper_sum_matrix(n):
    j = lax.broadcasted_iota(jnp.int32, (2 * n, n), 0) & (n - 1)
    s = lax.broadcasted_iota(jnp.int32, (2 * n, n), 1)
    return (j > s).astype(BF16)


def _tail_sums(l1m, uu):
    hi = l1m.astype(BF16)
    lo = (l1m - hi.astype(F32)).astype(BF16)
    return _dot(jnp.concatenate([hi, lo], axis=1), uu)


SB_GROUPS_PER_LOOP = 4


def _sb_prompt_body(q_ref, kv_ref, o_ref, r_ref, acc_ref, *, n_kv, grp):
    qi = pl.program_id(1)
    tq = q_ref.shape[1]
    rows = grp * tq
    uu = _strict_upper_sum_matrix(Q_TILE)
    qpos = qi * tq + (lax.broadcasted_iota(jnp.int32, (rows, Q_TILE), 0) & (tq - 1))
    lane = lax.broadcasted_iota(jnp.int32, (rows, Q_TILE), 1)
    for g0 in range(0, n_kv, SB_GROUPS_PER_LOOP):
        groups = range(g0, g0 + SB_GROUPS_PER_LOOP)
        qgs = []
        for g in groups:
            qg = jnp.concatenate(
                [q_ref[0, :, (g * grp + u) * HEAD_DIM:(g * grp + u + 1) * HEAD_DIM] for u in range(grp)], axis=0)
            qgs.append((qg * (HEAD_DIM ** -0.5)).astype(BF16))
        r_ref[...] = jnp.zeros(r_ref.shape, F32)
        acc_ref[...] = jnp.zeros(acc_ref.shape, F32)

        def cond(c):
            return jnp.logical_and(c[0] >= 0, c[1] > SB_DEAD)

        def body(c):
            kj = c[0]
            off = pl.multiple_of(kj * Q_TILE, Q_TILE)
            valid = (kj * Q_TILE + lane) < qpos
            alive = jnp.float32(-jnp.inf)
            for i, g in enumerate(groups):
                part = slice(i * rows, (i + 1) * rows)
                k = kv_ref[0, g * HEAD_DIM:(g + 1) * HEAD_DIM, pl.ds(off, Q_TILE)].astype(BF16)
                v = kv_ref[0, (n_kv + g) * HEAD_DIM:(n_kv + g + 1) * HEAD_DIM, pl.ds(off, Q_TILE)].astype(BF16)
                l1m, ls = _sb_terms(_dot(qgs[i], k), valid)
                r = r_ref[part, :]
                a = jnp.where(valid, jnp.exp(ls + _tail_sums(l1m, uu) + r), 0.0)
                acc_ref[part, :] += _dot_nt(a.astype(BF16), v)
                rn = r + jnp.sum(l1m, axis=1, keepdims=True)
                r_ref[part, :] = rn
                alive = jnp.maximum(alive, jnp.max(rn))
            return kj - 1, alive

        lax.while_loop(cond, body, (qi, jnp.float32(0.0)))
        for i, g in enumerate(groups):
            for u in range(grp):
                h = g * grp + u
                o_ref[0, :, h * HEAD_DIM:(h + 1) * HEAD_DIM] = acc_ref[i * rows + u * tq:i * rows + (u + 1) * tq, :]


def _sb_prompt(q, kv, n_kv):
    b, t, d = q.shape
    grp = N_HEADS // n_kv
    assert n_kv % SB_GROUPS_PER_LOOP == 0
    rows = SB_GROUPS_PER_LOOP * grp * Q_TILE
    return pl.pallas_call(
        functools.partial(_sb_prompt_body, n_kv=n_kv, grp=grp),
        grid=(b, t // Q_TILE),
        in_specs=[pl.BlockSpec((1, Q_TILE, d), lambda i, j: (i, j, 0)),
                  pl.BlockSpec((1, kv.shape[1], t), lambda i, j: (i, 0, 0))],
        out_specs=pl.BlockSpec((1, Q_TILE, d), lambda i, j: (i, j, 0)),
        out_shape=jax.ShapeDtypeStruct((b, t, d), F32),
        scratch_shapes=[pltpu.VMEM((rows, 1), F32), pltpu.VMEM((rows, HEAD_DIM), F32)],
        compiler_params=_cparams("parallel", "arbitrary"),
        name="sb_prompt",
    )(q, kv)


def _head_spread(n_kv):
    d = lax.broadcasted_iota(jnp.int32, (HEAD_DIM, n_kv * HEAD_DIM), 0)
    c = lax.broadcasted_iota(jnp.int32, (HEAD_DIM, n_kv * HEAD_DIM), 1)
    return ((c & (HEAD_DIM - 1)) == d).astype(BF16)


def _block_diag_q(q, n_kv, scale):
    grp = N_HEADS // n_kv
    width = n_kv * HEAD_DIM
    spread = _dot((q * scale).astype(BF16), _head_spread(n_kv))
    row = lax.broadcasted_iota(jnp.int32, (N_HEADS, width), 0)
    col = lax.broadcasted_iota(jnp.int32, (N_HEADS, width), 1)
    own = (col >> 6) == (row >> int(math.log2(grp)))
    return jnp.where(own, spread, 0.0).astype(BF16)


def _block_diag_pick(o_full, n_kv):
    grp = N_HEADS // n_kv
    row = lax.broadcasted_iota(jnp.int32, (N_HEADS, HEAD_DIM), 0)
    out = jnp.zeros((N_HEADS, HEAD_DIM), F32)
    for g in range(n_kv):
        out = out + jnp.where((row >> int(math.log2(grp))) == g, o_full[:, g * HEAD_DIM:(g + 1) * HEAD_DIM], 0.0)
    return out


SB_EAGER_PAGES = 4


def _sb_sample_body(pt_ref, q_ref, pool_ref, o_ref, buf_ref, sem_ref, old_ref, old_sem, *, n_kv, n_pages):
    b = pl.program_id(0)
    nb = pl.num_programs(0)
    slot = b % 2
    n_eager = buf_ref.shape[1]
    first = n_pages - n_eager

    def eager(bb, sl, start):
        for i in range(n_eager):
            cp = pltpu.make_async_copy(pool_ref.at[pt_ref[bb, first + i]], buf_ref.at[sl, i], sem_ref.at[sl])
            if start:
                cp.start()
            else:
                cp.wait()

    @pl.when(b == 0)
    def _():
        eager(0, 0, True)

    @pl.when(b + 1 < nb)
    def _():
        eager(b + 1, 1 - slot, True)

    eager(b, slot, False)
    width = n_kv * HEAD_DIM
    qbd = _block_diag_q(q_ref[0], n_kv, HEAD_DIM ** -0.5)
    uu = _strict_upper_sum_matrix(PAGE)

    def step(page, r, acc):
        l1m, ls = _sb_terms(_dot(qbd, page[0:width, :].astype(BF16)), None)
        a = jnp.exp(ls + _tail_sums(l1m, uu) + r)
        acc = acc + _dot_nt(a.astype(BF16), page[width:2 * width, :].astype(BF16))
        rn = r + jnp.sum(l1m, axis=1, keepdims=True)
        return jnp.max(rn), rn, acc

    def cond(c):
        return jnp.logical_and(c[0] >= 0, c[1] > SB_DEAD)

    def newest(c):
        i, _, r, acc = c
        return (i - 1,) + step(buf_ref[slot, i], r, acc)

    def older(c):
        p, _, r, acc = c
        cp = pltpu.make_async_copy(pool_ref.at[pt_ref[b, p]], old_ref, old_sem.at[0])
        cp.start()
        cp.wait()
        return (p - 1,) + step(old_ref[...], r, acc)

    init = (jnp.int32(n_eager - 1), jnp.float32(0.0), jnp.zeros((N_HEADS, 1), F32), jnp.zeros((N_HEADS, width), F32))
    _, alive, r, acc = lax.while_loop(cond, newest, init)
    acc = lax.while_loop(cond, older, (jnp.int32(first - 1), alive, r, acc))[3]
    o_ref[0] = _block_diag_pick(acc, n_kv)


def _sb_sample(q, pool, page_table, n_kv):
    nb = q.shape[0]
    n_pages = page_table.shape[1]
    feat = pool.shape[1]
    n_eager = min(SB_EAGER_PAGES, n_pages)
    grid_spec = pltpu.PrefetchScalarGridSpec(
        num_scalar_prefetch=1,
        grid=(nb,),
        in_specs=[pl.BlockSpec((1, N_HEADS, HEAD_DIM), lambda i, pt: (i, 0, 0)),
                  pl.BlockSpec(memory_space=pl.ANY)],
        out_specs=pl.BlockSpec((1, N_HEADS, HEAD_DIM), lambda i, pt: (i, 0, 0)),
        scratch_shapes=[pltpu.VMEM((2, n_eager, feat, PAGE), F32), pltpu.SemaphoreType.DMA((2,)),
                        pltpu.VMEM((feat, PAGE), F32), pltpu.SemaphoreType.DMA((1,))],
    )
    return pl.pallas_call(
        functools.partial(_sb_sample_body, n_kv=n_kv, n_pages=n_pages),
        grid_spec=grid_spec,
        out_shape=jax.ShapeDtypeStruct((nb, N_HEADS, HEAD_DIM), F32),
        compiler_params=_cparams("arbitrary"),
        name="sb_sample",
    )(page_table, q, pool)


def _feature_major_pool(cache):
    n = cache.ndim
    return jnp.transpose(cache, (0,) + tuple(range(2, n)) + (1,)).reshape(cache.shape[0], -1, cache.shape[1])


def _token_major(x_t, feat_shape):
    b, _, t = x_t.shape
    nf = len(feat_shape)
    return jnp.transpose(x_t.reshape((b,) + tuple(feat_shape) + (t,)), (0, nf + 1) + tuple(range(1, nf + 1)))


def _sb_layer(xp, xs, cache, page_table, w_in):
    b, t, d = xp.shape
    n_kv = (w_in.shape[1] - d) // (2 * HEAD_DIM)
    ws, wts = [w_in[:, :d].astype(BF16)], [w_in[:, d:].T.astype(BF16)]
    qp, kvp_t = _proj(xp.reshape(b * t, d), ws, wts, b)
    attn_p = _sb_prompt(qp.reshape(b, t, d), kvp_t, n_kv)
    nb = xs.shape[0]
    qs, kvs_t = _proj(xs.reshape(nb, d), ws, wts, 1)
    attn_s = _sb_sample(qs.reshape(nb, N_HEADS, HEAD_DIM), _feature_major_pool(cache), page_table, n_kv)
    kv_s = _token_major(kvs_t, (2, n_kv, HEAD_DIM)).reshape(nb, 1, 2, n_kv, HEAD_DIM)
    return attn_p.reshape(b * t, d), attn_s.reshape(nb, d), _token_major(kvp_t, (2, n_kv, HEAD_DIM)), kv_s


def _toeplitz(base_row, rows, shift):
    return pltpu.roll(jnp.broadcast_to(base_row, (rows, base_row.shape[1])), shift, 1, stride=1, stride_axis=0)


def _dil_prompt_body(q_ref, kvc_ref, kvp_ref, base_ref, o_ref, lse_ref, *, n_kv, grp):
    mi = pl.program_id(2)
    tq = Q_TILE
    i_idx = lax.broadcasted_iota(jnp.int32, (tq, 2 * tq), 0)
    j_idx = lax.broadcasted_iota(jnp.int32, (tq, 2 * tq), 1)
    steps = i_idx - j_idx + tq
    mask = jnp.where((steps >= 0) & (steps <= tq) & ((mi > 0) | (j_idx >= tq)), 0.0, NEG_INF)
    lane = lax.broadcasted_iota(jnp.int32, (tq, 128), 1)
    lse_tile = jnp.zeros((tq, 128), F32)
    width = n_kv * HEAD_DIM
    for g in range(n_kv):
        ksl = slice(g * HEAD_DIM, (g + 1) * HEAD_DIM)
        vsl = slice(width + g * HEAD_DIM, width + (g + 1) * HEAD_DIM)
        k = jnp.concatenate([kvp_ref[0, :, ksl], kvc_ref[0, :, ksl]], axis=0).astype(BF16)
        v = jnp.concatenate([kvp_ref[0, :, vsl], kvc_ref[0, :, vsl]], axis=0).astype(BF16)
        heads = range(g * grp, (g + 1) * grp)
        qg = jnp.concatenate([q_ref[0, :, h * HEAD_DIM:(h + 1) * HEAD_DIM] for h in heads], axis=0)
        bias = jnp.stack([_toeplitz(base_ref[h:h + 1, :], tq, 0) for h in heads])
        s = _dot_nt((qg * (HEAD_DIM ** -0.5)).astype(BF16), k)
        s = (s.reshape(grp, tq, 2 * tq) + bias + mask[None]).reshape(grp * tq, 2 * tq)
        m = jnp.maximum(jnp.max(s, axis=1, keepdims=True), MASKED_ROW_FLOOR)
        e = jnp.exp(s - m)
        den = jnp.maximum(jnp.sum(e, axis=1, keepdims=True), 1e-30)
        o = _dot((e / den).astype(BF16), v)
        lse = m + jnp.log(den)
        for u, h in enumerate(heads):
            o_ref[0, :, h * HEAD_DIM:(h + 1) * HEAD_DIM] = o[u * tq:(u + 1) * tq, :]
            lse_tile = jnp.where(lane == h, lse[u * tq:(u + 1) * tq, :], lse_tile)
    lse_ref[0] = lse_tile


def _dil_prompt(q, kv, base, dil, n_kv):
    b, tm, _ = q.shape
    d = N_HEADS * HEAD_DIM
    kvw = 2 * n_kv * HEAD_DIM
    nm = tm // Q_TILE
    return pl.pallas_call(
        functools.partial(_dil_prompt_body, n_kv=n_kv, grp=N_HEADS // n_kv),
        grid=(b, dil, nm),
        in_specs=[pl.BlockSpec((1, Q_TILE, d), lambda i, r, m: (i, m, r)),
                  pl.BlockSpec((1, Q_TILE, kvw), lambda i, r, m: (i, m, r)),
                  pl.BlockSpec((1, Q_TILE, kvw), lambda i, r, m: (i, jnp.maximum(m - 1, 0), r)),
                  _const_spec(base.shape)],
        out_specs=[pl.BlockSpec((1, Q_TILE, d), lambda i, r, m: (i, m, r)),
                   pl.BlockSpec((1, Q_TILE, 128), lambda i, r, m: (i, m, r))],
        out_shape=[jax.ShapeDtypeStruct((b, tm, dil * d), F32), jax.ShapeDtypeStruct((b, tm, dil * 128), F32)],
        compiler_params=_cparams("parallel", "parallel", "arbitrary"),
        name="dil_prompt",
    )(q, kv, kv, base)


def _dil_sample_body(q_ref, kvn_ref, kvnt_ref, b0_ref, bm0_ref, bm1_ref, bm2_ref, st0_ref, st1_ref, st2_ref,
                     o_ref, so0_ref, so1_ref, so2_ref, *, n_kv):
    b = pl.program_id(0)
    width = n_kv * HEAD_DIM
    nb = kvnt_ref.shape[2]
    pick = lax.broadcasted_iota(jnp.int32, (2 * width, nb), 1) == b
    outs, lses = [], []
    groups = zip((st0_ref, st1_ref, st2_ref), (so0_ref, so1_ref, so2_ref), (bm0_ref, bm1_ref, bm2_ref))
    for g, (st, so, bm) in enumerate(groups):
        w = st.shape[2]
        kn = kvn_ref[0, g:g + 1, 0:width].astype(BF16).astype(F32)
        vn = kvn_ref[0, g:g + 1, width:2 * width].astype(BF16).astype(F32)
        qbd = _block_diag_q(q_ref[0, g], n_kv, HEAD_DIM ** -0.5)
        s_old = _dot(qbd, st[0, 0:width, :].astype(BF16)) + bm[...]
        s_new = jnp.sum(qbd.astype(F32) * kn, axis=1, keepdims=True) + b0_ref[:, 0:1]
        m = jnp.maximum(jnp.max(s_old, axis=1, keepdims=True), s_new)
        e_old = jnp.exp(s_old - m)
        e_new = jnp.exp(s_new - m)
        den = jnp.sum(e_old, axis=1, keepdims=True) + e_new
        o_full = (_dot_nt((e_old / den).astype(BF16), st[0, width:2 * width, :].astype(BF16))
                  + (e_new / den).astype(BF16).astype(F32) * vn)
        outs.append(_block_diag_pick(o_full, n_kv))
        lses.append(m + jnp.log(den))
        new_col = jnp.sum(jnp.where(pick, kvnt_ref[g], 0.0), axis=1, keepdims=True)
        last = lax.broadcasted_iota(jnp.int32, (2 * width, w), 1) == w - 1
        so[0] = jnp.where(last, new_col, pltpu.roll(st[0], w - 1, 1))
    m = functools.reduce(jnp.maximum, lses)
    es = [jnp.exp(l - m) for l in lses]
    den = functools.reduce(lambda a, b: a + b, es)
    o = None
    for e, og in zip(es, outs):
        o = (e / den) * og if o is None else o + (e / den) * og
    o_ref[0] = o


def _dil_sample(q, kvn, kvn_t, b0, bias_masks, states, n_kv):
    nb = q.shape[0]
    st_specs = [pl.BlockSpec((1,) + s.shape[1:], lambda i: (i, 0, 0)) for s in states]
    outs = pl.pallas_call(
        functools.partial(_dil_sample_body, n_kv=n_kv),
        grid=(nb,),
        in_specs=[pl.BlockSpec((1,) + q.shape[1:], lambda i: (i, 0, 0, 0)),
                  pl.BlockSpec((1,) + kvn.shape[1:], lambda i: (i, 0, 0)),
                  _const_spec(kvn_t.shape), _const_spec(b0.shape)]
        + [_const_spec(bm.shape) for bm in bias_masks] + st_specs,
        out_specs=[pl.BlockSpec((1, N_HEADS, HEAD_DIM), lambda i: (i, 0, 0))] + st_specs,
        out_shape=[jax.ShapeDtypeStruct((nb, N_HEADS, HEAD_DIM), F32)]
        + [jax.ShapeDtypeStruct(s.shape, F32) for s in states],
        compiler_params=_cparams("parallel"),
        name="dil_sample",
    )(q, kvn, kvn_t, b0, *bias_masks, *states)
    return outs[0], outs[1:]


def _dil_layer(xp, xs, states, w_in, rel_bias):
    b, t, d = xp.shape
    n_g = len(DIL_PATTERNS)
    w3 = w_in.reshape(d, n_g, -1)
    kvw = w3.shape[2] - d
    n_kv = kvw // (2 * HEAD_DIM)
    ws, wts = [], []
    for g in range(n_g):
        ws += [w3[:, g, :d].astype(BF16), w3[:, g, d:].astype(BF16)]
        wts.append(w3[:, g, d:].T.astype(BF16))
    outs_p = _proj(xp.reshape(b * t, d), ws, wts, b)
    lane = np.arange(2 * Q_TILE)
    mix, st_p = [], []
    for g, (w, dil) in enumerate(DIL_PATTERNS):
        assert w // dil == Q_TILE and t % (dil * Q_TILE) == 0
        base = _bias_by_distance(rel_bias, np.where(lane <= Q_TILE, (Q_TILE - lane) * dil, -1))
        o_g, lse_g = _dil_prompt(outs_p[2 * g].reshape(b, t // dil, dil * d),
                                 outs_p[2 * g + 1].reshape(b, t // dil, dil * kvw), base, dil, n_kv)
        mix.append((o_g.reshape(b * t, d), lse_g.reshape(b * t, 128)))
        st_p.append(_token_major(outs_p[2 * n_g + g][:, :, t - min(w, t):], (2, n_kv, HEAD_DIM)))
    mix_p = [m[0] for m in mix] + [m[1] for m in mix]
    nb = xs.shape[0]
    outs_s = _proj(xs.reshape(nb, d), ws, wts, 1)
    q_s = jnp.stack([outs_s[2 * g].reshape(nb, N_HEADS, HEAD_DIM) for g in range(n_g)], axis=1)
    kvn = jnp.stack([outs_s[2 * g + 1] for g in range(n_g)], axis=1)
    kvn_t = jnp.concatenate(outs_s[2 * n_g:], axis=0)
    bias_masks = []
    for g, (w, dil) in enumerate(DIL_PATTERNS):
        pos = np.arange(states[g].shape[1])
        bias = _bias_by_distance(rel_bias, np.where(pos % dil == 0, w - pos, -1))
        bias_masks.append(jnp.where(jnp.asarray(pos % dil == 0), bias, NEG_INF))
    b0 = _bias_by_distance(rel_bias, np.zeros((128,), np.int64))
    st_t = [jnp.transpose(s, (0, 2, 3, 4, 1)).reshape(nb, kvw, s.shape[1]) for s in states]
    attn_s, st_s = _dil_sample(q_s, kvn, kvn_t, b0, bias_masks, st_t, n_kv)
    st_s = [_token_major(s, (2, n_kv, HEAD_DIM)) for s in st_s]
    return mix_p, attn_s.reshape(nb, d), st_p, st_s


HEAD_PAD = 128
ROPE_HALF = QK_ROPE // 2
MLA_SCALE = (HEAD_DIM + QK_ROPE) ** -0.5


def _rms(x, g):
    return x * lax.rsqrt(jnp.mean(x * x, axis=-1, keepdims=True) + LN_EPS) * g


def _mla_project_body(x_ref, c_ref, s_ref, wdq_ref, qn_ref, kvn_ref, wq_ref, wqs_ref, wk_ref, wv_ref, vone_ref,
                      q_ref, k_ref, v_ref, lat_ref, latt_ref):
    h = _dot(x_ref[...].astype(BF16), wdq_ref[...])
    cq = _rms(h[:, 0:Q_LORA], qn_ref[...]).astype(BF16)
    ckv = _rms(h[:, Q_LORA:Q_LORA + KV_LORA], kvn_ref[...])
    ckvb = ckv.astype(BF16)
    cos, sin = c_ref[...], s_ref[...]
    base = Q_LORA + KV_LORA
    kr = h[:, base:base + HEAD_PAD] * cos + h[:, base + HEAD_PAD:base + 2 * HEAD_PAD] * sin
    cos_all = jnp.concatenate([cos] * N_HEADS, axis=1)
    sin_all = jnp.concatenate([sin] * N_HEADS, axis=1)
    q_ref[...] = ((_dot(cq, wq_ref[...]) * cos_all + _dot(cq, wqs_ref[...]) * sin_all) * MLA_SCALE).astype(BF16)
    k_ref[...] = (_dot(ckvb, wk_ref[...]) + jnp.concatenate([kr] * N_HEADS, axis=1)).astype(BF16)
    v_ref[...] = (_dot(ckvb, wv_ref[...]) + vone_ref[...]).astype(BF16)
    lat_ref[:, 0:KV_LORA] = ckv
    lat_ref[:, KV_LORA:KV_LORA + QK_ROPE] = kr[:, HEAD_DIM:HEAD_DIM + QK_ROPE]
    latt_ref[0, 0:KV_LORA, :] = ckv.T
    latt_ref[0, KV_LORA:KV_LORA + QK_ROPE, :] = kr.T[HEAD_DIM:HEAD_DIM + QK_ROPE, :]


def _mla_project(x, cos, sin, consts, n_batch):
    m, d = x.shape
    tm = _row_tile(m)
    t = m // n_batch
    per_b = t // tm
    pos_blocks = cos.shape[0] // tm
    lat_w = KV_LORA + QK_ROPE
    wide = N_HEADS * HEAD_PAD
    rope_spec = pl.BlockSpec((tm, HEAD_PAD), lambda i: (i % pos_blocks, 0))
    return pl.pallas_call(
        _mla_project_body,
        grid=(m // tm,),
        in_specs=[pl.BlockSpec((tm, d), lambda i: (i, 0)), rope_spec, rope_spec] + [_const_spec(c.shape) for c in consts],
        out_specs=[pl.BlockSpec((tm, wide), lambda i: (i, 0)), pl.BlockSpec((tm, wide), lambda i: (i, 0)),
                   pl.BlockSpec((tm, wide), lambda i: (i, 0)), pl.BlockSpec((tm, lat_w), lambda i: (i, 0)),
                   pl.BlockSpec((1, lat_w, tm), lambda i: (i // per_b, 0, i % per_b))],
        out_shape=[jax.ShapeDtypeStruct((m, wide), BF16), jax.ShapeDtypeStruct((m, wide), BF16),
                   jax.ShapeDtypeStruct((m, wide), BF16), jax.ShapeDtypeStruct((m, lat_w), F32),
                   jax.ShapeDtypeStruct((n_batch, lat_w, t), F32)],
        compiler_params=_cparams("parallel"),
        name="mla_project",
    )(x, cos, sin, *consts)


MLA_TQ = 256
MLA_CHUNK = 1024
MLA_HEADS_PER_STEP = 4


def _mla_prompt_body(q_ref, k_ref, v_ref, o_ref, m_ref, acc_ref):
    qi = pl.program_id(2)
    tq, ch = MLA_TQ, MLA_CHUNK
    q0 = qi * tq
    m_ref[...] = jnp.full(m_ref.shape, NEG_INF, F32)
    acc_ref[...] = jnp.zeros(acc_ref.shape, F32)
    i_idx = lax.broadcasted_iota(jnp.int32, (tq, ch), 0)
    j_idx = lax.broadcasted_iota(jnp.int32, (tq, ch), 1)

    def chunk(c, carry):
        k0 = pl.multiple_of(c * ch, ch)
        mask = jnp.where(q0 + i_idx >= k0 + j_idx, 0.0, NEG_INF)
        heads = range(MLA_HEADS_PER_STEP)
        rows = [slice(hh * tq, (hh + 1) * tq) for hh in heads]
        lanes = [slice(hh * HEAD_PAD, (hh + 1) * HEAD_PAD) for hh in heads]
        m_old = [m_ref[r, :] for r in rows]
        acc_old = [acc_ref[r, :] for r in rows]
        m_new, acc_new = [], []
        for hh in heads:
            s = _dot_nt(q_ref[0, :, lanes[hh]], k_ref[0, pl.ds(k0, ch), lanes[hh]]) + mask
            mn = jnp.maximum(m_old[hh], jnp.max(s, axis=1, keepdims=True))
            p = jnp.exp(s - jnp.maximum(mn, MASKED_ROW_FLOOR))
            m_new.append(mn)
            acc_new.append(jnp.exp(m_old[hh] - mn) * acc_old[hh] + _dot(p.astype(BF16), v_ref[0, pl.ds(k0, ch), lanes[hh]]))
        for hh in heads:
            m_ref[rows[hh], :] = m_new[hh]
            acc_ref[rows[hh], :] = acc_new[hh]
        return carry

    lax.fori_loop(0, (q0 + tq + ch - 1) // ch, chunk, 0)
    for hh in range(MLA_HEADS_PER_STEP):
        acc = acc_ref[hh * tq:(hh + 1) * tq, :]
        o_ref[0, :, hh * HEAD_DIM:(hh + 1) * HEAD_DIM] = acc[:, 0:HEAD_DIM] / jnp.maximum(acc[:, HEAD_DIM:HEAD_DIM + 1], 1e-30)


def _mla_prompt(q, k, v):
    b, t, _ = q.shape
    tq = min(MLA_TQ, t)
    assert tq == MLA_TQ and t % MLA_CHUNK == 0
    nh = MLA_HEADS_PER_STEP
    pair = nh * HEAD_PAD
    return pl.pallas_call(
        _mla_prompt_body,
        grid=(b, N_HEADS // nh, t // tq),
        in_specs=[pl.BlockSpec((1, tq, pair), lambda i, h, j: (i, j, h)),
                  pl.BlockSpec((1, t, pair), lambda i, h, j: (i, 0, h)),
                  pl.BlockSpec((1, t, pair), lambda i, h, j: (i, 0, h))],
        out_specs=pl.BlockSpec((1, tq, nh * HEAD_DIM), lambda i, h, j: (i, j, h)),
        out_shape=jax.ShapeDtypeStruct((b, t, N_HEADS * HEAD_DIM), F32),
        scratch_shapes=[pltpu.VMEM((nh * tq, 1), F32), pltpu.VMEM((nh * tq, HEAD_PAD), F32)],
        compiler_params=_cparams("parallel", "parallel", "arbitrary"),
        name="mla_prompt",
    )(q, k, v)


def _mla_absorb_body(q_ref, wuk_ref, o_ref):
    for h in range(N_HEADS):
        qn = q_ref[:, h * HEAD_PAD:h * HEAD_PAD + HEAD_DIM]
        o_ref[:, h * KV_LORA:(h + 1) * KV_LORA] = _dot_nt(qn, wuk_ref[:, h * HEAD_DIM:(h + 1) * HEAD_DIM])


def _mla_unabsorb_body(o_ref, wuv_ref, y_ref):
    for h in range(N_HEADS):
        y_ref[:, h * HEAD_DIM:(h + 1) * HEAD_DIM] = _dot(
            o_ref[:, h * KV_LORA:(h + 1) * KV_LORA].astype(BF16), wuv_ref[:, h * HEAD_DIM:(h + 1) * HEAD_DIM])


def _whole_call(body, out_shape, name, *args):
    return pl.pallas_call(
        body, grid=(1,),
        in_specs=[_const_spec(a.shape) for a in args],
        out_specs=_const_spec(out_shape.shape),
        out_shape=out_shape, compiler_params=_cparams("arbitrary"), name=name)(*args)


def _mla_sample_body(pt_ref, ql_ref, q_ref, new_ref, pool_ref, o_ref, buf_ref, sem_ref, ckv_ref, kr_ref, *, n_pages):
    slot = _paged_prefetch(pool_ref, buf_ref, sem_ref, pt_ref, n_pages)
    ql = ql_ref[0].astype(BF16)
    qr = q_ref[0, :, HEAD_DIM:HEAD_DIM + QK_ROPE]
    new_c = new_ref[0, :, 0:KV_LORA].astype(BF16).astype(F32)
    new_r = new_ref[0, :, KV_LORA:KV_LORA + QK_ROPE].astype(BF16).astype(F32)
    s_new = (jnp.sum(ql.astype(F32) * new_c, axis=1, keepdims=True)
             + jnp.sum(qr.astype(F32) * new_r, axis=1, keepdims=True))
    for p in range(n_pages):
        ckv_ref[:, p * PAGE:(p + 1) * PAGE] = buf_ref[slot, p, 0:KV_LORA, :].astype(BF16)
        kr_ref[:, p * PAGE:(p + 1) * PAGE] = buf_ref[slot, p, KV_LORA:KV_LORA + QK_ROPE, :].astype(BF16)
    s = _dot(ql, ckv_ref[...]) + _dot(qr, kr_ref[...])
    m = jnp.maximum(jnp.max(s, axis=1, keepdims=True), s_new)
    p_old = jnp.exp(s - m)
    p_new = jnp.exp(s_new - m)
    den = jnp.sum(p_old, axis=1, keepdims=True) + p_new
    acc = _dot_nt(p_old.astype(BF16), ckv_ref[...]) + p_new * new_c
    o_ref[0] = acc / den


def _mla_sample(ql, q, lat_new, pool, page_table):
    nb = ql.shape[0]
    n_pages = page_table.shape[1]
    feat = pool.shape[1]
    grid_spec = pltpu.PrefetchScalarGridSpec(
        num_scalar_prefetch=1,
        grid=(nb,),
        in_specs=[pl.BlockSpec((1, N_HEADS, KV_LORA), lambda i, pt: (i, 0, 0)),
                  pl.BlockSpec((1, N_HEADS, HEAD_PAD), lambda i, pt: (i, 0, 0)),
                  pl.BlockSpec((1, 1, lat_new.shape[2]), lambda i, pt: (i, 0, 0)),
                  pl.BlockSpec(memory_space=pl.ANY)],
        out_specs=pl.BlockSpec((1, N_HEADS, KV_LORA), lambda i, pt: (i, 0, 0)),
        scratch_shapes=[pltpu.VMEM((2, n_pages, feat, PAGE), F32), pltpu.SemaphoreType.DMA((2,)),
                        pltpu.VMEM((KV_LORA, n_pages * PAGE), BF16), pltpu.VMEM((QK_ROPE, n_pages * PAGE), BF16)],
    )
    return pl.pallas_call(
        functools.partial(_mla_sample_body, n_pages=n_pages),
        grid_spec=grid_spec,
        out_shape=jax.ShapeDtypeStruct((nb, N_HEADS, KV_LORA), F32),
        compiler_params=_cparams("arbitrary"),
        name="mla_sample",
    )(page_table, ql, q, lat_new, pool)


def _pad_heads(w, parts):
    out = jnp.zeros((w.shape[0], N_HEADS, HEAD_PAD), w.dtype)
    for src, size, dst in parts:
        out = out.at[:, :, dst:dst + size].set(w[:, :, src:src + size])
    return out.reshape(w.shape[0], N_HEADS * HEAD_PAD)


def _mla_layer(xp, xs, cache, page_table, w_dq, q_norm, kv_norm, w_uq, w_uk, w_uv, past_len):
    b, t, d = xp.shape
    nb = xs.shape[0]
    r0, r1 = HEAD_DIM, HEAD_DIM + ROPE_HALF
    keep = [(0, HEAD_DIM, 0), (HEAD_DIM, ROPE_HALF, r0), (r1, ROPE_HALF, r1)]
    swap = [(r1, ROPE_HALF, r0), (HEAD_DIM, ROPE_HALF, r1)]
    base = Q_LORA + KV_LORA
    x1w, x2w = w_dq[:, base:base + ROPE_HALF], w_dq[:, base + ROPE_HALF:base + QK_ROPE]
    z_lo, z_hi = jnp.zeros((d, HEAD_DIM), F32), jnp.zeros((d, HEAD_PAD - HEAD_DIM - QK_ROPE), F32)
    kr_keep = jnp.concatenate([z_lo, x1w, x2w, z_hi], axis=1)
    kr_swap = jnp.concatenate([z_lo, x2w, x1w, z_hi], axis=1)
    wdq = jnp.concatenate([w_dq[:, :base], kr_keep, kr_swap], axis=1).astype(BF16)
    wq3 = w_uq.reshape(Q_LORA, N_HEADS, HEAD_DIM + QK_ROPE)
    wq = _pad_heads(wq3, keep).astype(BF16)
    wqs = _pad_heads(wq3, swap).astype(BF16)
    wk = _pad_heads(w_uk, [(0, HEAD_DIM, 0)]).astype(BF16)
    wv = w_uv.reshape(KV_LORA, N_HEADS * HEAD_DIM).astype(BF16)
    wv_pad = _pad_heads(w_uv, [(0, HEAD_DIM, 0)]).astype(BF16)
    vone = jnp.asarray((np.arange(N_HEADS * HEAD_PAD) % HEAD_PAD == HEAD_DIM).astype(np.float32)).reshape(1, -1)
    consts = [wdq, q_norm.reshape(1, -1), kv_norm.reshape(1, -1), wq, wqs, wk, wv_pad, vone]

    def rope_tables(pos):
        inv = ROPE_THETA ** (-jnp.arange(ROPE_HALF, dtype=F32) / ROPE_HALF)
        ang = pos.astype(F32)[:, None] * inv[None, :]
        cos, sin = jnp.cos(ang), jnp.sin(ang)
        n = pos.shape[0]
        c = jnp.concatenate([jnp.ones((n, HEAD_DIM), F32), cos, cos, jnp.zeros((n, HEAD_PAD - r1 - ROPE_HALF), F32)], axis=1)
        s = jnp.concatenate([jnp.zeros((n, HEAD_DIM), F32), -sin, sin, jnp.zeros((n, HEAD_PAD - r1 - ROPE_HALF), F32)], axis=1)
        return c, s

    cos_p, sin_p = rope_tables(jnp.arange(t))
    q, k, v, _, lat_p_t = _mla_project(xp.reshape(b * t, d), cos_p, sin_p, consts, b)
    wide = N_HEADS * HEAD_PAD
    attn_p = _mla_prompt(q.reshape(b, t, wide), k.reshape(b, t, wide), v.reshape(b, t, wide))
    cos_s, sin_s = rope_tables(jnp.full((nb,), past_len, jnp.int32))
    qs, _, _, lat_s, lat_s_t = _mla_project(xs.reshape(nb, d), cos_s, sin_s, consts, 1)
    wuk2 = w_uk.reshape(KV_LORA, N_HEADS * HEAD_DIM).astype(BF16)
    ql = _whole_call(_mla_absorb_body, jax.ShapeDtypeStruct((nb, N_HEADS * KV_LORA), F32), "mla_absorb", qs, wuk2)
    o_lat = _mla_sample(ql.reshape(nb, N_HEADS, KV_LORA), qs.reshape(nb, N_HEADS, HEAD_PAD),
                        lat_s.reshape(nb, 1, -1), _feature_major_pool(cache), page_table)
    attn_s = _whole_call(_mla_unabsorb_body, jax.ShapeDtypeStruct((nb, d), F32), "mla_unabsorb",
                         o_lat.reshape(nb, N_HEADS * KV_LORA), wv)
    return (attn_p.reshape(b * t, d), attn_s, jnp.transpose(lat_p_t, (0, 2, 1)),
            jnp.transpose(lat_s_t, (2, 0, 1)))


CMP_HIDDEN = 2 * HEAD_DIM
KV_PAIR = 2 * HEAD_DIM


def _nsa_pe_body(pe_ref, w1_ref, o_ref):
    for c in range(2):
        o_ref[:, c * CMP_HIDDEN:(c + 1) * CMP_HIDDEN] = _dot(pe_ref[c], w1_ref[c])


def _compress(load_rows, n_h, wblk_ref, peh_ref, w2k_ref, w2v_ref):
    hid = jnp.zeros((n_h, 4 * CMP_HIDDEN), F32)
    for s in range(0, CMP_STRIDE, 2):
        rows = jnp.concatenate([load_rows(s), load_rows(s + 1)], axis=1).astype(BF16)
        hid = hid + _dot(rows, wblk_ref[s // 2])
    up = lambda x: pltpu.roll(x, n_h - 1, 0)
    peh = peh_ref[0:1, :]
    hk = peh[:, 0:CMP_HIDDEN] + hid[:, 0:CMP_HIDDEN] + up(hid[:, CMP_HIDDEN:2 * CMP_HIDDEN])
    hv = peh[:, CMP_HIDDEN:] + hid[:, 2 * CMP_HIDDEN:3 * CMP_HIDDEN] + up(hid[:, 3 * CMP_HIDDEN:])
    return _dot(jax.nn.gelu(hk).astype(BF16), w2k_ref[...]) + _dot(jax.nn.gelu(hv).astype(BF16), w2v_ref[...])


def _nsa_compress_body(cmp_ref, wblk_ref, peh_ref, w2k_ref, w2v_ref, o_ref):
    n_h = o_ref.shape[1]
    o_ref[0] = _compress(lambda s: cmp_ref[0, pl.ds(s, n_h, stride=CMP_STRIDE), :], n_h,
                         wblk_ref, peh_ref, w2k_ref, w2v_ref)


def _nsa_compress_prompt(cmp, consts):
    b, t, _ = cmp.shape
    n_h = t // CMP_STRIDE
    return pl.pallas_call(
        _nsa_compress_body,
        grid=(b,),
        in_specs=[pl.BlockSpec((1, t, KV_PAIR), lambda i: (i, 0, 0))] + [_const_spec(c.shape) for c in consts],
        out_specs=pl.BlockSpec((1, n_h, KV_PAIR), lambda i: (i, 0, 0)),
        out_shape=jax.ShapeDtypeStruct((b, n_h, KV_PAIR), F32),
        compiler_params=_cparams("parallel"),
        name="nsa_compress",
    )(cmp, *consts)


def _intersect_matrix(n_c, n_s):
    n = lax.broadcasted_iota(jnp.int32, (n_c, n_s), 0) * CMP_STRIDE
    j = lax.broadcasted_iota(jnp.int32, (n_c, n_s), 1) * SLC_BLOCK
    return ((n < j + SLC_BLOCK) & (n + CMP_BLOCK > j)).astype(BF16)


def _split_dot(x, w):
    hi = x.astype(BF16)
    lo = (x - hi.astype(F32)).astype(BF16)
    return _dot(hi, w) + _dot(lo, w)


def _nsa_cmp_bias_body(base_ref, o_ref):
    n_c = o_ref.shape[1]
    for c in range(CMP_STRIDE):
        o_ref[0, :, c * n_c:(c + 1) * n_c] = _toeplitz(base_ref[0, c:c + 1, :], n_c, 1)[:, 0:n_c]


def _nsa_cmp_bias(rel_bias, t):
    n_c = t // CMP_STRIDE
    u = np.arange(2 * n_c)[None, :]
    c = np.arange(CMP_STRIDE)[:, None]
    dist = np.where(u >= n_c, CMP_STRIDE * (2 * n_c - 1 - u) + c - (CMP_BLOCK - 1), -1)
    base = _bias_by_distance(rel_bias, dist)
    out = pl.pallas_call(
        _nsa_cmp_bias_body,
        grid=(N_HEADS,),
        in_specs=[pl.BlockSpec((1, CMP_STRIDE, 2 * n_c), lambda h: (h, 0, 0))],
        out_specs=pl.BlockSpec((1, n_c, CMP_STRIDE * n_c), lambda h: (h, 0, 0)),
        out_shape=jax.ShapeDtypeStruct((N_HEADS, n_c, CMP_STRIDE * n_c), F32),
        compiler_params=_cparams("parallel"),
        name="nsa_cmp_bias",
    )(base)
    return out.reshape(N_HEADS, t, n_c)


NSA_PCHUNK = 1024
NSA_WKEYS = NSA_WINDOW + Q_TILE


def _value_ones(kv):
    lane = lax.broadcasted_iota(jnp.int32, kv.shape, 1)
    rolled = pltpu.roll(kv, HEAD_DIM, 1)
    return jnp.where(lane < HEAD_DIM, rolled, jnp.where(lane == HEAD_DIM, 1.0, 0.0)).astype(BF16)


def _value_ones_t(v_t):
    first = lax.broadcasted_iota(jnp.int32, v_t.shape, 0) == 0
    return jnp.concatenate([v_t, jnp.where(first, 1.0, 0.0)], axis=0).astype(BF16)


def _nsa_bias_tiles_body(rev_ref, o_ref, base_ref, *, nq):
    delta = pl.program_id(0)
    tq = Q_TILE
    width = o_ref.shape[3]
    for t in range(width // tq + 1):
        base_ref[:, t * tq:(t + 1) * tq] = rev_ref[nq - 1 - delta + t]
    for h in range(N_HEADS):
        o_ref[0, h] = _toeplitz(base_ref[h:h + 1, :], tq, width + 1)[:, 0:width]


def _nsa_bias_tiles(rev, nq, n_delta, width):
    return pl.pallas_call(
        functools.partial(_nsa_bias_tiles_body, nq=nq),
        grid=(n_delta,),
        in_specs=[_const_spec(rev.shape)],
        out_specs=pl.BlockSpec((1, N_HEADS, Q_TILE, width), lambda i: (i, 0, 0, 0)),
        out_shape=jax.ShapeDtypeStruct((n_delta, N_HEADS, Q_TILE, width), F32),
        scratch_shapes=[pltpu.VMEM((N_HEADS, width + Q_TILE), F32)],
        compiler_params=_cparams("parallel"),
        name="nsa_bias_tiles",
    )(rev)


def _nsa_prompt_body(q_ref, gate_ref, slc_ref, win_ref, kvc_ref, biasc_ref, tile_ref, wtile_ref, o_ref,
                     qst_ref, oc_ref, ms_ref, accs_ref, sel_ref, *, n_s):
    qi = pl.program_id(1)
    c = pl.program_id(2)
    tq, ch = Q_TILE, NSA_PCHUNK
    n_c = kvc_ref.shape[1]
    q0 = qi * tq
    rows_all = N_HEADS * tq

    @pl.when(c == 0)
    def _():
        for h in range(N_HEADS):
            qst_ref[h * tq:(h + 1) * tq, :] = (
                q_ref[0, :, h * HEAD_DIM:(h + 1) * HEAD_DIM] * (HEAD_DIM ** -0.5)).astype(BF16)

        kvc = kvc_ref[0]
        kcb = kvc[:, 0:HEAD_DIM].astype(BF16)
        vc = _value_ones(kvc)
        qpos_c = q0 + lax.broadcasted_iota(jnp.int32, (tq, n_c), 0)
        cend = lax.broadcasted_iota(jnp.int32, (tq, n_c), 1) * CMP_STRIDE + CMP_BLOCK
        mask_c = jnp.where(cend <= qpos_c + 1, 0.0, NEG_INF)
        s = (_dot_nt(qst_ref[...], kcb).reshape(N_HEADS, tq, n_c) + biasc_ref[...] + mask_c[None]).reshape(rows_all, n_c)
        m = jnp.maximum(jnp.max(s, axis=1, keepdims=True), MASKED_ROW_FLOOR)
        e = jnp.exp(s - m)
        p = e / jnp.maximum(jnp.sum(e, axis=1, keepdims=True), 1e-30)
        oc_ref[...] = _dot(p.astype(BF16), vc)[:, 0:HEAD_DIM]
        n_sp = sel_ref.shape[1]
        imp = _split_dot(jnp.sum(p.reshape(N_HEADS, tq, n_c), axis=0), _intersect_matrix(n_c, n_sp))

        qblk = (q0 + lax.broadcasted_iota(jnp.int32, (tq, n_sp), 0)) >> 6
        jb = lax.broadcasted_iota(jnp.int32, (tq, n_sp), 1)
        forced = (jb == 0) | (jb == qblk) | (jb == qblk - 1)
        score = jnp.where(jb <= qblk, imp + jnp.where(forced, FORCE_BONUS, 0.0), NEG_INF)
        score_t = score.T[0:n_s, :]
        blk_t = lax.broadcasted_iota(jnp.int32, (n_s, tq), 0)
        rank_t = jnp.zeros((n_s, tq), F32)
        for j in range(n_s):
            row = score_t[j:j + 1, :]
            rank_t = rank_t + ((row > score_t) | ((row == score_t) & (blk_t > j))).astype(F32)
        sel_t = jnp.concatenate([(rank_t < N_SELECT).astype(F32), jnp.zeros((n_sp - n_s, tq), F32)], axis=0)
        sel_ref[...] = sel_t.T.astype(BF16)
        ms_ref[...] = jnp.full(ms_ref.shape, NEG_INF, F32)
        accs_ref[...] = jnp.zeros(accs_ref.shape, F32)

    @pl.when(c * ch < q0 + tq)
    def _():
        k0 = c * ch
        i_idx = lax.broadcasted_iota(jnp.int32, (tq, ch), 0)
        j_idx = lax.broadcasted_iota(jnp.int32, (tq, ch), 1)
        e_row = lax.broadcasted_iota(jnp.int32, (sel_ref.shape[1], ch), 0)
        e_lane = lax.broadcasted_iota(jnp.int32, (sel_ref.shape[1], ch), 1)
        expand = (((k0 + e_lane) >> 6) == e_row).astype(BF16)
        valid = (_dot(sel_ref[...], expand) > 0.5) & (q0 + i_idx >= k0 + j_idx)
        mask = jnp.where(valid, 0.0, NEG_INF)

        @pl.when(jnp.max(mask) > -1.0)
        def _():
            kv = slc_ref[0, :, pl.ds(pl.multiple_of(k0, ch), ch)]
            kb = kv[0:HEAD_DIM, :].astype(BF16)
            va = _value_ones_t(kv[HEAD_DIM:KV_PAIR, :])
            s = (_dot(qst_ref[...], kb).reshape(N_HEADS, tq, ch) + tile_ref[0] + mask[None]).reshape(N_HEADS * tq, ch)
            m = ms_ref[...]
            mn = jnp.maximum(m, jnp.max(s, axis=1, keepdims=True))
            p = jnp.exp(s - jnp.maximum(mn, MASKED_ROW_FLOOR))
            ms_ref[...] = mn
            accs_ref[...] = jnp.exp(m - mn) * accs_ref[...] + _dot_nt(p.astype(BF16), va)

    @pl.when(c == pl.num_programs(2) - 1)
    def _():
        wk = NSA_WKEYS
        k0w = jnp.maximum(q0 - NSA_WINDOW, 0)
        dist = ((q0 - k0w) + lax.broadcasted_iota(jnp.int32, (tq, wk), 0)
                - lax.broadcasted_iota(jnp.int32, (tq, wk), 1))
        mask_w = jnp.where((dist >= 0) & (dist <= NSA_WINDOW), 0.0, NEG_INF)
        kvw = win_ref[0, :, pl.ds(pl.multiple_of(k0w, tq), wk)]
        kwb = kvw[0:HEAD_DIM, :].astype(BF16)
        vwa = _value_ones_t(kvw[HEAD_DIM:KV_PAIR, :])
        s = (_dot(qst_ref[...], kwb).reshape(N_HEADS, tq, wk) + wtile_ref[0] + mask_w[None]).reshape(rows_all, wk)
        m = jnp.maximum(jnp.max(s, axis=1, keepdims=True), MASKED_ROW_FLOOR)
        acc_w = _dot_nt(jnp.exp(s - m).astype(BF16), vwa)
        o_w = acc_w[:, 0:HEAD_DIM] / jnp.maximum(acc_w[:, HEAD_DIM:HEAD_DIM + 1], 1e-30)
        acc_s = accs_ref[...]
        o_s = acc_s[:, 0:HEAD_DIM] / jnp.maximum(acc_s[:, HEAD_DIM:HEAD_DIM + 1], 1e-30)
        gates = jax.nn.sigmoid(gate_ref[0])
        for h in range(N_HEADS):
            rows = slice(h * tq, (h + 1) * tq)
            o_ref[0, :, h * HEAD_DIM:(h + 1) * HEAD_DIM] = (
                gates[:, h:h + 1] * oc_ref[rows, :] + gates[:, N_HEADS + h:N_HEADS + h + 1] * o_s[rows, :]
                + gates[:, 2 * N_HEADS + h:2 * N_HEADS + h + 1] * o_w[rows, :])


def _nsa_prompt(q, gate, kv_t, win_t, kvc, bias_c, tiles, wtiles):
    b, t, d = q.shape
    tq, ch = Q_TILE, NSA_PCHUNK
    n_c = kvc.shape[1]
    n_s = t // SLC_BLOCK
    rows = N_HEADS * tq
    assert t % ch == 0 and t >= NSA_WKEYS
    step = ch // tq

    def tile_index(i, j, c):
        return (j - step * jnp.minimum(c, (j * tq + tq - 1) // ch), 0, 0, 0)

    return pl.pallas_call(
        functools.partial(_nsa_prompt_body, n_s=n_s),
        grid=(b, t // tq, t // ch),
        in_specs=[pl.BlockSpec((1, tq, d), lambda i, j, c: (i, j, 0)),
                  pl.BlockSpec((1, tq, gate.shape[2]), lambda i, j, c: (i, j, 0)),
                  pl.BlockSpec((1, KV_PAIR, t), lambda i, j, c: (i, 1, 0)),
                  pl.BlockSpec((1, KV_PAIR, t), lambda i, j, c: (i, 0, 0)),
                  pl.BlockSpec((1, n_c, KV_PAIR), lambda i, j, c: (i, 0, 0)),
                  pl.BlockSpec((N_HEADS, tq, n_c), lambda i, j, c: (0, j, 0)),
                  pl.BlockSpec((1, N_HEADS, tq, ch), tile_index),
                  pl.BlockSpec((1, N_HEADS, tq, NSA_WKEYS), lambda i, j, c: (jnp.minimum(j, NSA_WINDOW // tq), 0, 0, 0))],
        out_specs=pl.BlockSpec((1, tq, d), lambda i, j, c: (i, j, 0)),
        out_shape=jax.ShapeDtypeStruct((b, t, d), F32),
        scratch_shapes=[pltpu.VMEM((rows, HEAD_DIM), BF16), pltpu.VMEM((rows, HEAD_DIM), F32),
                        pltpu.VMEM((rows, 1), F32), pltpu.VMEM((rows, KV_PAIR), F32),
                        pltpu.VMEM((tq, -(-n_s // 128) * 128), BF16)],
        compiler_params=_cparams("parallel", "arbitrary", "arbitrary"),
        name="nsa_prompt",
    )(q, gate, kv_t, win_t, kvc, bias_c, tiles, wtiles)


def _nsa_sample_body(pt_ref, q_ref, gate_ref, new_ref, wnew_ref, cwin_ref, bc_ref, bs_ref, bw_ref, b0_ref,
                     wblk_ref, peh_ref, w2k_ref, w2v_ref, pool_ref, o_ref, wout_ref, buf_ref, sem_ref, cmp_ref,
                     ks_ref, vs_ref, expand_ref, *, n_pages, n_sp):
    slot = _paged_prefetch(pool_ref, buf_ref, sem_ref, pt_ref, n_pages)
    b = pl.program_id(0)
    past = n_pages * PAGE
    n_h = past // CMP_STRIDE
    q = (q_ref[0] * (HEAD_DIM ** -0.5)).astype(BF16)
    qf = q.astype(F32)

    for p in range(n_pages):
        cmp_ref[p * PAGE:(p + 1) * PAGE, :] = buf_ref[slot, p, 0:KV_PAIR, :].T
    kvc = _compress(lambda s: cmp_ref[pl.ds(s, n_h, stride=CMP_STRIDE), :], n_h, wblk_ref, peh_ref, w2k_ref, w2v_ref)
    kc = kvc[:, 0:HEAD_DIM].astype(BF16)
    vc = kvc[:, HEAD_DIM:KV_PAIR].astype(BF16)
    cend = lax.broadcasted_iota(jnp.int32, (N_HEADS, n_h), 1) * CMP_STRIDE + CMP_BLOCK
    valid_c = cend <= past + 1
    s = jnp.where(valid_c, _dot_nt(q, kc) + bc_ref[...], NEG_INF)
    m = jnp.max(s, axis=1, keepdims=True)
    e = jnp.where(valid_c, jnp.exp(s - m), 0.0)
    p_c = e / jnp.maximum(jnp.sum(e, axis=1, keepdims=True), 1e-30)
    o_c = _dot(p_c.astype(BF16), vc)

    psum = jnp.broadcast_to(jnp.sum(p_c, axis=0, keepdims=True), (8, n_h))
    imp = _split_dot(psum, _intersect_matrix(n_h, n_sp))[0:1, :]
    qblk = past // SLC_BLOCK
    jb = lax.broadcasted_iota(jnp.int32, (1, n_sp), 1)
    forced = (jb == 0) | (jb == qblk) | (jb == qblk - 1)
    score = jnp.where(jb <= qblk, imp + jnp.where(forced, FORCE_BONUS, 0.0), NEG_INF)
    r_idx = lax.broadcasted_iota(jnp.int32, (n_sp, n_sp), 0)
    c_idx = lax.broadcasted_iota(jnp.int32, (n_sp, n_sp), 1)
    score_b = jnp.broadcast_to(score, (n_sp, n_sp))
    score_col = jnp.sum(jnp.where(r_idx == c_idx, score_b, 0.0), axis=1, keepdims=True)
    beats = (score_col > score_b) | ((score_col == score_b) & (r_idx < c_idx))
    rank = jnp.sum(beats.astype(F32), axis=0, keepdims=True)
    sel = jnp.broadcast_to((rank < N_SELECT).astype(BF16), (8, n_sp))
    sel_new = rank[:, qblk:qblk + 1] < N_SELECT

    def attend(s_old, v_old_t, kv_new, new_ok):
        k_new = kv_new[:, 0:HEAD_DIM].astype(BF16).astype(F32)
        v_new = kv_new[:, HEAD_DIM:KV_PAIR].astype(BF16).astype(F32)
        s_new = jnp.where(new_ok, jnp.sum(qf * k_new, axis=1, keepdims=True) + b0_ref[:, 0:1], NEG_INF)
        m = jnp.maximum(jnp.maximum(jnp.max(s_old, axis=1, keepdims=True), s_new), MASKED_ROW_FLOOR)
        p_old = jnp.exp(s_old - m)
        p_new = jnp.exp(s_new - m)
        den = jnp.sum(p_old, axis=1, keepdims=True) + p_new
        acc = _dot_nt(p_old.astype(BF16), v_old_t) + p_new.astype(BF16).astype(F32) * v_new
        return acc / jnp.maximum(den, 1e-30)

    @pl.when(b == 0)
    def _():
        blk = lax.broadcasted_iota(jnp.int32, expand_ref.shape, 1) >> 6
        expand_ref[...] = (blk == lax.broadcasted_iota(jnp.int32, expand_ref.shape, 0)).astype(BF16)

    for p in range(n_pages):
        lanes = slice(p * PAGE, (p + 1) * PAGE)
        ks_ref[:, lanes] = buf_ref[slot, p, KV_PAIR:KV_PAIR + HEAD_DIM, :].astype(BF16)
        vs_ref[:, lanes] = buf_ref[slot, p, KV_PAIR + HEAD_DIM:2 * KV_PAIR, :].astype(BF16)
    picked = _dot(sel, expand_ref[...])[0:1, :] > 0.5
    s = _dot(q, ks_ref[...]) + bs_ref[...] + jnp.where(picked, 0.0, NEG_INF)
    o_s = attend(s, vs_ref[...], new_ref[0, 0:1, :], jnp.broadcast_to(sel_new, (N_HEADS, 1)))

    cwin = cwin_ref[0]
    s = _dot(q, cwin[0:HEAD_DIM, :].astype(BF16)) + bw_ref[...]
    o_w = attend(s, cwin[HEAD_DIM:KV_PAIR, :].astype(BF16), new_ref[0, 1:2, :], jnp.full((N_HEADS, 1), True))

    gates = jax.nn.sigmoid(gate_ref[0])
    o_ref[0] = gates[:, 0:1] * o_c + gates[:, 1:2] * o_s + gates[:, 2:3] * o_w

    wb = cwin.shape[1]
    pick = lax.broadcasted_iota(jnp.int32, wnew_ref.shape, 1) == b
    new_col = jnp.sum(jnp.where(pick, wnew_ref[...], 0.0), axis=1, keepdims=True)
    last = lax.broadcasted_iota(jnp.int32, cwin.shape, 1) == wb - 1
    wout_ref[0] = jnp.where(last, new_col, pltpu.roll(cwin, wb - 1, 1))


def _nsa_sample(q, gate_t, new, wnew_t, cwin_t, biases, consts, pool, page_table):
    nb = q.shape[0]
    n_pages = page_table.shape[1]
    past = n_pages * PAGE
    assert cwin_t.shape[2] <= NSA_WINDOW
    n_s = past // SLC_BLOCK + 1
    n_sp = -(-n_s // 128) * 128
    per_b = lambda a: pl.BlockSpec((1,) + a.shape[1:], lambda i, pt: (i,) + (0,) * (a.ndim - 1))
    const = lambda a: pl.BlockSpec(a.shape, lambda i, pt: (0,) * a.ndim)
    grid_spec = pltpu.PrefetchScalarGridSpec(
        num_scalar_prefetch=1,
        grid=(nb,),
        in_specs=[per_b(q), per_b(gate_t), per_b(new), const(wnew_t), per_b(cwin_t)]
        + [const(a) for a in biases] + [const(a) for a in consts]
        + [pl.BlockSpec(memory_space=pl.ANY)],
        out_specs=[pl.BlockSpec((1, N_HEADS, HEAD_DIM), lambda i, pt: (i, 0, 0)), per_b(cwin_t)],
        scratch_shapes=[pltpu.VMEM((2, n_pages, pool.shape[1], PAGE), F32), pltpu.SemaphoreType.DMA((2,)),
                        pltpu.VMEM((past, KV_PAIR), F32), pltpu.VMEM((HEAD_DIM, past), BF16),
                        pltpu.VMEM((HEAD_DIM, past), BF16), pltpu.VMEM((n_sp, past), BF16)],
    )
    return pl.pallas_call(
        functools.partial(_nsa_sample_body, n_pages=n_pages, n_sp=n_sp),
        grid_spec=grid_spec,
        out_shape=[jax.ShapeDtypeStruct((nb, N_HEADS, HEAD_DIM), F32), jax.ShapeDtypeStruct(cwin_t.shape, F32)],
        compiler_params=_cparams("arbitrary"),
        name="nsa_sample",
    )(page_table, q, gate_t, new, wnew_t, cwin_t, *biases, *consts, pool)


def _nsa_layer(xp, xs, cache_kv, cache_win, page_table, w_in, pe, w1, w2, rel_bias):
    b, t, d = xp.shape
    nb = xs.shape[0]
    past = page_table.shape[1] * PAGE
    w_q, w_gate = w_in[:, :d].astype(BF16), w_in[:, d + 3 * KV_PAIR:].astype(BF16)
    w_cmp, w_slc, w_win = [w_in[:, d + i * KV_PAIR:d + (i + 1) * KV_PAIR].astype(BF16) for i in range(3)]
    wts = [w_in[:, d:d + 2 * KV_PAIR].T.astype(BF16), w_in[:, d + 2 * KV_PAIR:d + 3 * KV_PAIR].T.astype(BF16)]

    w1r = w1.reshape(2, 2, CMP_STRIDE, HEAD_DIM, CMP_HIDDEN)
    zero = jnp.zeros((CMP_STRIDE, HEAD_DIM, CMP_HIDDEN), F32)
    top = jnp.concatenate([w1r[0, 0], w1r[0, 1], zero, zero], axis=2)
    bot = jnp.concatenate([zero, zero, w1r[1, 0], w1r[1, 1]], axis=2)
    wblk = jnp.concatenate([top, bot], axis=1).astype(BF16).reshape(CMP_STRIDE // 2, 2 * KV_PAIR, -1)
    pe8 = jnp.broadcast_to(pe.reshape(2, 1, -1), (2, 8, CMP_BLOCK * HEAD_DIM)).astype(BF16)
    peh = _whole_call(_nsa_pe_body, jax.ShapeDtypeStruct((8, 2 * CMP_HIDDEN), F32), "nsa_pe", pe8, w1.astype(BF16))
    zpad = jnp.zeros((CMP_HIDDEN, HEAD_DIM), F32)
    w2k = jnp.concatenate([w2[0], zpad], axis=1).astype(BF16)
    w2v = jnp.concatenate([zpad, w2[1]], axis=1).astype(BF16)
    consts = [wblk, peh, w2k, w2v]

    qp, cmp_p, gate_p, kv_t, win_t = _proj(xp.reshape(b * t, d), [w_q, w_cmp, w_gate], wts, b)
    r3 = lambda a: a.reshape(b, t, -1)
    kvc = _nsa_compress_prompt(r3(cmp_p), consts)
    nq = t // Q_TILE
    bias_c = _nsa_cmp_bias(rel_bias, t)
    n_rev = nq + max(NSA_WKEYS, NSA_PCHUNK) // Q_TILE
    rev = _bias_by_distance(rel_bias, Q_TILE * nq - 1 - np.arange(n_rev * Q_TILE))
    rev = jnp.transpose(rev.reshape(N_HEADS, n_rev, Q_TILE), (1, 0, 2))
    tiles = _nsa_bias_tiles(rev, nq, nq, NSA_PCHUNK)
    wtiles = _nsa_bias_tiles(rev, nq, NSA_WINDOW // Q_TILE + 1, NSA_WKEYS)
    attn_p = _nsa_prompt(r3(qp), r3(gate_p), kv_t, win_t, kvc, bias_c, tiles, wtiles)
    kv_p = _token_major(kv_t, (4, 1, HEAD_DIM))
    win_out_p = _token_major(win_t[:, :, t - min(NSA_WINDOW, t):], (2, 1, HEAD_DIM))

    qs, slc_s, win_s, gate_s, kvs_t, wins_t = _proj(xs.reshape(nb, d), [w_q, w_slc, w_win, w_gate], wts, 1)
    wb = cache_win.shape[1]
    n_h = past // CMP_STRIDE
    bc = _bias_by_distance(rel_bias, past - (np.arange(n_h) * CMP_STRIDE + CMP_BLOCK - 1))
    bs = _bias_by_distance(rel_bias, past - np.arange(past))
    bw = _bias_by_distance(rel_bias, wb - np.arange(wb))
    b0 = _bias_by_distance(rel_bias, np.zeros((128,), np.int64))
    gate_t = jnp.transpose(gate_s.reshape(nb, 3, N_HEADS), (0, 2, 1))
    new = jnp.stack([slc_s, win_s], axis=1)
    cwin_t = jnp.transpose(cache_win, (0, 2, 3, 4, 1)).reshape(nb, KV_PAIR, wb)
    attn_s, wout_t = _nsa_sample(qs.reshape(nb, N_HEADS, HEAD_DIM), gate_t, new, wins_t[0], cwin_t,
                                 [bc, bs, bw, b0], consts, _feature_major_pool(cache_kv), page_table)
    kv_s = jnp.transpose(kvs_t[0], (1, 0)).reshape(nb, 1, 4, 1, HEAD_DIM)
    win_out_s = _token_major(wout_t, (2, 1, HEAD_DIM))
    return attn_p.reshape(b * t, d), attn_s.reshape(nb, d), kv_p, kv_s, win_out_p, win_out_s


def kernel(x_prompt, x_sample, cache_nsa_kv, cache_nsa_win, cache_mla, state_dil_w128, state_dil_w512,
           state_dil_w2048, cache_sb_kv, page_table, p_prompt, p_sample, rel_bias, ln1_g, ln1_b, ln2_g, ln2_b,
           ffn_wg, ffn_wu, ffn_wd, ple_wg, ple_wp, nsa_w_in, nsa_cmp_pe, nsa_cmp_w1, nsa_cmp_w2, nsa_w_out,
           mla_w_dq, mla_q_norm, mla_kv_norm, mla_w_uq, mla_w_uk, mla_w_uv, mla_w_out, dil_w_in, dil_w_out,
           sb_w_in, sb_w_out):
    b, t, d = x_prompt.shape
    nb = x_sample.shape[0]
    past_len = page_table.shape[1] * PAGE
    depth = p_prompt.shape[0]
    n_mixers = 4
    dil_states = (state_dil_w128, state_dil_w512, state_dil_w2048)
    xp = x_prompt.reshape(b * t, d)
    xs = x_sample.reshape(nb, d)
    pp_all = p_prompt.reshape(depth, b * t, -1)
    ps_all = p_sample.reshape(depth, nb, -1)
    outs = {k: [] for k in ("nsa_kv_p", "nsa_kv_s", "nsa_win_p", "nsa_win_s", "mla_p", "mla_s", "sb_p", "sb_s")}
    dil_p = [[] for _ in DIL_PATTERNS]
    dil_s = [[] for _ in DIL_PATTERNS]
    for i in range(depth):
        kind, j = i % n_mixers, i // n_mixers
        xp3, xs3 = xp.reshape(b, t, d), xs.reshape(nb, 1, d)
        if kind == 0:
            mp, ms, a, b_, c, e = _nsa_layer(xp3, xs3, cache_nsa_kv[j], cache_nsa_win[j], page_table, nsa_w_in[j],
                                            nsa_cmp_pe[j], nsa_cmp_w1[j], nsa_cmp_w2[j], rel_bias)
            mp, ms, w_out = [mp], [ms], nsa_w_out[j]
            outs["nsa_kv_p"].append(a)
            outs["nsa_kv_s"].append(b_)
            outs["nsa_win_p"].append(c)
            outs["nsa_win_s"].append(e)
        elif kind == 1:
            mp, ms, a, b_ = _mla_layer(xp3, xs3, cache_mla[j], page_table, mla_w_dq[j], mla_q_norm[j], mla_kv_norm[j],
                                       mla_w_uq[j], mla_w_uk[j], mla_w_uv[j], past_len)
            mp, ms, w_out = [mp], [ms], mla_w_out[j]
            outs["mla_p"].append(a)
            outs["mla_s"].append(b_)
        elif kind == 2:
            mp, ms, st_p, st_s = _dil_layer(xp3, xs3, [s[j] for s in dil_states], dil_w_in[j], rel_bias)
            ms, w_out = [ms], dil_w_out[j]
            for g in range(len(DIL_PATTERNS)):
                dil_p[g].append(st_p[g])
                dil_s[g].append(st_s[g])
        else:
            mp, ms, a, b_ = _sb_layer(xp3, xs3, cache_sb_kv[j], page_table, sb_w_in[j])
            mp, ms, w_out = [mp], [ms], sb_w_out[j]
            outs["sb_p"].append(a)
            outs["sb_s"].append(b_)
        row = lambda v: v.reshape(1, -1)
        consts = (w_out.astype(BF16), row(ln1_g[i]), row(ln1_b[i]), row(ln2_g[i]), row(ln2_b[i]),
                  ffn_wg[i].astype(BF16), ffn_wu[i].astype(BF16), ffn_wd[i].astype(BF16),
                  ple_wg[i].astype(BF16), ple_wp[i].astype(BF16))
        xp = _tail(mp, xp, pp_all, i, *consts)
        xs = _tail(ms, xs, ps_all, i, *consts)
    st = jnp.stack
    return (xp.reshape(b, t, d), xs.reshape(nb, 1, d),
            st(outs["nsa_kv_p"]), st(outs["nsa_kv_s"]), st(outs["nsa_win_p"]), st(outs["nsa_win_s"]),
            st(outs["mla_p"]), st(outs["mla_s"]),
            st(dil_p[0]), st(dil_s[0]), st(dil_p[1]), st(dil_s[1]), st(dil_p[2]), st(dil_s[2]),
            st(outs["sb_p"]), st(outs["sb_s"]))
```
